```python
import jax, jax.numpy as jnp
from jax import lax
import numpy as np

D_MODEL = 1024
BATCH = 8
SEQ = 8192
DEPTH = 2

HEAD_DIM = 64
N_HEADS_A = 8
N_HEADS_B = 8
N_KV_B = 2
GQA_GROUP = N_HEADS_B // N_KV_B
MIX_A = N_HEADS_A * HEAD_DIM
MIX_B = N_HEADS_B * HEAD_DIM
KV_B = N_KV_B * HEAD_DIM
PROJ_WIDTH = 3 * MIX_A + MIX_B + 2 * KV_B
SPLITS = (MIX_A, 2 * MIX_A, 3 * MIX_A, 3 * MIX_A + MIX_B, 3 * MIX_A + MIX_B + KV_B)
DILATED_CONFIGS = ((128, 1), (512, 4), (2048, 16))
SWA_WINDOW = 128
BLOCK = 128
ROPE_THETA = 500000.0
ROPE_DIM = HEAD_DIM // 4
N_GROUPS = 4
EXPERTS_PER_GROUP = 8
N_EXPERTS = N_GROUPS * EXPERTS_PER_GROUP
TOP_K = 2
D_EXPERT = 512
EPS = 1e-6

kernel_name = 'hybrid_dilated_swa_sink_hmoe'


def rmsnorm(x, g):
    xf = x.astype(jnp.float32)
    y = xf * lax.rsqrt(jnp.mean(xf * xf, axis=-1, keepdims=True) + EPS)
    return (y * g.astype(jnp.float32)).astype(x.dtype)


def rope_tables(positions):
    inv_freq = ROPE_THETA ** (-jnp.arange(0, ROPE_DIM, 2, dtype=jnp.float32) / ROPE_DIM)
    ang = positions.astype(jnp.float32)[..., None] * inv_freq
    return jnp.cos(ang)[:, :, None, :], jnp.sin(ang)[:, :, None, :]


def apply_partial_rope(x, cos, sin):
    half = ROPE_DIM // 2
    x1 = x[..., :half].astype(jnp.float32)
    x2 = x[..., half:ROPE_DIM].astype(jnp.float32)
    rot = jnp.concatenate([x1 * cos - x2 * sin, x2 * cos + x1 * sin], axis=-1).astype(x.dtype)
    return jnp.concatenate([rot, x[..., ROPE_DIM:]], axis=-1)


def banded_attention(q, k, v, max_back, sink=None):
    n, hd = q.shape[-2], q.shape[-1]
    nb = -(-n // BLOCK)
    pad = nb * BLOCK - n
    if pad:
        q = jnp.pad(q, [(0, 0)] * (q.ndim - 2) + [(0, pad), (0, 0)])
        k = jnp.pad(k, [(0, 0)] * (k.ndim - 2) + [(0, pad), (0, 0)])
        v = jnp.pad(v, [(0, 0)] * (v.ndim - 2) + [(0, pad), (0, 0)])
    qb = q.reshape(q.shape[:-2] + (nb, BLOCK, hd))

    def with_prev(t):
        tb = t.reshape(t.shape[:-2] + (nb, BLOCK, hd))
        prev = jnp.concatenate([jnp.zeros_like(tb[..., :1, :, :]), tb[..., :-1, :, :]], axis=-3)
        return jnp.concatenate([prev, tb], axis=-2)

    kc, vc = with_prev(k), with_prev(v)
    s = jnp.einsum('...hgnqd,...hnkd->...hgnqk', qb, kc,
                   preferred_element_type=jnp.float32) * (hd ** -0.5)
    qi = jnp.arange(BLOCK)[:, None]
    kj = jnp.arange(2 * BLOCK)[None, :]
    dist = BLOCK + qi - kj
    kpos = (jnp.arange(nb)[:, None, None] - 1) * BLOCK + kj[None]
    valid = (dist >= 0) & (dist <= max_back) & (kpos >= 0)
    s = jnp.where(valid, s, -jnp.inf)
    m = jnp.max(s, axis=-1)
    if sink is not None:
        sk = sink.astype(jnp.float32)[:, :, None, None]
        m = jnp.maximum(m, sk)
    p = jnp.exp(s - m[..., None])
    denom = jnp.sum(p, axis=-1)
    if sink is not None:
        denom = denom + jnp.exp(sk - m)
    o = jnp.einsum('...hgnqk,...hnkd->...hgnqd', p, vc.astype(jnp.float32)) / denom[..., None]
    lse = m + jnp.log(denom)
    o = o.reshape(o.shape[:-3] + (nb * BLOCK, hd))[..., :n, :]
    lse = lse.reshape(lse.shape[:-2] + (nb * BLOCK,))[..., :n]
    return o.astype(q.dtype), lse


def dilated_attention(q, k, v):
    b, s, h, hd = q.shape
    outs, lses = [], []
    for window, dil in DILATED_CONFIGS:
        n = s // dil

        def split(t):
            return t.reshape(b, n, dil, h, hd).transpose(0, 2, 3, 1, 4)

        o, lse = banded_attention(split(q)[:, :, :, None], split(k), split(v), window // dil)
        outs.append(o[:, :, :, 0].transpose(0, 3, 1, 2, 4).reshape(b, s, h, hd))
        lses.append(lse[:, :, :, 0].transpose(0, 3, 1, 2).reshape(b, s, h))
    w = jax.nn.softmax(jnp.stack(lses, axis=0), axis=0)
    o = jnp.sum(w[..., None] * jnp.stack(outs, axis=0).astype(jnp.float32), axis=0)
    return o.astype(q.dtype)


def sink_swa_attention(q, k, v, sinks):
    b, s, _, hd = q.shape
    qg = q.reshape(b, s, N_KV_B, GQA_GROUP, hd).transpose(0, 2, 3, 1, 4)
    o, _ = banded_attention(qg, k.transpose(0, 2, 1, 3), v.transpose(0, 2, 1, 3),
                            SWA_WINDOW - 1, sink=sinks.reshape(N_KV_B, GQA_GROUP))
    return o.transpose(0, 3, 1, 2, 4).reshape(b, s, N_HEADS_B * hd)


def hierarchical_moe(h, w_rg, w_re, w_gu, w_dn):
    b, s, d = h.shape
    t = h.reshape(b * s, d)
    g_prob = jax.nn.softmax(jnp.einsum('td,dg->tg', t, w_rg).astype(jnp.float32), axis=-1)
    g_top, g_idx = lax.top_k(g_prob, 1)
    e_logits = jnp.einsum('td,de->te', t, w_re).astype(jnp.float32)
    e_logits = e_logits.reshape(-1, N_GROUPS, EXPERTS_PER_GROUP)
    e_logits = jnp.einsum('tge,tg->te', e_logits,
                          jax.nn.one_hot(g_idx[:, 0], N_GROUPS, dtype=jnp.float32))
    e_top, e_idx = lax.top_k(jax.nn.softmax(e_logits, axis=-1), TOP_K)
    gate = g_top * e_top / jnp.sum(e_top, axis=-1, keepdims=True)
    flat = (g_idx * EXPERTS_PER_GROUP + e_idx).reshape(-1)
    order = jnp.argsort(flat)
    xs = t[order // TOP_K]
    sizes = jnp.bincount(flat, length=N_EXPERTS).astype(jnp.int32)
    gu = lax.ragged_dot(xs, w_gu, sizes)
    a = jax.nn.silu(gu[:, :D_EXPERT]) * gu[:, D_EXPERT:]
    ys = lax.ragged_dot(a, w_dn, sizes)
    ys = ys[jnp.argsort(order)].reshape(-1, TOP_K, d)
    y = jnp.einsum('tkd,tk->td', ys.astype(jnp.float32), gate)
    return y.astype(h.dtype).reshape(b, s, d)


def setup_inputs(seed: int = 0) -> dict:
    key = jax.random.key(seed)
    ks = jax.random.split(key, 20)
    f32 = jnp.float32
    L, D = DEPTH, D_MODEL

    def gain(k, shape):
        return 1.0 + 0.02 * jax.random.normal(k, shape, f32)

    x = jax.random.normal(ks[0], (BATCH, SEQ, D), f32)
    offset = jax.random.randint(ks[1], (BATCH, 1), 0, 4096, dtype=jnp.int32)
    positions = (jnp.arange(SEQ, dtype=jnp.int32)[None, :] + offset).astype(jnp.int32)
    return {
        'x': x,
        'positions': positions,
        'attn_norm': gain(ks[2], (L, D)),
        'w_in': jax.random.normal(ks[3], (L, D, PROJ_WIDTH), f32) * D ** -0.5,
        'q_norm_a': gain(ks[4], (L, HEAD_DIM)),
        'k_norm_a': gain(ks[5], (L, HEAD_DIM)),
        'q_norm_b': gain(ks[6], (L, HEAD_DIM)),
        'k_norm_b': gain(ks[7], (L, HEAD_DIM)),
        'sinks_b': 0.5 * jax.random.normal(ks[8], (L, N_HEADS_B), f32),
        'out_norm_a': gain(ks[9], (L, MIX_A)),
        'out_norm_b': gain(ks[10], (L, MIX_B)),
        'w_out': jax.random.normal(ks[11], (L, MIX_A + MIX_B, D), f32) * (MIX_A + MIX_B) ** -0.5,
        'ffn_norm': gain(ks[12], (L, D)),
        'w_router_group': jax.random.normal(ks[13], (L, D, N_GROUPS), f32) * D ** -0.5,
        'w_router_expert': jax.random.normal(ks[14], (L, D, N_EXPERTS), f32) * D ** -0.5,
        'w_gate_up': jax.random.normal(ks[15], (L, N_EXPERTS, D, 2 * D_EXPERT), f32) * D ** -0.5,
        'w_down': jax.random.normal(ks[16], (L, N_EXPERTS, D_EXPERT, D), f32) * D_EXPERT ** -0.5,
    }


def reference(x, positions, attn_norm, w_in, q_norm_a, k_norm_a, q_norm_b, k_norm_b, sinks_b,
              out_norm_a, out_norm_b, w_out, ffn_norm, w_router_group, w_router_expert,
              w_gate_up, w_down):
    b, s, _ = x.shape
    cos, sin = rope_tables(positions)
    for l in range(DEPTH):
        h = rmsnorm(x, attn_norm[l])
        proj = jnp.einsum('bsd,dp->bsp', h, w_in[l])
        qa, ka, va, qb, kb, vb = jnp.split(proj, SPLITS, axis=-1)
        qa = qa.reshape(b, s, N_HEADS_A, HEAD_DIM)
        ka = ka.reshape(b, s, N_HEADS_A, HEAD_DIM)
        va = va.reshape(b, s, N_HEADS_A, HEAD_DIM)
        qb = qb.reshape(b, s, N_HEADS_B, HEAD_DIM)
        kb = kb.reshape(b, s, N_KV_B, HEAD_DIM)
        vb = vb.reshape(b, s, N_KV_B, HEAD_DIM)
        qa = apply_partial_rope(rmsnorm(qa, q_norm_a[l]), cos, sin)
        ka = apply_partial_rope(rmsnorm(ka, k_norm_a[l]), cos, sin)
        qb = apply_partial_rope(rmsnorm(qb, q_norm_b[l]), cos, sin)
        kb = apply_partial_rope(rmsnorm(kb, k_norm_b[l]), cos, sin)
        oa = dilated_attention(qa, ka, va).reshape(b, s, MIX_A)
        ob = sink_swa_attention(qb, kb, vb, sinks_b[l])
        mixed = jnp.concatenate([rmsnorm(oa, out_norm_a[l]), rmsnorm(ob, out_norm_b[l])], axis=-1)
        x = x + jnp.einsum('bsm,md->bsd', mixed, w_out[l])
        h = rmsnorm(x, ffn_norm[l])
        x = x + hierarchical_moe(h, w_router_group[l], w_router_expert[l], w_gate_up[l], w_down[l])
    return x
```

```python
import functools

import jax
import jax.numpy as jnp
from jax import lax
from jax.experimental import pallas as pl
from jax.experimental.pallas import tpu as pltpu

F32 = jnp.float32
BF16 = jnp.bfloat16

HEAD_DIM = 64
N_HEADS = 8
MIX = N_HEADS * HEAD_DIM
N_KV_B = 2
KV_B = N_KV_B * HEAD_DIM
DILATIONS = (1, 4, 16)
BLOCK = 128
SWA_BACK = 127
ROPE_DIM = HEAD_DIM // 4
ROPE_HALF = ROPE_DIM // 2
ROPE_THETA = 500000.0
N_GROUPS = 4
EXPERTS_PER_GROUP = 8
N_EXPERTS = N_GROUPS * EXPERTS_PER_GROUP
TOP_K = 2
D_EXPERT = 512
EPS = 1e-6

LANES = 128
PAIR = 2 * HEAD_DIM
EXPERT_LANE0 = 32
ROW_TILE = 256
GROUP_TILE = 256
VMEM_LIMIT = 48 * 1024 * 1024

META_E0, META_E1, META_G0, META_G1, META_R0, META_R1 = 0, 1, 2, 3, 4, 5


def _split2(a):
    hi = a.astype(BF16)
    lo = (a - hi.astype(F32)).astype(BF16)
    return hi, lo


def _split3(a):
    hi = a.astype(BF16)
    r = a - hi.astype(F32)
    mid = r.astype(BF16)
    lo = (r - mid.astype(F32)).astype(BF16)
    return hi, mid, lo


def _dot(a, b):
    return jnp.dot(a, b, preferred_element_type=F32)


def _rms_rows(a):
    return a * lax.rsqrt(jnp.mean(a * a, axis=-1, keepdims=True) + EPS)


def _proj_kernel(x_ref, g_ref, w_ref, rope_ref, e3_ref, bd_ref, hg_ref,
                 qa_ref, ka_ref, va_ref, qb_ref, kb_ref, vb_ref):
    h = _rms_rows(x_ref[...]) * g_ref[...]
    acc = _dot(h.astype(BF16), w_ref[...])

    hi, mid, lo = _split3(rope_ref[...])
    tab = _dot(jnp.concatenate([hi, mid, lo], axis=1), e3_ref[...])
    lane = lax.broadcasted_iota(jnp.int32, (1, LANES), 1)
    not_rope = ((lane & (HEAD_DIM - 1)) >= ROPE_DIM).astype(F32)
    c = tab[:, :LANES] + not_rope
    s_lo = tab[:, LANES:2 * LANES]
    s_hi = tab[:, 2 * LANES:]
    bd = bd_ref[...]

    def norm_rope(a, gain):
        w = a.shape[1]
        a2_hi, a2_lo = _split2(a * a)
        ss = _dot(a2_hi, bd[:w, :w]) + _dot(a2_lo, bd[:w, :w])
        y = (a * lax.rsqrt(ss * (1.0 / HEAD_DIM) + EPS)) * gain
        outs = []
        for j in range(w // LANES):
            yj = y[:, j * LANES:(j + 1) * LANES]
            outs.append(yj * c + pltpu.roll(yj, ROPE_HALF, 1) * s_lo
                        + pltpu.roll(yj, LANES - ROPE_HALF, 1) * s_hi)
        return outs[0] if len(outs) == 1 else jnp.concatenate(outs, axis=1)

    def normed(col0, width, gain_row):
        outs = []
        for c0 in range(col0, col0 + width, 2 * LANES):
            wd = min(2 * LANES, col0 + width - c0)
            outs.append(norm_rope(acc[:, c0:c0 + wd], hg_ref[gain_row:gain_row + 1, :wd]))
        return (outs[0] if len(outs) == 1 else jnp.concatenate(outs, axis=1)).astype(BF16)

    qa_ref[...] = normed(0, MIX, 0)
    ka_ref[...] = normed(MIX, MIX, 1)
    va_ref[...] = acc[:, 2 * MIX:3 * MIX].astype(BF16)
    qb_ref[...] = normed(3 * MIX, MIX, 2)
    kb_ref[...] = normed(4 * MIX, KV_B, 3)
    vb_ref[...] = acc[:, 4 * MIX + KV_B:].astype(BF16)


def _proj(x2d, gain, w_bf16, rope_tab, e3, bd, head_gains):
    t, d = x2d.shape
    pw = w_bf16.shape[1]
    tm = ROW_TILE
    row = lambda i: (i, 0)
    fixed = lambda i: (0, 0)
    out_shapes = [jax.ShapeDtypeStruct((t, w), BF16) for w in (MIX, MIX, MIX, MIX, KV_B, KV_B)]
    return pl.pallas_call(
        _proj_kernel,
        grid=(t // tm,),
        in_specs=[
            pl.BlockSpec((tm, d), row),
            pl.BlockSpec((1, d), fixed),
            pl.BlockSpec((d, pw), fixed),
            pl.BlockSpec((tm, LANES), row),
            pl.BlockSpec(e3.shape, fixed),
            pl.BlockSpec(bd.shape, fixed),
            pl.BlockSpec(head_gains.shape, fixed),
        ],
        out_specs=[pl.BlockSpec((tm, s.shape[1]), row) for s in out_shapes],
        out_shape=out_shapes,
        compiler_params=pltpu.CompilerParams(
            dimension_semantics=("arbitrary",), vmem_limit_bytes=VMEM_LIMIT),
        name="proj",
    )(x2d, gain, w_bf16, rope_tab, e3, bd, head_gains)


def _attn_kernel(*refs, min_back, shared_kv, has_sink, has_state, finalize):
    it = iter(refs)
    q_ref, kp_ref, kc_ref, vp_ref, vc_ref = (next(it) for _ in range(5))
    sink_ref = next(it) if has_sink else None
    if has_state:
        m_in, l_in, acc_in = (next(it) for _ in range(3))
    if finalize:
        o_ref = next(it)
    else:
        m_out, l_out, acc_out = (next(it) for _ in range(3))

    mt = pl.program_id(2)
    qi = lax.broadcasted_iota(jnp.int32, (BLOCK, 2 * BLOCK), 0)
    kj = lax.broadcasted_iota(jnp.int32, (BLOCK, 2 * BLOCK), 1)
    first_key = jnp.where(mt > 0, 0, BLOCK)
    valid = (kj >= qi + min_back) & (kj <= qi + BLOCK) & (kj >= first_key)

    lane = lax.broadcasted_iota(jnp.int32, (1, LANES), 1)
    even = lane < HEAD_DIM
    kcat_all = jnp.concatenate([kp_ref[...], kc_ref[...]], axis=0)
    vcat_all = jnp.concatenate([vp_ref[...], vc_ref[...]], axis=0)

    if has_state:
        m_tile = m_in[...]
        l_tile = l_in[...]
    if not finalize:
        m_pack = jnp.zeros((BLOCK, LANES), F32)
        l_pack = jnp.zeros((BLOCK, LANES), F32)

    for p in range(N_HEADS // 2):
        cols = slice(p * PAIR, (p + 1) * PAIR)
        q_pair = q_ref[:, cols]
        kcat = kcat_all if shared_kv else kcat_all[:, cols]
        vcat = vcat_all if shared_kv else vcat_all[:, cols]
        halves = []
        for half in range(2):
            h = 2 * p + half
            sel = even if half == 0 else jnp.logical_not(even)
            qm = jnp.where(sel, q_pair, jnp.zeros_like(q_pair))
            s = lax.dot_general(qm, kcat, (((1,), (1,)), ((), ())),
                                preferred_element_type=F32)
            s = jnp.where(valid, s, -jnp.inf)
            m_new = jnp.max(s, axis=1, keepdims=True)
            if has_sink:
                m_new = jnp.maximum(m_new, sink_ref[h])
            if has_state:
                m_old = jnp.max(jnp.where(lane == h, m_tile, -jnp.inf), axis=1, keepdims=True)
                m_new = jnp.maximum(m_new, m_old)
            pr = jnp.exp(s - m_new)
            l_new = jnp.sum(pr, axis=1, keepdims=True)
            if has_sink:
                l_new = l_new + jnp.exp(sink_ref[h] - m_new)
            pv = _dot(pr.astype(BF16), vcat)
            if has_state:
                alpha = jnp.exp(m_old - m_new)
                l_old = jnp.sum(jnp.where(lane == h, l_tile, 0.0), axis=1, keepdims=True)
                l_new = l_new + alpha * l_old
                pv = pv + alpha * acc_in[:, cols]
            if finalize:
                pv = pv / l_new
            else:
                m_pack = jnp.where(lane == h, m_new, m_pack)
                l_pack = jnp.where(lane == h, l_new, l_pack)
            halves.append(pv)
        pair_out = jnp.where(even, halves[0], halves[1])
        if finalize:
            o_ref[:, cols] = pair_out.astype(o_ref.dtype)
        else:
            acc_out[:, cols] = pair_out
    if not finalize:
        m_out[...] = m_pack
        l_out[...] = l_pack


def _attention(q, k, v, *, dil, min_back, sink=None, state=None, finalize):
    b, s, _ = q.shape
    n = s // dil
    kvw = k.shape[-1]
    shared_kv = kvw == KV_B
    view = lambda a: a.reshape(b, n, dil * a.shape[-1])
    cur = lambda bi, r, mt: (bi, mt, r)
    prev = lambda bi, r, mt: (bi, jnp.maximum(mt - 1, 0), r)
    wide = lambda imap: pl.BlockSpec((None, BLOCK, MIX), imap)
    kv_spec = lambda imap: pl.BlockSpec((None, BLOCK, kvw), imap)
    packed = pl.BlockSpec((None, BLOCK, LANES), cur)

    args = [view(q), view(k), view(k), view(v), view(v)]
    in_specs = [wide(cur), kv_spec(prev), kv_spec(cur), kv_spec(prev), kv_spec(cur)]
    if sink is not None:
        args.append(sink)
        in_specs.append(pl.BlockSpec(memory_space=pltpu.SMEM))
    aliases = {}
    if state is not None:
        m, l, acc = state
        first = len(args)
        args += [view(m), view(l), view(acc)]
        in_specs += [packed, packed, wide(cur)]
        if not finalize:
            aliases = {first: 0, first + 1: 1, first + 2: 2}
    if finalize:
        out_shape = jax.ShapeDtypeStruct((b, n, dil * MIX), BF16)
        out_specs = wide(cur)
    else:
        out_shape = [jax.ShapeDtypeStruct((b, n, dil * LANES), F32),
                     jax.ShapeDtypeStruct((b, n, dil * LANES), F32),
                     jax.ShapeDtypeStruct((b, n, dil * MIX), F32)]
        out_specs = [packed, packed, wide(cur)]
    kern = functools.partial(_attn_kernel, min_back=min_back, shared_kv=shared_kv,
                             has_sink=sink is not None, has_state=state is not None,
                             finalize=finalize)
    out = pl.pallas_call(
        kern,
        grid=(b, dil, n // BLOCK),
        in_specs=in_specs,
        out_specs=out_specs,
        out_shape=out_shape,
        input_output_aliases=aliases,
        compiler_params=pltpu.CompilerParams(
            dimension_semantics=("arbitrary", "arbitrary", "arbitrary"),
            vmem_limit_bytes=VMEM_LIMIT),
        name=f"attn_d{dil}_{'fin' if finalize else 'acc'}",
    )(*args)
    if finalize:
        return out.reshape(b, s, MIX)
    return tuple(o.reshape(b, s, o.shape[-1] // dil) for o in out)


def _out_router_kernel(oa_ref, ob_ref, x_ref, ga_ref, gb_ref, wo_ref, gf_ref,
                       wr_hi_ref, wr_lo_ref, tri_ref,
                       x1_ref, h_ref, meta_ref, cnt_ref, carry_ref):
    @pl.when(pl.program_id(0) == 0)
    def _():
        carry_ref[...] = jnp.zeros_like(carry_ref)

    na = _rms_rows(oa_ref[...].astype(F32)) * ga_ref[...]
    nb = _rms_rows(ob_ref[...].astype(F32)) * gb_ref[...]
    mixed = jnp.concatenate([na, nb], axis=1).astype(BF16)
    x1 = x_ref[...] + _dot(mixed, wo_ref[...])
    x1_ref[...] = x1
    h = _rms_rows(x1) * gf_ref[...]
    h_ref[...] = h

    h_hi, h_lo = _split2(h)
    logits = _dot(h_hi, wr_hi_ref[...]) + _dot(h_hi, wr_lo_ref[...]) + _dot(h_lo, wr_hi_ref[...])

    tm = logits.shape[0]
    lane = lax.broadcasted_iota(jnp.int32, (tm, LANES), 1)
    far = jnp.int32(2 * LANES)
    neg = -jnp.inf

    def first_argmax(vals):
        mx = jnp.max(vals, axis=1, keepdims=True)
        idx = jnp.min(jnp.where(vals == mx, lane, far), axis=1, keepdims=True)
        return mx, idx

    gl = jnp.where(lane < N_GROUPS, logits, neg)
    gmax, gidx = first_argmax(gl)
    g_top = 1.0 / jnp.sum(jnp.exp(gl - gmax), axis=1, keepdims=True)

    lo_lane = EXPERT_LANE0 + EXPERTS_PER_GROUP * gidx
    el = jnp.where((lane >= lo_lane) & (lane < lo_lane + EXPERTS_PER_GROUP), logits, neg)
    m1, i1 = first_argmax(el)
    m2, i2 = first_argmax(jnp.where(lane == i1, neg, el))
    t2 = jnp.exp(m2 - m1)
    gate0 = g_top / (1.0 + t2)
    gate1 = g_top * t2 / (1.0 + t2)
    e0 = i1 - EXPERT_LANE0
    e1 = i2 - EXPERT_LANE0

    oh0 = (lane == e0)
    oh1 = (lane == e1)
    tri = tri_ref[...]
    cum0 = _dot(tri, oh0.astype(BF16))
    cum1 = _dot(tri, oh1.astype(BF16))
    tot0 = jnp.sum(oh0.astype(F32), axis=0, keepdims=True)
    tot1 = jnp.sum(oh1.astype(F32), axis=0, keepdims=True)
    carry = carry_ref[...]
    rank0 = jnp.sum(jnp.where(oh0, cum0 + carry, 0.0), axis=1, keepdims=True)
    rank1 = jnp.sum(jnp.where(oh1, cum1 + carry + tot0, 0.0), axis=1, keepdims=True)
    carry = carry + tot0 + tot1
    carry_ref[...] = carry
    cnt_ref[...] = carry

    meta = jnp.zeros((tm, LANES), F32)
    for ln, val in ((META_E0, e0.astype(F32)), (META_E1, e1.astype(F32)), (META_G0, gate0),
                    (META_G1, gate1), (META_R0, rank0), (META_R1, rank1)):
        meta = jnp.where(lane == ln, val, meta)
    meta_ref[...] = meta


def _out_router(oa, ob, x2d, ga, gb, wo_bf16, gf, wr_hi, wr_lo, tri):
    t, d = x2d.shape
    tm = ROW_TILE
    row = lambda i: (i, 0)
    fixed = lambda i: (0, 0)
    return pl.pallas_call(
        _out_router_kernel,
        grid=(t // tm,),
        in_specs=[
            pl.BlockSpec((tm, MIX), row),
            pl.BlockSpec((tm, MIX), row),
            pl.BlockSpec((tm, d), row),
            pl.BlockSpec((1, MIX), fixed),
            pl.BlockSpec((1, MIX), fixed),
            pl.BlockSpec((2 * MIX, d), fixed),
            pl.BlockSpec((1, d), fixed),
            pl.BlockSpec((d, LANES), fixed),
            pl.BlockSpec((d, LANES), fixed),
            pl.BlockSpec((tm, tm), fixed),
        ],
        out_specs=[
            pl.BlockSpec((tm, d), row),
            pl.BlockSpec((tm, d), row),
            pl.BlockSpec((tm, LANES), row),
            pl.BlockSpec((1, LANES), fixed),
        ],
        out_shape=[
            jax.ShapeDtypeStruct((t, d), F32),
            jax.ShapeDtypeStruct((t, d), F32),
            jax.ShapeDtypeStruct((t, LANES), F32),
            jax.ShapeDtypeStruct((1, LANES), F32),
        ],
        scratch_shapes=[pltpu.VMEM((1, LANES), F32)],
        compiler_params=pltpu.CompilerParams(
            dimension_semantics=("arbitrary",), vmem_limit_bytes=VMEM_LIMIT),
        name="out_router",
    )(oa, ob, x2d, ga, gb, wo_bf16, gf, wr_hi, wr_lo, tri)


def _row_copy(src, i, dst, j, sem):
    return pltpu.make_async_copy(src.at[pl.ds(i, 1)], dst.at[pl.ds(j, 1)], sem)


def _dispatch_kernel(slot_ref, h_ref, xs_in_ref, xs_ref, sem):
    del xs_in_ref
    tm = h_ref.shape[0]

    def start(i, c):
        for k in range(TOP_K):
            _row_copy(h_ref, i, xs_ref, slot_ref[0, TOP_K * i + k], sem).start()
        return c

    lax.fori_loop(0, tm, start, 0)

    def wait(i, c):
        for k in range(TOP_K):
            _row_copy(h_ref, i, xs_ref, slot_ref[0, TOP_K * i + k], sem).wait()
        return c

    lax.fori_loop(0, tm, wait, 0)


def _dispatch(slots, h, n_rows):
    t, d = h.shape
    tm = ROW_TILE
    xs0 = jnp.zeros((n_rows, d), h.dtype)
    return pl.pallas_call(
        _dispatch_kernel,
        grid=(t // tm,),
        in_specs=[
            pl.BlockSpec((None, 1, TOP_K * tm), lambda i: (i, 0, 0), memory_space=pltpu.SMEM),
            pl.BlockSpec((tm, d), lambda i: (i, 0)),
            pl.BlockSpec(memory_space=pl.ANY),
        ],
        out_specs=pl.BlockSpec(memory_space=pl.ANY),
        out_shape=jax.ShapeDtypeStruct((n_rows, d), h.dtype),
        scratch_shapes=[pltpu.SemaphoreType.DMA(())],
        input_output_aliases={2: 0},
        compiler_params=pltpu.CompilerParams(
            dimension_semantics=("arbitrary",), vmem_limit_bytes=VMEM_LIMIT),
        name="dispatch",
    )(slots, h, xs0)


def _expert_kernel(te_ref, nv_ref, xs_ref, wgu_ref, wdn_ref, ys_ref):
    i = pl.program_id(0)

    @pl.when(i < nv_ref[0])
    def _():
        gu = _dot(xs_ref[...].astype(BF16), wgu_ref[...])
        g = gu[:, :D_EXPERT]
        a = (g * (1.0 / (1.0 + jnp.exp(-g)))) * gu[:, D_EXPERT:]
        ys_ref[...] = _dot(a.astype(BF16), wdn_ref[...])

    @pl.when(i >= nv_ref[0])
    def _():
        ys_ref[...] = jnp.zeros_like(ys_ref)


def _experts(tile_expert, n_valid, xs, wgu_bf16, wdn_bf16):
    n_rows, d = xs.shape
    tg = GROUP_TILE
    grid_spec = pltpu.PrefetchScalarGridSpec(
        num_scalar_prefetch=2,
        grid=(n_rows // tg,),
        in_specs=[
            pl.BlockSpec((tg, d), lambda i, te, nv: (i, 0)),
            pl.BlockSpec((None, d, 2 * D_EXPERT), lambda i, te, nv: (te[i], 0, 0)),
            pl.BlockSpec((None, D_EXPERT, d), lambda i, te, nv: (te[i], 0, 0)),
        ],
        out_specs=pl.BlockSpec((tg, d), lambda i, te, nv: (i, 0)),
    )
    return pl.pallas_call(
        _expert_kernel,
        grid_spec=grid_spec,
        out_shape=jax.ShapeDtypeStruct((n_rows, d), F32),
        compiler_params=pltpu.CompilerParams(
            dimension_semantics=("arbitrary",), vmem_limit_bytes=VMEM_LIMIT),
        name="experts",
    )(tile_expert, n_valid, xs, wgu_bf16, wdn_bf16)


def _combine_kernel(slot_ref, x_ref, meta_ref, ys_ref, out_ref, buf, sem):
    tm = x_ref.shape[0]

    def start(i, c):
        for k in range(TOP_K):
            _row_copy(ys_ref, slot_ref[0, TOP_K * i + k], buf.at[k], i, sem).start()
        return c

    lax.fori_loop(0, tm, start, 0)

    def wait(i, c):
        for k in range(TOP_K):
            _row_copy(ys_ref, slot_ref[0, TOP_K * i + k], buf.at[k], i, sem).wait()
        return c

    lax.fori_loop(0, tm, wait, 0)
    meta = meta_ref[...]
    lane = lax.broadcasted_iota(jnp.int32, (1, LANES), 1)
    g0 = jnp.sum(jnp.where(lane == META_G0, meta, 0.0), axis=1, keepdims=True)
    g1 = jnp.sum(jnp.where(lane == META_G1, meta, 0.0), axis=1, keepdims=True)
    y = g0 * buf[0] + g1 * buf[1]
    out_ref[...] = x_ref[...] + y


def _combine(slots, x1, meta, ys):
    t, d = x1.shape
    tm = ROW_TILE
    return pl.pallas_call(
        _combine_kernel,
        grid=(t // tm,),
        in_specs=[
            pl.BlockSpec((None, 1, TOP_K * tm), lambda i: (i, 0, 0), memory_space=pltpu.SMEM),
            pl.BlockSpec((tm, d), lambda i: (i, 0)),
            pl.BlockSpec((tm, LANES), lambda i: (i, 0)),
            pl.BlockSpec(memory_space=pl.ANY),
        ],
        out_specs=pl.BlockSpec((tm, d), lambda i: (i, 0)),
        out_shape=jax.ShapeDtypeStruct((t, d), F32),
        scratch_shapes=[pltpu.VMEM((TOP_K, tm, d), F32), pltpu.SemaphoreType.DMA(())],
        compiler_params=pltpu.CompilerParams(
            dimension_semantics=("arbitrary",), vmem_limit_bytes=VMEM_LIMIT),
        name="combine",
    )(slots, x1, meta, ys)


def _rope_expand_matrix():
    src = jnp.arange(LANES)[:, None]
    dst = jnp.arange(LANES)[None, :]
    in_head = dst % HEAD_DIM
    cos_m = (src < ROPE_HALF) & (in_head < ROPE_DIM) & (in_head % ROPE_HALF == src)
    is_sin = (src >= ROPE_HALF) & (src < ROPE_DIM)
    sin_lo = is_sin & (in_head >= ROPE_HALF) & (in_head < ROPE_DIM) & (in_head - ROPE_HALF == src - ROPE_HALF)
    sin_hi = is_sin & (in_head < ROPE_HALF) & (in_head == src - ROPE_HALF)
    e = jnp.concatenate([cos_m.astype(F32), sin_lo.astype(F32), -sin_hi.astype(F32)], axis=1)
    return jnp.concatenate([e, e, e], axis=0).astype(BF16)


def _block_diag_ones():
    i = jnp.arange(2 * LANES)
    return (i[:, None] // HEAD_DIM == i[None, :] // HEAD_DIM).astype(BF16)


def _tile4(g):
    return jnp.tile(g.astype(F32), 2 * LANES // HEAD_DIM)


_B_HEAD_ORDER = tuple(h for p in range(N_HEADS // 2) for h in (p, p + N_HEADS // 2))


def _b_cols():
    order = jnp.asarray(_B_HEAD_ORDER)
    return (order[:, None] * HEAD_DIM + jnp.arange(HEAD_DIM)[None, :]).reshape(-1)


def kernel(x, positions, attn_norm, w_in, q_norm_a, k_norm_a, q_norm_b, k_norm_b, sinks_b,
           out_norm_a, out_norm_b, w_out, ffn_norm, w_router_group, w_router_expert,
           w_gate_up, w_down):
    b, s, d = x.shape
    t = b * s
    depth = w_in.shape[0]
    assert s % (BLOCK * max(DILATIONS)) == 0 and t % ROW_TILE == 0
    assert d == 2 * MIX

    inv_freq = ROPE_THETA ** (-jnp.arange(0, ROPE_DIM, 2, dtype=F32) / ROPE_DIM)
    ang = positions.astype(F32).reshape(t, 1) * inv_freq[None, :]
    rope_tab = jnp.concatenate(
        [jnp.cos(ang), jnp.sin(ang), jnp.zeros((t, LANES - ROPE_DIM), F32)], axis=1)
    e3 = _rope_expand_matrix()
    bd = _block_diag_ones()
    row_i = jnp.arange(ROW_TILE)
    tri = (row_i[None, :] < row_i[:, None]).astype(BF16)
    bcols = _b_cols()
    scale = HEAD_DIM ** -0.5
    n_sorted = t * TOP_K + N_EXPERTS * GROUP_TILE
    n_tiles = n_sorted // GROUP_TILE

    x2d = x.reshape(t, d)
    for l in range(depth):
        qb0 = 3 * MIX
        w_l = w_in[l]
        w_l = jnp.concatenate([w_l[:, :qb0], w_l[:, qb0:qb0 + MIX][:, bcols], w_l[:, qb0 + MIX:]], axis=1)
        head_gains = jnp.stack([_tile4(q_norm_a[l]) * scale, _tile4(k_norm_a[l]),
                                _tile4(q_norm_b[l]) * scale, _tile4(k_norm_b[l])])
        sink = sinks_b[l][jnp.asarray(_B_HEAD_ORDER)].astype(F32)
        gb_perm = out_norm_b[l][bcols]
        wo_l = jnp.concatenate([w_out[l][:MIX], w_out[l][MIX:][bcols]], axis=0).astype(BF16)
        wr = jnp.zeros((d, LANES), F32)
        wr = wr.at[:, :N_GROUPS].set(w_router_group[l])
        wr = wr.at[:, EXPERT_LANE0:EXPERT_LANE0 + N_EXPERTS].set(w_router_expert[l])
        wr_hi = wr.astype(BF16)
        wr_lo = (wr - wr_hi.astype(F32)).astype(BF16)

        qa, ka, va, qb, kb, vb = _proj(x2d, attn_norm[l].reshape(1, d), w_l.astype(BF16),
                                       rope_tab, e3, bd, head_gains)
        r3 = lambda a: a.reshape(b, s, a.shape[-1])
        qa, ka, va, qb, kb, vb = map(r3, (qa, ka, va, qb, kb, vb))
        state = None
        for bi, dil in enumerate(DILATIONS):
            last = bi == len(DILATIONS) - 1
            res = _attention(qa, ka, va, dil=dil, min_back=0, state=state, finalize=last)
            if last:
                oa = res
            else:
                state = res
        ob = _attention(qb, kb, vb, dil=1, min_back=BLOCK - SWA_BACK, sink=sink, finalize=True)

        x1, h, meta, counts = _out_router(
            oa.reshape(t, MIX), ob.reshape(t, MIX), x2d, out_norm_a[l].reshape(1, MIX),
            gb_perm.reshape(1, MIX), wo_l, ffn_norm[l].reshape(1, d), wr_hi, wr_lo, tri)

        cnt = counts[0, :N_EXPERTS].astype(jnp.int32)
        padded = ((cnt + GROUP_TILE - 1) // GROUP_TILE) * GROUP_TILE
        ends = jnp.cumsum(padded)
        offs = ends - padded
        eid = meta[:, META_E0:META_E1 + 1].astype(jnp.int32)
        rank = meta[:, META_R0:META_R1 + 1].astype(jnp.int32)
        slots = (offs[eid] + rank).reshape(t // ROW_TILE, 1, TOP_K * ROW_TILE)
        tile_start = jnp.arange(n_tiles, dtype=jnp.int32) * GROUP_TILE
        tile_expert = jnp.minimum(
            jnp.searchsorted(ends, tile_start, side="right"), N_EXPERTS - 1).astype(jnp.int32)
        n_valid = (ends[-1:] // GROUP_TILE).astype(jnp.int32)

        xs = _dispatch(slots, h, n_sorted)
        ys = _experts(tile_expert, n_valid, xs, w_gate_up[l].astype(BF16), w_down[l].astype(BF16))
        x2d = _combine(slots, x1, meta, ys)
    return x2d.reshape(b, s, d)
```

```python
import functools

import jax
import jax.numpy as jnp
from jax import lax
from jax.experimental import pallas as pl
from jax.experimental.pallas import tpu as pltpu

F32 = jnp.float32
BF16 = jnp.bfloat16

HEAD_DIM = 64
N_HEADS = 8
MIX = N_HEADS * HEAD_DIM
N_KV_B = 2
KV_B = N_KV_B * HEAD_DIM
DILATIONS = (1, 4, 16)
BLOCK = 128
CHUNK = BLOCK * max(DILATIONS)
SWA_BACK = 127
ROPE_DIM = HEAD_DIM // 4
ROPE_HALF = ROPE_DIM // 2
ROPE_THETA = 500000.0
N_GROUPS = 4
EXPERTS_PER_GROUP = 8
N_EXPERTS = N_GROUPS * EXPERTS_PER_GROUP
TOP_K = 2
D_EXPERT = 512
EPS = 1e-6

LANES = 128
SUBLANES = 8
PAIR = 2 * HEAD_DIM
HALF_W = MIX // 2
EXPERT_LANE0 = 32
ROW_TILE = 256
GROUP_TILE = 256
DMA_UNROLL = 8
VMEM_LIMIT = 48 * 1024 * 1024

META_E0, META_E1, META_G0, META_G1, META_R0, META_R1 = 0, 1, 2, 3, 4, 5


def _split2(a):
    hi = a.astype(BF16)
    lo = (a - hi.astype(F32)).astype(BF16)
    return hi, lo


def _split3(a):
    hi = a.astype(BF16)
    r = a - hi.astype(F32)
    mid = r.astype(BF16)
    lo = (r - mid.astype(F32)).astype(BF16)
    return hi, mid, lo


def _dot(a, b):
    return jnp.dot(a, b, preferred_element_type=F32)


def _rms_rows(a):
    return a * lax.rsqrt(jnp.mean(a * a, axis=-1, keepdims=True) + EPS)


def _lane_col(tile, lane, idx, fill, reduce):
    return reduce(jnp.where(lane == idx, tile, fill), axis=1, keepdims=True)


def _proj_kernel(x_ref, g_ref, w_ref, rope_ref, e3_ref, bd_ref, hg_ref,
                 qa_ref, ka_ref, va_ref, qa4_ref, ka4_ref, va4_ref, qa16_ref, ka16_ref, va16_ref,
                 qb_ref, kb_ref, vb_ref, scr):
    h = _rms_rows(x_ref[...]) * g_ref[...]
    acc = _dot(h.astype(BF16), w_ref[...])
    tm = acc.shape[0]

    hi, mid, lo = _split3(rope_ref[...])
    tab = _dot(jnp.concatenate([hi, mid, lo], axis=1), e3_ref[...])
    lane = lax.broadcasted_iota(jnp.int32, (1, LANES), 1)
    not_rope = ((lane & (HEAD_DIM - 1)) >= ROPE_DIM).astype(F32)
    c = tab[:, :LANES] + not_rope
    s_lo = tab[:, LANES:2 * LANES]
    s_hi = tab[:, 2 * LANES:]
    bd = bd_ref[...]

    def norm_rope(a, gain):
        w = a.shape[1]
        a2_hi, a2_lo = _split2(a * a)
        ss = _dot(a2_hi, bd[:w, :w]) + _dot(a2_lo, bd[:w, :w])
        y = (a * lax.rsqrt(ss * (1.0 / HEAD_DIM) + EPS)) * gain
        outs = []
        for j in range(w // LANES):
            yj = y[:, j * LANES:(j + 1) * LANES]
            outs.append(yj * c + pltpu.roll(yj, ROPE_HALF, 1) * s_lo
                        + pltpu.roll(yj, LANES - ROPE_HALF, 1) * s_hi)
        return outs[0] if len(outs) == 1 else jnp.concatenate(outs, axis=1)

    def normed(col0, width, gain_row):
        outs = []
        for c0 in range(col0, col0 + width, 2 * LANES):
            wd = min(2 * LANES, col0 + width - c0)
            outs.append(norm_rope(acc[:, c0:c0 + wd], hg_ref[gain_row:gain_row + 1, :wd]))
        return outs[0] if len(outs) == 1 else jnp.concatenate(outs, axis=1)

    def emit_by_class(val, nat_ref, class_refs):
        nat_ref[...] = val.astype(BF16)
        for ct in range(MIX // LANES):
            scr[ct] = val[:, ct * LANES:(ct + 1) * LANES]
        for ref, dil in class_refs:
            for r in range(dil):
                rows = [scr[ct, pl.ds(r, tm // dil, stride=dil), :] for ct in range(MIX // LANES)]
                ref[r] = jnp.concatenate(rows, axis=1).astype(BF16)

    emit_by_class(normed(0, MIX, 0), qa_ref, ((qa4_ref, 4), (qa16_ref, 16)))
    emit_by_class(normed(MIX, MIX, 1), ka_ref, ((ka4_ref, 4), (ka16_ref, 16)))
    emit_by_class(acc[:, 2 * MIX:3 * MIX], va_ref, ((va4_ref, 4), (va16_ref, 16)))
    qb_ref[...] = normed(3 * MIX, MIX, 2).astype(BF16)
    kb_ref[...] = normed(4 * MIX, KV_B, 3).astype(BF16)
    vb_ref[...] = acc[:, 4 * MIX + KV_B:].astype(BF16)


def _proj(x2d, b, gain, w_bf16, rope_tab, e3, bd, head_gains):
    t, d = x2d.shape
    s = t // b
    pw = w_bf16.shape[1]
    tm = ROW_TILE
    nt = s // tm
    row = lambda bi, i: (bi * nt + i, 0)
    fixed = lambda bi, i: (0, 0)
    by_class = lambda bi, i: (bi, 0, i, 0)
    nat = lambda w: (jax.ShapeDtypeStruct((t, w), BF16), pl.BlockSpec((tm, w), row))
    grouped = lambda dil: (jax.ShapeDtypeStruct((b, dil, s // dil, MIX), BF16),
                           pl.BlockSpec((None, dil, tm // dil, MIX), by_class))
    outs = [nat(MIX)] * 3 + [grouped(4)] * 3 + [grouped(16)] * 3 + [nat(MIX), nat(KV_B), nat(KV_B)]
    return pl.pallas_call(
        _proj_kernel,
        grid=(b, nt),
        in_specs=[
            pl.BlockSpec((tm, d), row),
            pl.BlockSpec((1, d), fixed),
            pl.BlockSpec((d, pw), fixed),
            pl.BlockSpec((tm, LANES), row),
            pl.BlockSpec(e3.shape, fixed),
            pl.BlockSpec(bd.shape, fixed),
            pl.BlockSpec(head_gains.shape, fixed),
        ],
        out_specs=[o[1] for o in outs],
        out_shape=[o[0] for o in outs],
        scratch_shapes=[pltpu.VMEM((MIX // LANES, tm, LANES), F32)],
        compiler_params=pltpu.CompilerParams(
            dimension_semantics=("arbitrary", "arbitrary"), vmem_limit_bytes=VMEM_LIMIT),
        name="proj",
    )(x2d, gain, w_bf16, rope_tab, e3, bd, head_gains)


def _band(min_back, first_key):
    qi = lax.broadcasted_iota(jnp.int32, (BLOCK, 2 * BLOCK), 0)
    kj = lax.broadcasted_iota(jnp.int32, (BLOCK, 2 * BLOCK), 1)
    valid = (kj >= qi + min_back) & (kj <= qi + BLOCK)
    return valid if first_key is None else valid & (kj >= first_key)


def _head_scores(q_pair, kcat, sel, valid):
    qm = jnp.where(sel, q_pair, jnp.zeros_like(q_pair))
    s = lax.dot_general(qm, kcat, (((1,), (1,)), ((), ())), preferred_element_type=F32)
    return jnp.where(valid, s, -jnp.inf)


def _weighted_values(pr, v_pair, sel):
    return _dot(pr.astype(BF16), jnp.where(sel, v_pair, jnp.ones_like(v_pair)))


def _normalise_pair(res_even, res_odd, even, extra=None):
    num = jnp.where(even, res_even, res_odd)
    den = pltpu.roll(jnp.where(even, res_odd, res_even), HEAD_DIM, 1)
    return num / (den if extra is None else den + extra)


def _swa_kernel(q_ref, kp_ref, kc_ref, vp_ref, vc_ref, sink_ref, o_ref):
    first_key = jnp.where(pl.program_id(1) > 0, 0, BLOCK)
    valid = _band(BLOCK - SWA_BACK, first_key)
    lane = lax.broadcasted_iota(jnp.int32, (1, LANES), 1)
    even = lane < HEAD_DIM
    kcat = jnp.concatenate([kp_ref[...], kc_ref[...]], axis=0)
    vcat = jnp.concatenate([vp_ref[...], vc_ref[...]], axis=0)
    for p in range(N_HEADS // 2):
        cols = slice(p * PAIR, (p + 1) * PAIR)
        q_pair = q_ref[:, cols]
        res, sink_term = [], []
        for half in range(2):
            sink = sink_ref[2 * p + half]
            sel = even if half == 0 else jnp.logical_not(even)
            s = _head_scores(q_pair, kcat, sel, valid)
            m = jnp.maximum(jnp.max(s, axis=1, keepdims=True), sink)
            res.append(_weighted_values(jnp.exp(s - m), vcat, sel))
            sink_term.append(jnp.exp(sink - m))
        extra = jnp.where(even, sink_term[0], sink_term[1])
        o_ref[:, cols] = _normalise_pair(res[0], res[1], even, extra).astype(o_ref.dtype)


def _swa_attention(q, k, v, sink):
    b, s, _ = q.shape
    cur = lambda bi, mt: (bi, mt, 0)
    prev = lambda bi, mt: (bi, jnp.maximum(mt - 1, 0), 0)
    kv = lambda imap: pl.BlockSpec((None, BLOCK, KV_B), imap)
    return pl.pallas_call(
        _swa_kernel,
        grid=(b, s // BLOCK),
        in_specs=[pl.BlockSpec((None, BLOCK, MIX), cur), kv(prev), kv(cur), kv(prev), kv(cur),
                  pl.BlockSpec(memory_space=pltpu.SMEM)],
        out_specs=pl.BlockSpec((None, BLOCK, MIX), cur),
        out_shape=jax.ShapeDtypeStruct((b, s, MIX), BF16),
        compiler_params=pltpu.CompilerParams(
            dimension_semantics=("arbitrary", "arbitrary"), vmem_limit_bytes=VMEM_LIMIT),
        name="swa",
    )(q, k, k, v, v, sink)


def _dilated_kernel(qn, knp, kn, vnp, vn, q4, k4p, k4, v4p, v4, q16, k16p, k16, v16p, v16,
                    o_ref, m_st, r_st):
    n_pairs = HALF_W // PAIR
    first_key = jnp.where(pl.program_id(1) > 0, 0, BLOCK)
    band = _band(0, None)
    band_first = _band(0, first_key)
    lane = lax.broadcasted_iota(jnp.int32, (1, LANES), 1)
    even = lane < HEAD_DIM

    def tile(q, kcat, vcat, valid, row0, stride, init):
        rows = pl.ds(row0, BLOCK) if stride == 1 else pl.ds(row0, BLOCK, stride=stride)
        for p in range(n_pairs):
            cols = slice(p * PAIR, (p + 1) * PAIR)
            for half in range(2):
                h = 2 * p + half
                sel = even if half == 0 else jnp.logical_not(even)
                s = _head_scores(q[:, cols], kcat[:, cols], sel, valid)
                m_tile = jnp.max(s, axis=1, keepdims=True)
                if init:
                    m_new = jnp.broadcast_to(m_tile, (BLOCK, LANES))
                    res = _weighted_values(jnp.exp(s - m_tile), vcat[:, cols], sel)
                else:
                    m_old = m_st[h, rows, :]
                    m_new = jnp.maximum(m_old, m_tile)
                    pr = jnp.exp(s - jnp.concatenate([m_new, m_new], axis=1))
                    res = (_weighted_values(pr, vcat[:, cols], sel)
                           + jnp.exp(m_old - m_new) * r_st[h, rows, :])
                m_st[h, rows, :] = m_new
                r_st[h, rows, :] = res

    def with_halo(prev, cur):
        return jnp.concatenate([prev, cur], axis=0)

    def d16(r, carry):
        tile(q16[r], with_halo(k16p[r], k16[r]), with_halo(v16p[r], v16[r]), band_first, r, 16, True)
        return carry

    lax.fori_loop(0, 16, d16, 0, unroll=4)

    tile(qn[0:BLOCK, :], with_halo(knp[...], kn[0:BLOCK, :]), with_halo(vnp[...], vn[0:BLOCK, :]),
         band_first, 0, 1, False)

    def d1(j, carry):
        k0 = pl.multiple_of((j - 1) * BLOCK, BLOCK)
        q0 = pl.multiple_of(j * BLOCK, BLOCK)
        tile(qn[pl.ds(q0, BLOCK), :], kn[pl.ds(k0, 2 * BLOCK), :], vn[pl.ds(k0, 2 * BLOCK), :],
             band, q0, 1, False)
        return carry

    lax.fori_loop(1, CHUNK // BLOCK, d1, 0, unroll=5)

    def d4(r, carry):
        tile(q4[r, 0:BLOCK, :], with_halo(k4p[r], k4[r, 0:BLOCK, :]),
             with_halo(v4p[r], v4[r, 0:BLOCK, :]), band_first, r, 4, False)

        def rest(j, c2):
            k0 = pl.multiple_of((j - 1) * BLOCK, BLOCK)
            q0 = pl.multiple_of(j * BLOCK, BLOCK)
            tile(q4[r, pl.ds(q0, BLOCK), :], k4[r, pl.ds(k0, 2 * BLOCK), :],
                 v4[r, pl.ds(k0, 2 * BLOCK), :], band, r + 4 * q0, 4, False)
            return c2

        lax.fori_loop(1, CHUNK // (4 * BLOCK), rest, 0, unroll=True)
        return carry

    lax.fori_loop(0, 4, d4, 0)

    def finish(j, carry):
        rows = pl.ds(pl.multiple_of(j * BLOCK, BLOCK), BLOCK)
        for p in range(n_pairs):
            out = _normalise_pair(r_st[2 * p, rows, :], r_st[2 * p + 1, rows, :], even)
            o_ref[rows, p * PAIR:(p + 1) * PAIR] = out.astype(o_ref.dtype)
        return carry

    lax.fori_loop(0, CHUNK // BLOCK, finish, 0, unroll=2)


def _dilated_attention(nat, by4, by16, b, s):
    nat = [a.reshape(b, s, MIX) for a in nat]
    n_chunks = s // CHUNK
    halves = MIX // HALF_W

    def flat(rows, step):
        cur = pl.BlockSpec((None, rows, HALF_W), lambda bi, c, hf: (bi, c, hf))
        prev = pl.BlockSpec((None, BLOCK, HALF_W),
                            lambda bi, c, hf: (bi, jnp.maximum(step * c - 1, 0), hf))
        return cur, prev

    def grouped(dil):
        rows = CHUNK // dil
        step = rows // BLOCK
        cur = pl.BlockSpec((None, dil, rows, HALF_W), lambda bi, c, hf: (bi, 0, c, hf))
        prev = pl.BlockSpec((None, dil, BLOCK, HALF_W),
                            lambda bi, c, hf: (bi, 0, jnp.maximum(step * c - 1, 0), hf))
        return cur, prev

    args, in_specs = [], []
    for (q, k, v), (cur, prev) in ((nat, flat(CHUNK, CHUNK // BLOCK)), (by4, grouped(4)), (by16, grouped(16))):
        args += [q, k, k, v, v]
        in_specs += [cur, prev, cur, prev, cur]
    return pl.pallas_call(
        _dilated_kernel,
        grid=(b, n_chunks, halves),
        in_specs=in_specs,
        out_specs=pl.BlockSpec((None, CHUNK, HALF_W), lambda bi, c, hf: (bi, c, hf)),
        out_shape=jax.ShapeDtypeStruct((b, s, MIX), BF16),
        scratch_shapes=[pltpu.VMEM((HALF_W // HEAD_DIM, CHUNK, LANES), F32),
                        pltpu.VMEM((HALF_W // HEAD_DIM, CHUNK, LANES), F32)],
        compiler_params=pltpu.CompilerParams(
            dimension_semantics=("arbitrary", "arbitrary", "arbitrary"),
            vmem_limit_bytes=VMEM_LIMIT),
        name="dilated",
    )(*args)


def _out_router_kernel(oa_ref, ob_ref, x_ref, ga_ref, gb_ref, wo_ref, gf_ref,
                       wr_hi_ref, wr_lo_ref, tri_ref,
                       x1_ref, h_ref, meta_ref, meta_t_ref, cnt_ref, carry_ref):
    @pl.when(pl.program_id(0) == 0)
    def _():
        carry_ref[...] = jnp.zeros_like(carry_ref)

    na = _rms_rows(oa_ref[...].astype(F32)) * ga_ref[...]
    nb = _rms_rows(ob_ref[...].astype(F32)) * gb_ref[...]
    mixed = jnp.concatenate([na, nb], axis=1).astype(BF16)
    x1 = x_ref[...] + _dot(mixed, wo_ref[...])
    x1_ref[...] = x1
    h = _rms_rows(x1) * gf_ref[...]
    h_ref[...] = h

    h_hi, h_lo = _split2(h)
    logits = _dot(h_hi, wr_hi_ref[...]) + _dot(h_hi, wr_lo_ref[...]) + _dot(h_lo, wr_hi_ref[...])

    tm = logits.shape[0]
    lane = lax.broadcasted_iota(jnp.int32, (tm, LANES), 1)
    far = jnp.int32(2 * LANES)
    neg = -jnp.inf

    def first_argmax(vals):
        mx = jnp.max(vals, axis=1, keepdims=True)
        idx = jnp.min(jnp.where(vals == mx, lane, far), axis=1, keepdims=True)
        return mx, idx

    gl = jnp.where(lane < N_GROUPS, logits, neg)
    gmax, gidx = first_argmax(gl)
    g_top = 1.0 / jnp.sum(jnp.exp(gl - gmax), axis=1, keepdims=True)

    lo_lane = EXPERT_LANE0 + EXPERTS_PER_GROUP * gidx
    el = jnp.where((lane >= lo_lane) & (lane < lo_lane + EXPERTS_PER_GROUP), logits, neg)
    m1, i1 = first_argmax(el)
    m2, i2 = first_argmax(jnp.where(lane == i1, neg, el))
    t2 = jnp.exp(m2 - m1)
    gate0 = g_top / (1.0 + t2)
    gate1 = g_top * t2 / (1.0 + t2)
    e0 = i1 - EXPERT_LANE0
    e1 = i2 - EXPERT_LANE0

    oh0 = (lane == e0)
    oh1 = (lane == e1)
    tri = tri_ref[...]
    cum0 = _dot(tri, oh0.astype(BF16))
    cum1 = _dot(tri, oh1.astype(BF16))
    tot0 = jnp.sum(oh0.astype(F32), axis=0, keepdims=True)
    tot1 = jnp.sum(oh1.astype(F32), axis=0, keepdims=True)
    carry = carry_ref[...]
    rank0 = jnp.sum(jnp.where(oh0, cum0 + carry, 0.0), axis=1, keepdims=True)
    rank1 = jnp.sum(jnp.where(oh1, cum1 + carry + tot0, 0.0), axis=1, keepdims=True)
    carry = carry + tot0 + tot1
    carry_ref[...] = carry
    cnt_ref[...] = carry

    meta = jnp.zeros((tm, LANES), F32)
    for ln, val in ((META_E0, e0.astype(F32)), (META_E1, e1.astype(F32)), (META_G0, gate0),
                    (META_G1, gate1), (META_R0, rank0), (META_R1, rank1)):
        meta = jnp.where(lane == ln, val, meta)
    meta_ref[...] = meta
    meta_t_ref[...] = meta.T[:SUBLANES, :]


def _out_router(oa, ob, x2d, ga, gb, wo_bf16, gf, wr_hi, wr_lo, tri):
    t, d = x2d.shape
    tm = ROW_TILE
    row = lambda i: (i, 0)
    fixed = lambda i: (0, 0)
    return pl.pallas_call(
        _out_router_kernel,
        grid=(t // tm,),
        in_specs=[
            pl.BlockSpec((tm, MIX), row),
            pl.BlockSpec((tm, MIX), row),
            pl.BlockSpec((tm, d), row),
            pl.BlockSpec((1, MIX), fixed),
            pl.BlockSpec((1, MIX), fixed),
            pl.BlockSpec((2 * MIX, d), fixed),
            pl.BlockSpec((1, d), fixed),
            pl.BlockSpec((d, LANES), fixed),
            pl.BlockSpec((d, LANES), fixed),
            pl.BlockSpec((tm, tm), fixed),
        ],
        out_specs=[
            pl.BlockSpec((tm, d), row),
            pl.BlockSpec((tm, d), row),
            pl.BlockSpec((tm, LANES), row),
            pl.BlockSpec((None, SUBLANES, tm), lambda i: (i, 0, 0)),
            pl.BlockSpec((1, LANES), fixed),
        ],
        out_shape=[
            jax.ShapeDtypeStruct((t, d), F32),
            jax.ShapeDtypeStruct((t, d), F32),
            jax.ShapeDtypeStruct((t, LANES), F32),
            jax.ShapeDtypeStruct((t // tm, SUBLANES, tm), F32),
            jax.ShapeDtypeStruct((1, LANES), F32),
        ],
        scratch_shapes=[pltpu.VMEM((1, LANES), F32)],
        compiler_params=pltpu.CompilerParams(
            dimension_semantics=("arbitrary",), vmem_limit_bytes=VMEM_LIMIT),
        name="out_router",
    )(oa, ob, x2d, ga, gb, wo_bf16, gf, wr_hi, wr_lo, tri)


def _row_copy(src, i, dst, j, sem):
    return pltpu.make_async_copy(src.at[pl.ds(i, 1)], dst.at[pl.ds(j, 1)], sem)


def _dispatch_kernel(slot_ref, h_ref, xs_in_ref, xs_ref, sem):
    del xs_in_ref
    tm = h_ref.shape[0]

    def start(blk, c):
        for u in range(DMA_UNROLL):
            i = blk * DMA_UNROLL + u
            for k in range(TOP_K):
                _row_copy(h_ref, i, xs_ref, slot_ref[0, k * tm + i], sem).start()
        return c

    lax.fori_loop(0, tm // DMA_UNROLL, start, 0)
    for k in range(TOP_K):
        pltpu.make_async_copy(h_ref, xs_ref.at[pl.ds(0, tm)], sem).wait()


def _dispatch(slots, h, xs_buf):
    t, d = h.shape
    tm = ROW_TILE
    return pl.pallas_call(
        _dispatch_kernel,
        grid=(t // tm,),
        in_specs=[
            pl.BlockSpec((None, 1, TOP_K * tm), lambda i: (i, 0, 0), memory_space=pltpu.SMEM),
            pl.BlockSpec((tm, d), lambda i: (i, 0)),
            pl.BlockSpec(memory_space=pl.ANY),
        ],
        out_specs=pl.BlockSpec(memory_space=pl.ANY),
        out_shape=jax.ShapeDtypeStruct(xs_buf.shape, xs_buf.dtype),
        scratch_shapes=[pltpu.SemaphoreType.DMA(())],
        input_output_aliases={2: 0},
        compiler_params=pltpu.CompilerParams(
            dimension_semantics=("arbitrary",), vmem_limit_bytes=VMEM_LIMIT),
        name="dispatch",
    )(slots, h, xs_buf)


def _expert_kernel(te_ref, nv_ref, xs_ref, wgu_ref, wdn_ref, ys_ref, wgu_bf, wdn_bf):
    i = pl.program_id(0)

    @pl.when((i == 0) | (te_ref[i] != te_ref[jnp.maximum(i - 1, 0)]))
    def _():
        wgu_bf[...] = wgu_ref[...].astype(BF16)
        wdn_bf[...] = wdn_ref[...].astype(BF16)

    @pl.when(i < nv_ref[0])
    def _():
        gu = _dot(xs_ref[...].astype(BF16), wgu_bf[...])
        g = gu[:, :D_EXPERT]
        a = (g * (1.0 / (1.0 + jnp.exp(-g)))) * gu[:, D_EXPERT:]
        ys_ref[...] = _dot(a.astype(BF16), wdn_bf[...])

    @pl.when(i >= nv_ref[0])
    def _():
        ys_ref[...] = jnp.zeros_like(ys_ref)


def _experts(tile_expert, n_valid, xs, wgu, wdn):
    n_rows, d = xs.shape
    tg = GROUP_TILE
    grid_spec = pltpu.PrefetchScalarGridSpec(
        num_scalar_prefetch=2,
        grid=(n_rows // tg,),
        in_specs=[
            pl.BlockSpec((tg, d), lambda i, te, nv: (i, 0)),
            pl.BlockSpec((None, d, 2 * D_EXPERT), lambda i, te, nv: (te[i], 0, 0)),
            pl.BlockSpec((None, D_EXPERT, d), lambda i, te, nv: (te[i], 0, 0)),
        ],
        out_specs=pl.BlockSpec((tg, d), lambda i, te, nv: (i, 0)),
        scratch_shapes=[pltpu.VMEM((d, 2 * D_EXPERT), BF16), pltpu.VMEM((D_EXPERT, d), BF16)],
    )
    return pl.pallas_call(
        _expert_kernel,
        grid_spec=grid_spec,
        out_shape=jax.ShapeDtypeStruct((n_rows, d), F32),
        compiler_params=pltpu.CompilerParams(
            dimension_semantics=("arbitrary",), vmem_limit_bytes=VMEM_LIMIT),
        name="experts",
    )(tile_expert, n_valid, xs, wgu, wdn)


def _combine_kernel(slot_ref, x_ref, meta_ref, ys_ref, out_ref, buf, sem):
    tm = x_ref.shape[0]

    def start(blk, c):
        for u in range(DMA_UNROLL):
            i = blk * DMA_UNROLL + u
            for k in range(TOP_K):
                _row_copy(ys_ref, slot_ref[0, k * tm + i], buf.at[k], i, sem).start()
        return c

    lax.fori_loop(0, tm // DMA_UNROLL, start, 0)
    for k in range(TOP_K):
        pltpu.make_async_copy(ys_ref.at[pl.ds(0, tm)], buf.at[k], sem).wait()
    meta = meta_ref[...]
    lane = lax.broadcasted_iota(jnp.int32, (1, LANES), 1)
    g0 = _lane_col(meta, lane, META_G0, 0.0, jnp.sum)
    g1 = _lane_col(meta, lane, META_G1, 0.0, jnp.sum)
    out_ref[...] = x_ref[...] + (g0 * buf[0] + g1 * buf[1])


def _combine(slots, x1, meta, ys):
    t, d = x1.shape
    tm = ROW_TILE
    return pl.pallas_call(
        _combine_kernel,
        grid=(t // tm,),
        in_specs=[
            pl.BlockSpec((None, 1, TOP_K * tm), lambda i: (i, 0, 0), memory_space=pltpu.SMEM),
            pl.BlockSpec((tm, d), lambda i: (i, 0)),
            pl.BlockSpec((tm, LANES), lambda i: (i, 0)),
            pl.BlockSpec(memory_space=pl.ANY),
        ],
        out_specs=pl.BlockSpec((tm, d), lambda i: (i, 0)),
        out_shape=jax.ShapeDtypeStruct((t, d), F32),
        scratch_shapes=[pltpu.VMEM((TOP_K, tm, d), F32), pltpu.SemaphoreType.DMA(())],
        compiler_params=pltpu.CompilerParams(
            dimension_semantics=("arbitrary",), vmem_limit_bytes=VMEM_LIMIT),
        name="combine",
    )(slots, x1, meta, ys)


def _rope_expand_matrix():
    src = jnp.arange(LANES)[:, None]
    dst = jnp.arange(LANES)[None, :]
    in_head = dst % HEAD_DIM
    cos_m = (src < ROPE_HALF) & (in_head < ROPE_DIM) & (in_head % ROPE_HALF == src)
    is_sin = (src >= ROPE_HALF) & (src < ROPE_DIM)
    sin_lo = is_sin & (in_head >= ROPE_HALF) & (in_head < ROPE_DIM) & (in_head - ROPE_HALF == src - ROPE_HALF)
    sin_hi = is_sin & (in_head < ROPE_HALF) & (in_head == src - ROPE_HALF)
    e = jnp.concatenate([cos_m.astype(F32), sin_lo.astype(F32), -sin_hi.astype(F32)], axis=1)
    return jnp.concatenate([e, e, e], axis=0).astype(BF16)


def _block_diag_ones():
    i = jnp.arange(2 * LANES)
    return (i[:, None] // HEAD_DIM == i[None, :] // HEAD_DIM).astype(BF16)


def _tile4(g):
    return jnp.tile(g.astype(F32), 2 * LANES // HEAD_DIM)


_B_HEAD_ORDER = tuple(h for p in range(N_HEADS // 2) for h in (p, p + N_HEADS // 2))


def _b_cols():
    order = jnp.asarray(_B_HEAD_ORDER)
    return (order[:, None] * HEAD_DIM + jnp.arange(HEAD_DIM)[None, :]).reshape(-1)


def kernel(x, positions, attn_norm, w_in, q_norm_a, k_norm_a, q_norm_b, k_norm_b, sinks_b,
           out_norm_a, out_norm_b, w_out, ffn_norm, w_router_group, w_router_expert,
           w_gate_up, w_down):
    b, s, d = x.shape
    t = b * s
    depth = w_in.shape[0]
    assert s % CHUNK == 0 and s % ROW_TILE == 0 and ROW_TILE % (2 * SUBLANES * max(DILATIONS)) == 0
    assert d == 2 * MIX

    inv_freq = ROPE_THETA ** (-jnp.arange(0, ROPE_DIM, 2, dtype=F32) / ROPE_DIM)
    ang = positions.astype(F32).reshape(t, 1) * inv_freq[None, :]
    rope_tab = jnp.concatenate(
        [jnp.cos(ang), jnp.sin(ang), jnp.zeros((t, LANES - ROPE_DIM), F32)], axis=1)
    e3 = _rope_expand_matrix()
    bd = _block_diag_ones()
    row_i = jnp.arange(ROW_TILE)
    tri = (row_i[None, :] < row_i[:, None]).astype(BF16)
    bcols = _b_cols()
    scale = HEAD_DIM ** -0.5
    n_sorted = t * TOP_K + N_EXPERTS * GROUP_TILE
    n_tiles = n_sorted // GROUP_TILE
    xs_buf = jnp.zeros((n_sorted, d), F32)

    x2d = x.reshape(t, d)
    for l in range(depth):
        qb0 = 3 * MIX
        w_l = w_in[l]
        w_l = jnp.concatenate([w_l[:, :qb0], w_l[:, qb0:qb0 + MIX][:, bcols], w_l[:, qb0 + MIX:]], axis=1)
        head_gains = jnp.stack([_tile4(q_norm_a[l]) * scale, _tile4(k_norm_a[l]),
                                _tile4(q_norm_b[l]) * scale, _tile4(k_norm_b[l])])
        sink = sinks_b[l][jnp.asarray(_B_HEAD_ORDER)].astype(F32)
        gb_perm = out_norm_b[l][bcols]
        wo_l = jnp.concatenate([w_out[l][:MIX], w_out[l][MIX:][bcols]], axis=0).astype(BF16)
        wr = jnp.concatenate(
            [w_router_group[l], jnp.zeros((d, EXPERT_LANE0 - N_GROUPS), F32), w_router_expert[l],
             jnp.zeros((d, LANES - EXPERT_LANE0 - N_EXPERTS), F32)], axis=1)
        wr_hi = wr.astype(BF16)
        wr_lo = (wr - wr_hi.astype(F32)).astype(BF16)

        (qa, ka, va, qa4, ka4, va4, qa16, ka16, va16, qb, kb, vb) = _proj(
            x2d, b, attn_norm[l].reshape(1, d), w_l.astype(BF16), rope_tab, e3, bd, head_gains)
        oa = _dilated_attention((qa, ka, va), (qa4, ka4, va4), (qa16, ka16, va16), b, s)
        ob = _swa_attention(qb.reshape(b, s, MIX), kb.reshape(b, s, KV_B), vb.reshape(b, s, KV_B), sink)

        x1, h, meta, meta_t, counts = _out_router(
            oa.reshape(t, MIX), ob.reshape(t, MIX), x2d, out_norm_a[l].reshape(1, MIX),
            gb_perm.reshape(1, MIX), wo_l, ffn_norm[l].reshape(1, d), wr_hi, wr_lo, tri)

        cnt = counts[0, :N_EXPERTS].astype(jnp.int32)
        padded = ((cnt + GROUP_TILE - 1) // GROUP_TILE) * GROUP_TILE
        ends = jnp.cumsum(padded)
        offs = ends - padded
        eid = meta_t[:, META_E0:META_E1 + 1, :].astype(jnp.int32)
        rank = meta_t[:, META_R0:META_R1 + 1, :].astype(jnp.int32)
        slots = (offs[eid] + rank).reshape(t // ROW_TILE, 1, TOP_K * ROW_TILE)
        tile_start = jnp.arange(n_tiles, dtype=jnp.int32) * GROUP_TILE
        tile_expert = jnp.minimum(
            jnp.sum((tile_start[:, None] >= ends[None, :]).astype(jnp.int32), axis=1), N_EXPERTS - 1)
        n_valid = (ends[-1:] // GROUP_TILE).astype(jnp.int32)

        xs_buf = _dispatch(slots, h, xs_buf)
        ys = _experts(tile_expert, n_valid, xs_buf, w_gate_up[l], w_down[l])
        x2d = _combine(slots, x1, meta, ys)
    return x2d.reshape(b, s, d)
```

```python
import functools
import math

import jax
import jax.numpy as jnp
from jax import lax
from jax.experimental import pallas as pl
from jax.experimental.pallas import tpu as pltpu

F32 = jnp.float32
BF16 = jnp.bfloat16

HEAD_DIM = 64
N_HEADS = 8
MIX = N_HEADS * HEAD_DIM
N_KV_B = 2
KV_B = N_KV_B * HEAD_DIM
DILATIONS = (1, 4, 16)
BLOCK = 128
CHUNK = BLOCK * max(DILATIONS)
SWA_BACK = 127
SWA_STEP = 8 * BLOCK
ROPE_DIM = HEAD_DIM // 4
ROPE_HALF = ROPE_DIM // 2
ROPE_THETA = 500000.0
N_GROUPS = 4
EXPERTS_PER_GROUP = 8
N_EXPERTS = N_GROUPS * EXPERTS_PER_GROUP
TOP_K = 2
D_EXPERT = 512
EPS = 1e-6
LOG2E = math.log2(math.e)

LANES = 128
SUBLANES = 8
PAIR = 2 * HEAD_DIM
HALF_W = MIX // 2
EXPERT_LANE0 = 32
ROW_TILE = 512
ROUTE_BLOCK = 256
ROUTE_ROWS = 128
GROUP_TILE = 256
DMA_UNROLL = 8
VMEM_LIMIT = 48 * 1024 * 1024

META_E0, META_E1, META_G0, META_G1, META_R0, META_R1 = 0, 1, 2, 3, 4, 5


def _split2(a):
    hi = a.astype(BF16)
    lo = (a - hi.astype(F32)).astype(BF16)
    return hi, lo


def _split3(a):
    hi = a.astype(BF16)
    r = a - hi.astype(F32)
    mid = r.astype(BF16)
    lo = (r - mid.astype(F32)).astype(BF16)
    return hi, mid, lo


def _dot(a, b):
    return jnp.dot(a, b, preferred_element_type=F32)


def _rms_rows(a):
    return a * lax.rsqrt(jnp.mean(a * a, axis=-1, keepdims=True) + EPS)


def _lane_col(tile, lane, idx, fill, reduce):
    return reduce(jnp.where(lane == idx, tile, fill), axis=1, keepdims=True)


def _store_token_tiles(ref, val):
    rows, width = val.shape
    n = width // LANES
    for c in range(n):
        ref[pl.ds(c, rows, stride=n), :] = val[:, c * LANES:(c + 1) * LANES]


def _load_token_tiles(ref, rows, n):
    return jnp.concatenate([ref[pl.ds(c, rows, stride=n), :] for c in range(n)], axis=1)


def _proj_kernel(x_ref, g_ref, w_ref, rope_ref, e3_ref, bd_ref, hg_ref,
                 qa_ref, ka_ref, va_ref, qa4_ref, ka4_ref, va4_ref, qa16_ref, ka16_ref, va16_ref,
                 qb_ref, kb_ref, vb_ref, scr):
    h = _rms_rows(x_ref[...]) * g_ref[...]
    acc = _dot(h.astype(BF16), w_ref[...])
    tm = acc.shape[0]

    hi, mid, lo = _split3(rope_ref[...])
    tab = _dot(jnp.concatenate([hi, mid, lo], axis=1), e3_ref[...])
    lane = lax.broadcasted_iota(jnp.int32, (1, LANES), 1)
    not_rope = ((lane & (HEAD_DIM - 1)) >= ROPE_DIM).astype(F32)
    c = tab[:, :LANES] + not_rope
    s_lo = tab[:, LANES:2 * LANES]
    s_hi = tab[:, 2 * LANES:]
    bd = bd_ref[...]

    def norm_rope(a, gain):
        w = a.shape[1]
        a2_hi, a2_lo = _split2(a * a)
        ss = _dot(a2_hi, bd[:w, :w]) + _dot(a2_lo, bd[:w, :w])
        y = (a * lax.rsqrt(ss * (1.0 / HEAD_DIM) + EPS)) * gain
        outs = []
        for j in range(w // LANES):
            yj = y[:, j * LANES:(j + 1) * LANES]
            outs.append(yj * c + pltpu.roll(yj, ROPE_HALF, 1) * s_lo
                        + pltpu.roll(yj, LANES - ROPE_HALF, 1) * s_hi)
        return outs[0] if len(outs) == 1 else jnp.concatenate(outs, axis=1)

    def normed(col0, width, gain_row):
        outs = []
        for c0 in range(col0, col0 + width, 2 * LANES):
            wd = min(2 * LANES, col0 + width - c0)
            outs.append(norm_rope(acc[:, c0:c0 + wd], hg_ref[gain_row:gain_row + 1, :wd]))
        return outs[0] if len(outs) == 1 else jnp.concatenate(outs, axis=1)

    def emit_by_class(val, nat_ref, class_refs):
        nat_ref[...] = val.astype(BF16)
        for ct in range(MIX // LANES):
            scr[ct] = val[:, ct * LANES:(ct + 1) * LANES]
        for ref, dil in class_refs:
            for r in range(dil):
                rows = [scr[ct, pl.ds(r, tm // dil, stride=dil), :] for ct in range(MIX // LANES)]
                ref[r] = jnp.concatenate(rows, axis=1).astype(BF16)

    emit_by_class(normed(0, MIX, 0), qa_ref, ((qa4_ref, 4), (qa16_ref, 16)))
    emit_by_class(normed(MIX, MIX, 1), ka_ref, ((ka4_ref, 4), (ka16_ref, 16)))
    emit_by_class(acc[:, 2 * MIX:3 * MIX], va_ref, ((va4_ref, 4), (va16_ref, 16)))
    qb_ref[...] = normed(3 * MIX, MIX, 2).astype(BF16)
    kb_ref[...] = normed(4 * MIX, KV_B, 3).astype(BF16)
    vb_ref[...] = acc[:, 4 * MIX + KV_B:].astype(BF16)


def _proj(x2d, b, gain, w_bf16, rope_tab, e3, bd, head_gains):
    t, d = x2d.shape
    s = t // b
    pw = w_bf16.shape[1]
    tm = ROW_TILE
    nt = s // tm
    row = lambda bi, i: (bi * nt + i, 0)
    fixed = lambda bi, i: (0, 0)
    by_class = lambda bi, i: (bi, 0, i, 0)
    nat = lambda w: (jax.ShapeDtypeStruct((t, w), BF16), pl.BlockSpec((tm, w), row))
    grouped = lambda dil: (jax.ShapeDtypeStruct((b, dil, s // dil, MIX), BF16),
                           pl.BlockSpec((None, dil, tm // dil, MIX), by_class))
    outs = [nat(MIX)] * 3 + [grouped(4)] * 3 + [grouped(16)] * 3 + [nat(MIX), nat(KV_B), nat(KV_B)]
    return pl.pallas_call(
        _proj_kernel,
        grid=(b, nt),
        in_specs=[
            pl.BlockSpec((tm, d), row),
            pl.BlockSpec((1, d), fixed),
            pl.BlockSpec((d, pw), fixed),
            pl.BlockSpec((tm, LANES), row),
            pl.BlockSpec(e3.shape, fixed),
            pl.BlockSpec(bd.shape, fixed),
            pl.BlockSpec(head_gains.shape, fixed),
        ],
        out_specs=[o[1] for o in outs],
        out_shape=[o[0] for o in outs],
        scratch_shapes=[pltpu.VMEM((MIX // LANES, tm, LANES), F32)],
        compiler_params=pltpu.CompilerParams(
            dimension_semantics=("arbitrary", "arbitrary"), vmem_limit_bytes=VMEM_LIMIT),
        name="proj",
    )(x2d, gain, w_bf16, rope_tab, e3, bd, head_gains)


def _band(min_back, first_key):
    qi = lax.broadcasted_iota(jnp.int32, (BLOCK, 2 * BLOCK), 0)
    kj = lax.broadcasted_iota(jnp.int32, (BLOCK, 2 * BLOCK), 1)
    valid = (kj >= qi + min_back) & (kj <= qi + BLOCK)
    return valid if first_key is None else valid & (kj >= first_key)


def _head_scores(q_pair, kcat, sel, valid):
    qm = jnp.where(sel, q_pair, jnp.zeros_like(q_pair))
    s = lax.dot_general(qm, kcat, (((1,), (1,)), ((), ())), preferred_element_type=F32)
    return jnp.where(valid, s, -jnp.inf)


def _weighted_values(pr, v_pair, sel):
    return _dot(pr.astype(BF16), jnp.where(sel, v_pair, jnp.ones_like(v_pair)))


def _normalise_pair(res_even, res_odd, even, extra=None):
    num = jnp.where(even, res_even, res_odd)
    den = pltpu.roll(jnp.where(even, res_odd, res_even), HEAD_DIM, 1)
    return num / (den if extra is None else den + extra)


def _with_halo(prev, cur):
    return jnp.concatenate([prev, cur], axis=0)


def _swa_kernel(q_ref, kp_ref, k_ref, vp_ref, v_ref, sink_ref, o_ref):
    first_key = jnp.where(pl.program_id(1) > 0, 0, BLOCK)
    band = _band(BLOCK - SWA_BACK, None)
    band_first = _band(BLOCK - SWA_BACK, first_key)
    lane = lax.broadcasted_iota(jnp.int32, (1, LANES), 1)
    even = lane < HEAD_DIM
    for j in range(SWA_STEP // BLOCK):
        valid = band_first if j == 0 else band
        kcat = _with_halo(kp_ref[...] if j == 0 else k_ref[j - 1], k_ref[j])
        vcat = _with_halo(vp_ref[...] if j == 0 else v_ref[j - 1], v_ref[j])
        for p in range(N_HEADS // 2):
            cols = slice(p * PAIR, (p + 1) * PAIR)
            q_pair = q_ref[j, :, cols]
            res, sink_term = [], []
            for half in range(2):
                sink = sink_ref[2 * p + half]
                sel = even if half == 0 else jnp.logical_not(even)
                s = _head_scores(q_pair, kcat, sel, valid)
                m = jnp.maximum(jnp.max(s, axis=1, keepdims=True), sink)
                res.append(_weighted_values(jnp.exp2(s - m), vcat, sel))
                sink_term.append(jnp.exp2(sink - m))
            extra = jnp.where(even, sink_term[0], sink_term[1])
            o_ref[j, :, cols] = _normalise_pair(res[0], res[1], even, extra).astype(o_ref.dtype)


def _swa_attention(q, k, v, sink):
    b, nb = q.shape[:2]
    step = SWA_STEP // BLOCK
    cur = lambda w: pl.BlockSpec((None, step, BLOCK, w), lambda bi, c: (bi, c, 0, 0))
    prev = pl.BlockSpec((None, None, BLOCK, KV_B),
                        lambda bi, c: (bi, jnp.maximum(step * c - 1, 0), 0, 0))
    return pl.pallas_call(
        _swa_kernel,
        grid=(b, nb // step),
        in_specs=[cur(MIX), prev, cur(KV_B), prev, cur(KV_B), pl.BlockSpec(memory_space=pltpu.SMEM)],
        out_specs=cur(MIX),
        out_shape=jax.ShapeDtypeStruct(q.shape, BF16),
        compiler_params=pltpu.CompilerParams(
            dimension_semantics=("arbitrary", "arbitrary"), vmem_limit_bytes=VMEM_LIMIT),
        name="swa",
    )(q, k, k, v, v, sink)


def _dilated_kernel(qn, knp, kn, vnp, vn, q4, k4p, k4, v4p, v4, q16, k16p, k16, v16p, v16,
                    o_ref, m_st, r_st):
    n_pairs = HALF_W // PAIR
    first_key = jnp.where(pl.program_id(1) > 0, 0, BLOCK)
    band = _band(0, None)
    band_first = _band(0, first_key)
    lane = lax.broadcasted_iota(jnp.int32, (1, LANES), 1)
    even = lane < HEAD_DIM

    def tile(q, kcat, vcat, valid, row0, stride, init):
        rows = pl.ds(row0, BLOCK) if stride == 1 else pl.ds(row0, BLOCK, stride=stride)
        for p in range(n_pairs):
            cols = slice(p * PAIR, (p + 1) * PAIR)
            for half in range(2):
                h = 2 * p + half
                sel = even if half == 0 else jnp.logical_not(even)
                s = _head_scores(q[:, cols], kcat[:, cols], sel, valid)
                m_tile = jnp.max(s, axis=1, keepdims=True)
                if init:
                    m_new = jnp.broadcast_to(m_tile, (BLOCK, LANES))
                    res = _weighted_values(jnp.exp2(s - m_tile), vcat[:, cols], sel)
                else:
                    m_old = m_st[h, rows, :]
                    m_new = jnp.maximum(m_old, m_tile)
                    pr = jnp.exp2(s - jnp.concatenate([m_new, m_new], axis=1))
                    res = (_weighted_values(pr, vcat[:, cols], sel)
                           + jnp.exp2(m_old - m_new) * r_st[h, rows, :])
                m_st[h, rows, :] = m_new
                r_st[h, rows, :] = res

    def d16(r, carry):
        tile(q16[r], _with_halo(k16p[r], k16[r]), _with_halo(v16p[r], v16[r]), band_first, r, 16, True)
        return carry

    lax.fori_loop(0, 16, d16, 0, unroll=4)

    tile(qn[0], _with_halo(knp[...], kn[0]), _with_halo(vnp[...], vn[0]), band_first, 0, 1, False)

    def d1(j, carry):
        tile(qn[j], _with_halo(kn[j - 1], kn[j]), _with_halo(vn[j - 1], vn[j]),
             band, pl.multiple_of(j * BLOCK, BLOCK), 1, False)
        return carry

    lax.fori_loop(1, CHUNK // BLOCK, d1, 0, unroll=5)

    def d4(r, carry):
        tile(q4[r, 0], _with_halo(k4p[r], k4[r, 0]), _with_halo(v4p[r], v4[r, 0]), band_first, r, 4, False)
        for j in range(1, CHUNK // (4 * BLOCK)):
            tile(q4[r, j], _with_halo(k4[r, j - 1], k4[r, j]), _with_halo(v4[r, j - 1], v4[r, j]),
                 band, r + 4 * j * BLOCK, 4, False)
        return carry

    lax.fori_loop(0, 4, d4, 0)

    def finish(j, carry):
        rows = pl.ds(pl.multiple_of(j * BLOCK, BLOCK), BLOCK)
        for p in range(n_pairs):
            out = _normalise_pair(r_st[2 * p, rows, :], r_st[2 * p + 1, rows, :], even)
            o_ref[j, :, p * PAIR:(p + 1) * PAIR] = out.astype(o_ref.dtype)
        return carry

    lax.fori_loop(0, CHUNK // BLOCK, finish, 0, unroll=2)


def _dilated_attention(nat, by4, by16):
    b, nb = nat[0].shape[:2]
    per_chunk = CHUNK // BLOCK
    n_chunks = nb // per_chunk

    def specs(dil):
        step = per_chunk // dil
        if dil == 1:
            cur = pl.BlockSpec((None, step, BLOCK, HALF_W), lambda bi, c, hf: (bi, c, 0, hf))
            prev = pl.BlockSpec((None, None, BLOCK, HALF_W),
                                lambda bi, c, hf: (bi, jnp.maximum(step * c - 1, 0), 0, hf))
        else:
            blocks = None if step == 1 else step
            cur = pl.BlockSpec((None, dil, blocks, BLOCK, HALF_W), lambda bi, c, hf: (bi, 0, c, 0, hf))
            prev = pl.BlockSpec((None, dil, None, BLOCK, HALF_W),
                                lambda bi, c, hf: (bi, 0, jnp.maximum(step * c - 1, 0), 0, hf))
        return cur, prev

    args, in_specs = [], []
    for (q, k, v), dil in ((nat, 1), (by4, 4), (by16, 16)):
        cur, prev = specs(dil)
        args += [q, k, k, v, v]
        in_specs += [cur, prev, cur, prev, cur]
    return pl.pallas_call(
        _dilated_kernel,
        grid=(b, n_chunks, MIX // HALF_W),
        in_specs=in_specs,
        out_specs=specs(1)[0],
        out_shape=jax.ShapeDtypeStruct(nat[0].shape, BF16),
        scratch_shapes=[pltpu.VMEM((HALF_W // HEAD_DIM, CHUNK, LANES), F32),
                        pltpu.VMEM((HALF_W // HEAD_DIM, CHUNK, LANES), F32)],
        compiler_params=pltpu.CompilerParams(
            dimension_semantics=("arbitrary", "arbitrary", "arbitrary"),
            vmem_limit_bytes=VMEM_LIMIT),
        name="dilated",
    )(*args)


def _route(logits, tri, base):
    n = logits.shape[0]
    lane = lax.broadcasted_iota(jnp.int32, (n, LANES), 1)
    far = jnp.int32(2 * LANES)
    neg = -jnp.inf

    def first_argmax(vals):
        mx = jnp.max(vals, axis=1, keepdims=True)
        idx = jnp.min(jnp.where(vals == mx, lane, far), axis=1, keepdims=True)
        return mx, idx

    gl = jnp.where(lane < N_GROUPS, logits, neg)
    gmax, gidx = first_argmax(gl)
    g_top = 1.0 / jnp.sum(jnp.exp(gl - gmax), axis=1, keepdims=True)

    lo_lane = EXPERT_LANE0 + EXPERTS_PER_GROUP * gidx
    el = jnp.where((lane >= lo_lane) & (lane < lo_lane + EXPERTS_PER_GROUP), logits, neg)
    m1, i1 = first_argmax(el)
    m2, i2 = first_argmax(jnp.where(lane == i1, neg, el))
    t2 = jnp.exp(m2 - m1)
    gate0 = g_top / (1.0 + t2)
    gate1 = g_top * t2 / (1.0 + t2)
    e0 = i1 - EXPERT_LANE0
    e1 = i2 - EXPERT_LANE0

    oh0 = (lane == e0)
    oh1 = (lane == e1)
    cum0 = _dot(tri, oh0.astype(BF16))
    cum1 = _dot(tri, oh1.astype(BF16))
    tot0 = jnp.sum(oh0.astype(F32), axis=0, keepdims=True)
    tot1 = jnp.sum(oh1.astype(F32), axis=0, keepdims=True)
    rank0 = jnp.sum(jnp.where(oh0, cum0 + base, 0.0), axis=1, keepdims=True)
    rank1 = jnp.sum(jnp.where(oh1, cum1 + base + tot0, 0.0), axis=1, keepdims=True)

    meta = jnp.zeros((n, LANES), F32)
    for ln, val in ((META_E0, e0.astype(F32)), (META_E1, e1.astype(F32)), (META_G0, gate0),
                    (META_G1, gate1), (META_R0, rank0), (META_R1, rank1)):
        meta = jnp.where(lane == ln, val, meta)
    return meta, base + tot0 + tot1


def _out_router_kernel(oa_ref, ob_ref, x_ref, ga_ref, gb_ref, wo_ref, gf_ref, wr_ref, tri_ref,
                       x1_ref, h_ref, meta_ref, meta_t_ref, cnt_ref, carry_ref):
    @pl.when(pl.program_id(0) == 0)
    def _():
        carry_ref[...] = jnp.zeros_like(carry_ref)

    tm, d = x_ref.shape
    tok = d // LANES
    tri = tri_ref[...]
    base = carry_ref[...]
    metas = []
    for r0 in range(0, tm, ROUTE_BLOCK):
        rows = slice(r0, r0 + ROUTE_BLOCK)
        na = _rms_rows(oa_ref[rows, :].astype(F32)) * ga_ref[...]
        nb = _rms_rows(ob_ref[rows, :].astype(F32)) * gb_ref[...]
        mixed = jnp.concatenate([na, nb], axis=1).astype(BF16)
        x1 = x_ref[rows, :] + _dot(mixed, wo_ref[...])
        x1_ref[rows, :] = x1
        h = _rms_rows(x1) * gf_ref[...]
        _store_token_tiles(h_ref.at[r0 * tok:(r0 + ROUTE_BLOCK) * tok], h)

        h_hi, h_lo = _split2(h)
        parts = _dot(h_hi, wr_ref[...]) + _dot(h_lo, wr_ref[...])
        logits = parts[:, :LANES] + parts[:, LANES:]
        for q0 in range(0, ROUTE_BLOCK, ROUTE_ROWS):
            meta, base = _route(logits[q0:q0 + ROUTE_ROWS], tri, base)
            metas.append(meta)
    carry_ref[...] = base
    cnt_ref[...] = base
    meta = jnp.concatenate(metas, axis=0)
    meta_ref[...] = meta
    meta_t_ref[...] = meta.T[:SUBLANES, :]


def _out_router(oa, ob, x2d, ga, gb, wo_bf16, gf, wr, tri):
    t, d = x2d.shape
    tm = ROW_TILE
    tok = d // LANES
    row = lambda i: (i, 0)
    fixed = lambda i: (0, 0)
    return pl.pallas_call(
        _out_router_kernel,
        grid=(t // tm,),
        in_specs=[
            pl.BlockSpec((tm, MIX), row),
            pl.BlockSpec((tm, MIX), row),
            pl.BlockSpec((tm, d), row),
            pl.BlockSpec((1, MIX), fixed),
            pl.BlockSpec((1, MIX), fixed),
            pl.BlockSpec((2 * MIX, d), fixed),
            pl.BlockSpec((1, d), fixed),
            pl.BlockSpec((d, 2 * LANES), fixed),
            pl.BlockSpec((ROUTE_ROWS, ROUTE_ROWS), fixed),
        ],
        out_specs=[
            pl.BlockSpec((tm, d), row),
            pl.BlockSpec((tm * tok, LANES), row),
            pl.BlockSpec((tm, LANES), row),
            pl.BlockSpec((None, SUBLANES, tm), lambda i: (i, 0, 0)),
            pl.BlockSpec((1, LANES), fixed),
        ],
        out_shape=[
            jax.ShapeDtypeStruct((t, d), F32),
            jax.ShapeDtypeStruct((t * tok, LANES), F32),
            jax.ShapeDtypeStruct((t, LANES), F32),
            jax.ShapeDtypeStruct((t // tm, SUBLANES, tm), F32),
            jax.ShapeDtypeStruct((1, LANES), F32),
        ],
        scratch_shapes=[pltpu.VMEM((1, LANES), F32)],
        compiler_params=pltpu.CompilerParams(
            dimension_semantics=("arbitrary",), vmem_limit_bytes=VMEM_LIMIT),
        name="out_router",
    )(oa, ob, x2d, ga, gb, wo_bf16, gf, wr, tri)


def _token_copy(src, i, dst, j, tok, sem):
    return pltpu.make_async_copy(src.at[pl.ds(pl.multiple_of(i * tok, tok), tok)],
                                 dst.at[pl.ds(pl.multiple_of(j * tok, tok), tok)], sem)


def _dispatch_kernel(slot_ref, h_ref, xs_in_ref, xs_ref, sem, *, tok):
    del xs_in_ref
    tm = h_ref.shape[0] // tok

    def start(blk, c):
        for u in range(DMA_UNROLL):
            i = blk * DMA_UNROLL + u
            for k in range(TOP_K):
                _token_copy(h_ref, i, xs_ref, slot_ref[0, k * tm + i], tok, sem).start(priority=k)
        return c

    lax.fori_loop(0, tm // DMA_UNROLL, start, 0)
    for k in range(TOP_K):
        pltpu.make_async_copy(h_ref, xs_ref.at[pl.ds(0, tm * tok)], sem).wait()


def _dispatch(slots, h_tok, xs_buf, tok):
    tm = ROW_TILE
    n_tiles = h_tok.shape[0] // (tm * tok)
    return pl.pallas_call(
        functools.partial(_dispatch_kernel, tok=tok),
        grid=(n_tiles,),
        in_specs=[
            pl.BlockSpec((None, 1, TOP_K * tm), lambda i: (i, 0, 0), memory_space=pltpu.SMEM),
            pl.BlockSpec((tm * tok, LANES), lambda i: (i, 0)),
            pl.BlockSpec(memory_space=pl.ANY),
        ],
        out_specs=pl.BlockSpec(memory_space=pl.ANY),
        out_shape=jax.ShapeDtypeStruct(xs_buf.shape, xs_buf.dtype),
        scratch_shapes=[pltpu.SemaphoreType.DMA(())],
        input_output_aliases={2: 0},
        compiler_params=pltpu.CompilerParams(
            dimension_semantics=("arbitrary",), vmem_limit_bytes=VMEM_LIMIT),
        name="dispatch",
    )(slots, h_tok, xs_buf)


def _expert_kernel(te_ref, nv_ref, xs_ref, wgu_ref, wdn_ref, ys_ref, wgu_bf, wdn_bf):
    i = pl.program_id(0)
    d = wgu_ref.shape[0]
    tok = d // LANES
    tg = xs_ref.shape[0] // tok

    @pl.when((i == 0) | (te_ref[i] != te_ref[jnp.maximum(i - 1, 0)]))
    def _():
        wgu_bf[...] = wgu_ref[...].astype(BF16)
        wdn_bf[...] = wdn_ref[...].astype(BF16)

    @pl.when(i < nv_ref[0])
    def _():
        xs = _load_token_tiles(xs_ref, tg, tok)
        gu = _dot(xs.astype(BF16), wgu_bf[...])
        g = gu[:, :D_EXPERT]
        a = (g * (1.0 / (1.0 + jnp.exp(-g)))) * gu[:, D_EXPERT:]
        _store_token_tiles(ys_ref, _dot(a.astype(BF16), wdn_bf[...]))

    @pl.when(i >= nv_ref[0])
    def _():
        ys_ref[...] = jnp.zeros_like(ys_ref)


def _experts(tile_expert, n_valid, xs, wgu, wdn, layer):
    d = wgu.shape[2]
    tok = d // LANES
    tg = GROUP_TILE
    grid_spec = pltpu.PrefetchScalarGridSpec(
        num_scalar_prefetch=2,
        grid=(xs.shape[0] // (tg * tok),),
        in_specs=[
            pl.BlockSpec((tg * tok, LANES), lambda i, te, nv: (i, 0)),
            pl.BlockSpec((None, None, d, 2 * D_EXPERT), lambda i, te, nv: (layer, te[i], 0, 0)),
            pl.BlockSpec((None, None, D_EXPERT, d), lambda i, te, nv: (layer, te[i], 0, 0)),
        ],
        out_specs=pl.BlockSpec((tg * tok, LANES), lambda i, te, nv: (i, 0)),
        scratch_shapes=[pltpu.VMEM((d, 2 * D_EXPERT), BF16), pltpu.VMEM((D_EXPERT, d), BF16)],
    )
    return pl.pallas_call(
        _expert_kernel,
        grid_spec=grid_spec,
        out_shape=jax.ShapeDtypeStruct(xs.shape, F32),
        compiler_params=pltpu.CompilerParams(
            dimension_semantics=("arbitrary",), vmem_limit_bytes=VMEM_LIMIT),
        name="experts",
    )(tile_expert, n_valid, xs, wgu, wdn)


def _combine_kernel(slot_ref, x_ref, meta_ref, ys_ref, out_ref, buf, sem):
    tm, d = x_ref.shape
    tok = d // LANES

    def start(blk, c):
        for u in range(DMA_UNROLL):
            i = blk * DMA_UNROLL + u
            for k in range(TOP_K):
                _token_copy(ys_ref, slot_ref[0, k * tm + i], buf.at[k], i, tok, sem).start(priority=k)
        return c

    lax.fori_loop(0, tm // DMA_UNROLL, start, 0)
    for k in range(TOP_K):
        pltpu.make_async_copy(ys_ref.at[pl.ds(0, tm * tok)], buf.at[k], sem).wait()
    meta = meta_ref[...]
    lane = lax.broadcasted_iota(jnp.int32, (1, LANES), 1)
    g0 = _lane_col(meta, lane, META_G0, 0.0, jnp.sum)
    g1 = _lane_col(meta, lane, META_G1, 0.0, jnp.sum)
    y = g0 * _load_token_tiles(buf.at[0], tm, tok) + g1 * _load_token_tiles(buf.at[1], tm, tok)
    out_ref[...] = x_ref[...] + y


def _combine(slots, x1, meta, ys):
    t, d = x1.shape
    tm = ROW_TILE
    tok = d // LANES
    return pl.pallas_call(
        _combine_kernel,
        grid=(t // tm,),
        in_specs=[
            pl.BlockSpec((None, 1, TOP_K * tm), lambda i: (i, 0, 0), memory_space=pltpu.SMEM),
            pl.BlockSpec((tm, d), lambda i: (i, 0)),
            pl.BlockSpec((tm, LANES), lambda i: (i, 0)),
            pl.BlockSpec(memory_space=pl.ANY),
        ],
        out_specs=pl.BlockSpec((tm, d), lambda i: (i, 0)),
        out_shape=jax.ShapeDtypeStruct((t, d), F32),
        scratch_shapes=[pltpu.VMEM((TOP_K, tm * tok, LANES), F32), pltpu.SemaphoreType.DMA(())],
        compiler_params=pltpu.CompilerParams(
            dimension_semantics=("arbitrary",), vmem_limit_bytes=VMEM_LIMIT),
        name="combine",
    )(slots, x1, meta, ys)


def _rope_expand_matrix():
    src = jnp.arange(LANES)[:, None]
    dst = jnp.arange(LANES)[None, :]
    in_head = dst % HEAD_DIM
    cos_m = (src < ROPE_HALF) & (in_head < ROPE_DIM) & (in_head % ROPE_HALF == src)
    is_sin = (src >= ROPE_HALF) & (src < ROPE_DIM)
    sin_lo = is_sin & (in_head >= ROPE_HALF) & (in_head < ROPE_DIM) & (in_head - ROPE_HALF == src - ROPE_HALF)
    sin_hi = is_sin & (in_head < ROPE_HALF) & (in_head == src - ROPE_HALF)
    e = jnp.concatenate([cos_m.astype(F32), sin_lo.astype(F32), -sin_hi.astype(F32)], axis=1)
    return jnp.concatenate([e, e, e], axis=0).astype(BF16)


def _block_diag_ones():
    i = jnp.arange(2 * LANES)
    return (i[:, None] // HEAD_DIM == i[None, :] // HEAD_DIM).astype(BF16)


def _tile4(g):
    return jnp.tile(g.astype(F32), 2 * LANES // HEAD_DIM)


_B_HEAD_ORDER = tuple(h for p in range(N_HEADS // 2) for h in (p, p + N_HEADS // 2))


def _b_cols():
    order = jnp.asarray(_B_HEAD_ORDER)
    return (order[:, None] * HEAD_DIM + jnp.arange(HEAD_DIM)[None, :]).reshape(-1)


def kernel(x, positions, attn_norm, w_in, q_norm_a, k_norm_a, q_norm_b, k_norm_b, sinks_b,
           out_norm_a, out_norm_b, w_out, ffn_norm, w_router_group, w_router_expert,
           w_gate_up, w_down):
    b, s, d = x.shape
    t = b * s
    depth = w_in.shape[0]
    tok = d // LANES
    assert s % CHUNK == 0 and s % ROW_TILE == 0 and s % SWA_STEP == 0
    assert ROW_TILE % (2 * SUBLANES * max(DILATIONS)) == 0 and ROW_TILE % ROUTE_BLOCK == 0 and ROUTE_BLOCK % ROUTE_ROWS == 0
    assert d == 2 * MIX and tok == SUBLANES

    inv_freq = ROPE_THETA ** (-jnp.arange(0, ROPE_DIM, 2, dtype=F32) / ROPE_DIM)
    ang = positions.astype(F32).reshape(t, 1) * inv_freq[None, :]
    rope_tab = jnp.concatenate(
        [jnp.cos(ang), jnp.sin(ang), jnp.zeros((t, LANES - ROPE_DIM), F32)], axis=1)
    e3 = _rope_expand_matrix()
    bd = _block_diag_ones()
    row_i = jnp.arange(ROUTE_ROWS)
    tri = (row_i[None, :] < row_i[:, None]).astype(BF16)
    bcols = _b_cols()
    q_scale = HEAD_DIM ** -0.5 * LOG2E
    n_sorted = t * TOP_K + N_EXPERTS * GROUP_TILE
    n_tiles = n_sorted // GROUP_TILE
    xs_buf = jnp.zeros((n_sorted * tok, LANES), F32)
    nb = s // BLOCK

    x2d = x.reshape(t, d)
    for l in range(depth):
        qb0 = 3 * MIX
        w_l = w_in[l]
        w_l = jnp.concatenate([w_l[:, :qb0], w_l[:, qb0:qb0 + MIX][:, bcols], w_l[:, qb0 + MIX:]], axis=1)
        head_gains = jnp.stack([_tile4(q_norm_a[l]) * q_scale, _tile4(k_norm_a[l]),
                                _tile4(q_norm_b[l]) * q_scale, _tile4(k_norm_b[l])])
        sink = sinks_b[l][jnp.asarray(_B_HEAD_ORDER)].astype(F32) * LOG2E
        gb_perm = out_norm_b[l][bcols]
        wo_l = jnp.concatenate([w_out[l][:MIX], w_out[l][MIX:][bcols]], axis=0).astype(BF16)
        wr = jnp.concatenate(
            [w_router_group[l], jnp.zeros((d, EXPERT_LANE0 - N_GROUPS), F32), w_router_expert[l],
             jnp.zeros((d, LANES - EXPERT_LANE0 - N_EXPERTS), F32)], axis=1)
        wr_hi = wr.astype(BF16)
        wr_split = jnp.concatenate([wr_hi, (wr - wr_hi.astype(F32)).astype(BF16)], axis=1)

        (qa, ka, va, qa4, ka4, va4, qa16, ka16, va16, qb, kb, vb) = _proj(
            x2d, b, attn_norm[l].reshape(1, d), w_l.astype(BF16), rope_tab, e3, bd, head_gains)
        blocks = lambda a: a.reshape(b, nb, BLOCK, a.shape[-1])
        class_blocks = lambda a: a.reshape(b, a.shape[1], a.shape[2] // BLOCK, BLOCK, MIX)
        oa = _dilated_attention(tuple(map(blocks, (qa, ka, va))),
                                tuple(map(class_blocks, (qa4, ka4, va4))),
                                tuple(map(class_blocks, (qa16, ka16, va16))))
        ob = _swa_attention(blocks(qb), blocks(kb), blocks(vb), sink)

        x1, h_tok, meta, meta_t, counts = _out_router(
            oa.reshape(t, MIX), ob.reshape(t, MIX), x2d, out_norm_a[l].reshape(1, MIX),
            gb_perm.reshape(1, MIX), wo_l, ffn_norm[l].reshape(1, d), wr_split, tri)

        cnt = counts[0, :N_EXPERTS].astype(jnp.int32)
        padded = ((cnt + GROUP_TILE - 1) // GROUP_TILE) * GROUP_TILE
        ends = jnp.cumsum(padded)
        offs = ends - padded
        eid = meta_t[:, META_E0:META_E1 + 1, :].astype(jnp.int32)
        rank = meta_t[:, META_R0:META_R1 + 1, :].astype(jnp.int32)
        slots = (offs[eid] + rank).reshape(t // ROW_TILE, 1, TOP_K * ROW_TILE)
        tile_start = jnp.arange(n_tiles, dtype=jnp.int32) * GROUP_TILE
        tile_expert = jnp.minimum(
            jnp.sum((tile_start[:, None] >= ends[None, :]).astype(jnp.int32), axis=1), N_EXPERTS - 1)
        n_valid = (ends[-1:] // GROUP_TILE).astype(jnp.int32)

        xs_buf = _dispatch(slots, h_tok, xs_buf, tok)
        ys = _experts(tile_expert, n_valid, xs_buf, w_gate_up, w_down, l)
        x2d = _combine(slots, x1, meta, ys)
    return x2d.reshape(b, s, d)
```

```python
import functools
import math

import jax
import jax.numpy as jnp
from jax import lax
from jax.experimental import pallas as pl
from jax.experimental.pallas import tpu as pltpu

F32 = jnp.float32
BF16 = jnp.bfloat16

HEAD_DIM = 64
N_HEADS = 8
MIX = N_HEADS * HEAD_DIM
N_KV_B = 2
KV_B = N_KV_B * HEAD_DIM
DILATIONS = (1, 4, 16)
BLOCK = 128
CHUNK = BLOCK * max(DILATIONS)
SWA_BACK = 127
SWA_STEP = 8 * BLOCK
ROPE_DIM = HEAD_DIM // 4
ROPE_HALF = ROPE_DIM // 2
ROPE_THETA = 500000.0
N_GROUPS = 4
EXPERTS_PER_GROUP = 8
N_EXPERTS = N_GROUPS * EXPERTS_PER_GROUP
TOP_K = 2
D_EXPERT = 512
EPS = 1e-6
LOG2E = math.log2(math.e)

LANES = 128
SUBLANES = 8
PAIR = 2 * HEAD_DIM
HALF_W = MIX // 2
EXPERT_LANE0 = 32
ROW_TILE = 512
ROUTE_BLOCK = 256
ROUTE_ROWS = 128
GROUP_TILE = 512
DMA_UNROLL = 8
VMEM_LIMIT = 48 * 1024 * 1024

META_E0, META_E1, META_G0, META_G1, META_R0, META_R1 = 0, 1, 2, 3, 4, 5


def _split2(a):
    hi = a.astype(BF16)
    lo = (a - hi.astype(F32)).astype(BF16)
    return hi, lo


def _split3(a):
    hi = a.astype(BF16)
    r = a - hi.astype(F32)
    mid = r.astype(BF16)
    lo = (r - mid.astype(F32)).astype(BF16)
    return hi, mid, lo


def _dot(a, b):
    return jnp.dot(a, b, preferred_element_type=F32)


def _rms_rows(a):
    return a * lax.rsqrt(jnp.mean(a * a, axis=-1, keepdims=True) + EPS)


def _lane_col(tile, lane, idx, fill, reduce):
    return reduce(jnp.where(lane == idx, tile, fill), axis=1, keepdims=True)


def _store_token_tiles(ref, val):
    rows, width = val.shape
    n = width // LANES
    for c in range(n):
        ref[pl.ds(c, rows, stride=n), :] = val[:, c * LANES:(c + 1) * LANES]


def _load_token_tiles(ref, rows, n):
    return jnp.concatenate([ref[pl.ds(c, rows, stride=n), :] for c in range(n)], axis=1)


def _proj_kernel(x_ref, g_ref, w_ref, rope_ref, e3_ref, bd_ref, hg_ref,
                 qa_ref, ka_ref, va_ref, qa4_ref, ka4_ref, va4_ref, qa16_ref, ka16_ref, va16_ref,
                 qb_ref, kb_ref, vb_ref, scr):
    h = _rms_rows(x_ref[...]) * g_ref[...]
    acc = _dot(h.astype(BF16), w_ref[...])
    tm = acc.shape[0]

    hi, mid, lo = _split3(rope_ref[...])
    tab = _dot(jnp.concatenate([hi, mid, lo], axis=1), e3_ref[...])
    lane = lax.broadcasted_iota(jnp.int32, (1, LANES), 1)
    not_rope = ((lane & (HEAD_DIM - 1)) >= ROPE_DIM).astype(F32)
    c = tab[:, :LANES] + not_rope
    s_lo = tab[:, LANES:2 * LANES]
    s_hi = tab[:, 2 * LANES:]
    bd = bd_ref[...]

    def norm_rope(a, gain):
        w = a.shape[1]
        a2_hi, a2_lo = _split2(a * a)
        ss = _dot(a2_hi, bd[:w, :w]) + _dot(a2_lo, bd[:w, :w])
        y = (a * lax.rsqrt(ss * (1.0 / HEAD_DIM) + EPS)) * gain
        outs = []
        for j in range(w // LANES):
            yj = y[:, j * LANES:(j + 1) * LANES]
            outs.append(yj * c + pltpu.roll(yj, ROPE_HALF, 1) * s_lo
                        + pltpu.roll(yj, LANES - ROPE_HALF, 1) * s_hi)
        return outs[0] if len(outs) == 1 else jnp.concatenate(outs, axis=1)

    def normed(col0, width, gain_row):
        outs = []
        for c0 in range(col0, col0 + width, 2 * LANES):
            wd = min(2 * LANES, col0 + width - c0)
            outs.append(norm_rope(acc[:, c0:c0 + wd], hg_ref[gain_row:gain_row + 1, :wd]))
        return outs[0] if len(outs) == 1 else jnp.concatenate(outs, axis=1)

    def emit_by_class(val, nat_ref, class_refs):
        nat_ref[...] = val.astype(BF16)
        for ct in range(MIX // LANES):
            scr[ct] = val[:, ct * LANES:(ct + 1) * LANES]
        for ref, dil in class_refs:
            for r in range(dil):
                rows = [scr[ct, pl.ds(r, tm // dil, stride=dil), :] for ct in range(MIX // LANES)]
                ref[r] = jnp.concatenate(rows, axis=1).astype(BF16)

    emit_by_class(normed(0, MIX, 0), qa_ref, ((qa4_ref, 4), (qa16_ref, 16)))
    emit_by_class(normed(MIX, MIX, 1), ka_ref, ((ka4_ref, 4), (ka16_ref, 16)))
    emit_by_class(acc[:, 2 * MIX:3 * MIX], va_ref, ((va4_ref, 4), (va16_ref, 16)))
    qb_ref[...] = normed(3 * MIX, MIX, 2).astype(BF16)
    kb_ref[...] = normed(4 * MIX, KV_B, 3).astype(BF16)
    vb_ref[...] = acc[:, 4 * MIX + KV_B:].astype(BF16)


def _proj(x2d, b, gain, w_bf16, rope_tab, e3, bd, head_gains):
    t, d = x2d.shape
    s = t // b
    pw = w_bf16.shape[1]
    tm = ROW_TILE
    nt = s // tm
    row = lambda bi, i: (bi * nt + i, 0)
    fixed = lambda bi, i: (0, 0)
    by_class = lambda bi, i: (bi, 0, i, 0)
    nat = lambda w: (jax.ShapeDtypeStruct((t, w), BF16), pl.BlockSpec((tm, w), row))
    grouped = lambda dil: (jax.ShapeDtypeStruct((b, dil, s // dil, MIX), BF16),
                           pl.BlockSpec((None, dil, tm // dil, MIX), by_class))
    outs = [nat(MIX)] * 3 + [grouped(4)] * 3 + [grouped(16)] * 3 + [nat(MIX), nat(KV_B), nat(KV_B)]
    return pl.pallas_call(
        _proj_kernel,
        grid=(b, nt),
        in_specs=[
            pl.BlockSpec((tm, d), row),
            pl.BlockSpec((1, d), fixed),
            pl.BlockSpec((d, pw), fixed),
            pl.BlockSpec((tm, LANES), row),
            pl.BlockSpec(e3.shape, fixed),
            pl.BlockSpec(bd.shape, fixed),
            pl.BlockSpec(head_gains.shape, fixed),
        ],
        out_specs=[o[1] for o in outs],
        out_shape=[o[0] for o in outs],
        scratch_shapes=[pltpu.VMEM((MIX // LANES, tm, LANES), F32)],
        compiler_params=pltpu.CompilerParams(
            dimension_semantics=("arbitrary", "arbitrary"), vmem_limit_bytes=VMEM_LIMIT),
        name="proj",
    )(x2d, gain, w_bf16, rope_tab, e3, bd, head_gains)


def _band(min_back, first_key):
    qi = lax.broadcasted_iota(jnp.int32, (BLOCK, 2 * BLOCK), 0)
    kj = lax.broadcasted_iota(jnp.int32, (BLOCK, 2 * BLOCK), 1)
    valid = (kj >= qi + min_back) & (kj <= qi + BLOCK)
    return valid if first_key is None else valid & (kj >= first_key)


def _head_scores(q_pair, kcat, sel, valid):
    qm = jnp.where(sel, q_pair, jnp.zeros_like(q_pair))
    s = lax.dot_general(qm, kcat, (((1,), (1,)), ((), ())), preferred_element_type=F32)
    return jnp.where(valid, s, -jnp.inf)


def _weighted_values(pr, v_pair, sel):
    return _dot(pr.astype(BF16), jnp.where(sel, v_pair, jnp.ones_like(v_pair)))


def _normalise_pair(res_even, res_odd, even, extra=None):
    num = jnp.where(even, res_even, res_odd)
    den = pltpu.roll(jnp.where(even, res_odd, res_even), HEAD_DIM, 1)
    return num / (den if extra is None else den + extra)


def _with_halo(prev, cur):
    return jnp.concatenate([prev, cur], axis=0)


def _swa_kernel(q_ref, kp_ref, k_ref, vp_ref, v_ref, sink_ref, o_ref):
    first_key = jnp.where(pl.program_id(1) > 0, 0, BLOCK)
    band = _band(BLOCK - SWA_BACK, None)
    band_first = _band(BLOCK - SWA_BACK, first_key)
    lane = lax.broadcasted_iota(jnp.int32, (1, LANES), 1)
    even = lane < HEAD_DIM
    for j in range(SWA_STEP // BLOCK):
        valid = band_first if j == 0 else band
        kcat = _with_halo(kp_ref[...] if j == 0 else k_ref[j - 1], k_ref[j])
        vcat = _with_halo(vp_ref[...] if j == 0 else v_ref[j - 1], v_ref[j])
        for p in range(N_HEADS // 2):
            cols = slice(p * PAIR, (p + 1) * PAIR)
            q_pair = q_ref[j, :, cols]
            res, sink_term = [], []
            for half in range(2):
                sink = sink_ref[2 * p + half]
                sel = even if half == 0 else jnp.logical_not(even)
                s = _head_scores(q_pair, kcat, sel, valid)
                m = jnp.maximum(jnp.max(s, axis=1, keepdims=True), sink)
                res.append(_weighted_values(jnp.exp2(s - m), vcat, sel))
                sink_term.append(jnp.exp2(sink - m))
            extra = jnp.where(even, sink_term[0], sink_term[1])
            o_ref[j, :, cols] = _normalise_pair(res[0], res[1], even, extra).astype(o_ref.dtype)


def _swa_attention(q, k, v, sink):
    b, nb = q.shape[:2]
    step = SWA_STEP // BLOCK
    cur = lambda w: pl.BlockSpec((None, step, BLOCK, w), lambda bi, c: (bi, c, 0, 0))
    prev = pl.BlockSpec((None, None, BLOCK, KV_B),
                        lambda bi, c: (bi, jnp.maximum(step * c - 1, 0), 0, 0))
    return pl.pallas_call(
        _swa_kernel,
        grid=(b, nb // step),
        in_specs=[cur(MIX), prev, cur(KV_B), prev, cur(KV_B), pl.BlockSpec(memory_space=pltpu.SMEM)],
        out_specs=cur(MIX),
        out_shape=jax.ShapeDtypeStruct(q.shape, BF16),
        compiler_params=pltpu.CompilerParams(
            dimension_semantics=("arbitrary", "arbitrary"), vmem_limit_bytes=VMEM_LIMIT),
        name="swa",
    )(q, k, k, v, v, sink)


def _dilated_kernel(qn, knp, kn, vnp, vn, q4, k4p, k4, v4p, v4, q16, k16p, k16, v16p, v16,
                    o_ref, m_st, r_st):
    n_pairs = HALF_W // PAIR
    first_key = jnp.where(pl.program_id(1) > 0, 0, BLOCK)
    band = _band(0, None)
    band_first = _band(0, first_key)
    lane = lax.broadcasted_iota(jnp.int32, (1, LANES), 1)
    even = lane < HEAD_DIM

    def tile(q, kcat, vcat, valid, row0, stride, init):
        rows = pl.ds(row0, BLOCK) if stride == 1 else pl.ds(row0, BLOCK, stride=stride)
        for p in range(n_pairs):
            cols = slice(p * PAIR, (p + 1) * PAIR)
            for half in range(2):
                h = 2 * p + half
                sel = even if half == 0 else jnp.logical_not(even)
                s = _head_scores(q[:, cols], kcat[:, cols], sel, valid)
                m_tile = jnp.max(s, axis=1, keepdims=True)
                if init:
                    m_new = jnp.broadcast_to(m_tile, (BLOCK, LANES))
                    res = _weighted_values(jnp.exp2(s - m_tile), vcat[:, cols], sel)
                else:
                    m_old = m_st[h, rows, :]
                    m_new = jnp.maximum(m_old, m_tile)
                    pr = jnp.exp2(s - jnp.concatenate([m_new, m_new], axis=1))
                    res = (_weighted_values(pr, vcat[:, cols], sel)
                           + jnp.exp2(m_old - m_new) * r_st[h, rows, :])
                m_st[h, rows, :] = m_new
                r_st[h, rows, :] = res

    def d16(r, carry):
        tile(q16[r], _with_halo(k16p[r], k16[r]), _with_halo(v16p[r], v16[r]), band_first, r, 16, True)
        return carry

    lax.fori_loop(0, 16, d16, 0, unroll=4)

    tile(qn[0], _with_halo(knp[...], kn[0]), _with_halo(vnp[...], vn[0]), band_first, 0, 1, False)

    def d1(j, carry):
        tile(qn[j], _with_halo(kn[j - 1], kn[j]), _with_halo(vn[j - 1], vn[j]),
             band, pl.multiple_of(j * BLOCK, BLOCK), 1, False)
        return carry

    lax.fori_loop(1, CHUNK // BLOCK, d1, 0, unroll=5)

    def d4(r, carry):
        tile(q4[r, 0], _with_halo(k4p[r], k4[r, 0]), _with_halo(v4p[r], v4[r, 0]), band_first, r, 4, False)
        for j in range(1, CHUNK // (4 * BLOCK)):
            tile(q4[r, j], _with_halo(k4[r, j - 1], k4[r, j]), _with_halo(v4[r, j - 1], v4[r, j]),
                 band, r + 4 * j * BLOCK, 4, False)
        return carry

    lax.fori_loop(0, 4, d4, 0)

    def finish(j, carry):
        rows = pl.ds(pl.multiple_of(j * BLOCK, BLOCK), BLOCK)
        for p in range(n_pairs):
            out = _normalise_pair(r_st[2 * p, rows, :], r_st[2 * p + 1, rows, :], even)
            o_ref[j, :, p * PAIR:(p + 1) * PAIR] = out.astype(o_ref.dtype)
        return carry

    lax.fori_loop(0, CHUNK // BLOCK, finish, 0, unroll=2)


def _dilated_attention(nat, by4, by16):
    b, nb = nat[0].shape[:2]
    per_chunk = CHUNK // BLOCK
    n_chunks = nb // per_chunk

    def specs(dil):
        step = per_chunk // dil
        if dil == 1:
            cur = pl.BlockSpec((None, step, BLOCK, HALF_W), lambda bi, c, hf: (bi, c, 0, hf))
            prev = pl.BlockSpec((None, None, BLOCK, HALF_W),
                                lambda bi, c, hf: (bi, jnp.maximum(step * c - 1, 0), 0, hf))
        else:
            blocks = None if step == 1 else step
            cur = pl.BlockSpec((None, dil, blocks, BLOCK, HALF_W), lambda bi, c, hf: (bi, 0, c, 0, hf))
            prev = pl.BlockSpec((None, dil, None, BLOCK, HALF_W),
                                lambda bi, c, hf: (bi, 0, jnp.maximum(step * c - 1, 0), 0, hf))
        return cur, prev

    args, in_specs = [], []
    for (q, k, v), dil in ((nat, 1), (by4, 4), (by16, 16)):
        cur, prev = specs(dil)
        args += [q, k, k, v, v]
        in_specs += [cur, prev, cur, prev, cur]
    return pl.pallas_call(
        _dilated_kernel,
        grid=(b, n_chunks, MIX // HALF_W),
        in_specs=in_specs,
        out_specs=specs(1)[0],
        out_shape=jax.ShapeDtypeStruct(nat[0].shape, BF16),
        scratch_shapes=[pltpu.VMEM((HALF_W // HEAD_DIM, CHUNK, LANES), F32),
                        pltpu.VMEM((HALF_W // HEAD_DIM, CHUNK, LANES), F32)],
        compiler_params=pltpu.CompilerParams(
            dimension_semantics=("arbitrary", "arbitrary", "arbitrary"),
            vmem_limit_bytes=VMEM_LIMIT),
        name="dilated",
    )(*args)


def _pick_experts(logits):
    n = logits.shape[0]
    lane = lax.broadcasted_iota(jnp.int32, (n, LANES), 1)
    far = jnp.int32(2 * LANES)
    neg = -jnp.inf

    def first_argmax(vals):
        mx = jnp.max(vals, axis=1, keepdims=True)
        idx = jnp.min(jnp.where(vals == mx, lane, far), axis=1, keepdims=True)
        return mx, idx

    gl = jnp.where(lane < N_GROUPS, logits, neg)
    gmax, gidx = first_argmax(gl)
    g_top = 1.0 / jnp.sum(jnp.exp(gl - gmax), axis=1, keepdims=True)

    lo_lane = EXPERT_LANE0 + EXPERTS_PER_GROUP * gidx
    el = jnp.where((lane >= lo_lane) & (lane < lo_lane + EXPERTS_PER_GROUP), logits, neg)
    m1, i1 = first_argmax(el)
    m2, i2 = first_argmax(jnp.where(lane == i1, neg, el))
    t2 = jnp.exp(m2 - m1)
    return i1 - EXPERT_LANE0, i2 - EXPERT_LANE0, g_top / (1.0 + t2), g_top * t2 / (1.0 + t2)


def _out_router_kernel(oa_ref, ob_ref, x_ref, ga_ref, gb_ref, wo_ref, gf_ref, wr_ref, tri_ref,
                       x1_ref, h_ref, meta_ref, meta_t_ref, cnt_ref, carry_ref):
    @pl.when(pl.program_id(0) == 0)
    def _():
        carry_ref[...] = jnp.zeros_like(carry_ref)

    tm, d = x_ref.shape
    tok = d // LANES
    logits = []
    for r0 in range(0, tm, ROUTE_BLOCK):
        rows = slice(r0, r0 + ROUTE_BLOCK)
        na = _rms_rows(oa_ref[rows, :].astype(F32)) * ga_ref[...]
        nb = _rms_rows(ob_ref[rows, :].astype(F32)) * gb_ref[...]
        mixed = jnp.concatenate([na, nb], axis=1).astype(BF16)
        x1 = x_ref[rows, :] + _dot(mixed, wo_ref[...])
        x1_ref[rows, :] = x1
        h = _rms_rows(x1) * gf_ref[...]
        _store_token_tiles(h_ref.at[r0 * tok:(r0 + ROUTE_BLOCK) * tok], h)

        h_hi, h_lo = _split2(h)
        parts = _dot(h_hi, wr_ref[...]) + _dot(h_lo, wr_ref[...])
        logits.append(parts[:, :LANES] + parts[:, LANES:])

    e0, e1, gate0, gate1 = _pick_experts(jnp.concatenate(logits, axis=0))

    lane = lax.broadcasted_iota(jnp.int32, (tm, LANES), 1)
    oh0 = lane == e0
    oh1 = lane == e1
    cum = _dot(tri_ref[...], jnp.concatenate([oh0.astype(BF16), oh1.astype(BF16)], axis=1))
    tot0 = jnp.sum(oh0.astype(F32), axis=0, keepdims=True)
    tot1 = jnp.sum(oh1.astype(F32), axis=0, keepdims=True)
    base = carry_ref[...]
    rank0 = jnp.sum(jnp.where(oh0, cum[:, :LANES] + base, 0.0), axis=1, keepdims=True)
    rank1 = jnp.sum(jnp.where(oh1, cum[:, LANES:] + (base + tot0), 0.0), axis=1, keepdims=True)
    base = base + tot0 + tot1
    carry_ref[...] = base
    cnt_ref[...] = base

    meta = jnp.zeros((tm, LANES), F32)
    for ln, val in ((META_E0, e0.astype(F32)), (META_E1, e1.astype(F32)), (META_G0, gate0),
                    (META_G1, gate1), (META_R0, rank0), (META_R1, rank1)):
        meta = jnp.where(lane == ln, val, meta)
    meta_ref[...] = meta
    meta_t_ref[...] = meta.T[:SUBLANES, :]


def _out_router(oa, ob, x2d, ga, gb, wo_bf16, gf, wr, tri):
    t, d = x2d.shape
    tm = ROW_TILE
    tok = d // LANES
    row = lambda i: (i, 0)
    fixed = lambda i: (0, 0)
    return pl.pallas_call(
        _out_router_kernel,
        grid=(t // tm,),
        in_specs=[
            pl.BlockSpec((tm, MIX), row),
            pl.BlockSpec((tm, MIX), row),
            pl.BlockSpec((tm, d), row),
            pl.BlockSpec((1, MIX), fixed),
            pl.BlockSpec((1, MIX), fixed),
            pl.BlockSpec((2 * MIX, d), fixed),
            pl.BlockSpec((1, d), fixed),
            pl.BlockSpec((d, 2 * LANES), fixed),
            pl.BlockSpec((tm, tm), fixed),
        ],
        out_specs=[
            pl.BlockSpec((tm, d), row),
            pl.BlockSpec((tm * tok, LANES), row),
            pl.BlockSpec((tm, LANES), row),
            pl.BlockSpec((None, SUBLANES, tm), lambda i: (i, 0, 0)),
            pl.BlockSpec((1, LANES), fixed),
        ],
        out_shape=[
            jax.ShapeDtypeStruct((t, d), F32),
            jax.ShapeDtypeStruct((t * tok, LANES), F32),
            jax.ShapeDtypeStruct((t, LANES), F32),
            jax.ShapeDtypeStruct((t // tm, SUBLANES, tm), F32),
            jax.ShapeDtypeStruct((1, LANES), F32),
        ],
        scratch_shapes=[pltpu.VMEM((1, LANES), F32)],
        compiler_params=pltpu.CompilerParams(
            dimension_semantics=("arbitrary",), vmem_limit_bytes=VMEM_LIMIT),
        name="out_router",
    )(oa, ob, x2d, ga, gb, wo_bf16, gf, wr, tri)


def _token_copy(src, i, dst, j, tok, sem):
    return pltpu.make_async_copy(src.at[pl.ds(pl.multiple_of(i * tok, tok), tok)],
                                 dst.at[pl.ds(pl.multiple_of(j * tok, tok), tok)], sem)


def _dispatch_kernel(slot_ref, h_ref, xs_in_ref, xs_ref, sem, *, tok):
    del xs_in_ref
    tm = h_ref.shape[0] // tok

    def start(blk, c):
        for u in range(DMA_UNROLL):
            i = blk * DMA_UNROLL + u
            for k in range(TOP_K):
                _token_copy(h_ref, i, xs_ref, slot_ref[0, k * tm + i], tok, sem).start(priority=k)
        return c

    lax.fori_loop(0, tm // DMA_UNROLL, start, 0)
    for k in range(TOP_K):
        pltpu.make_async_copy(h_ref, xs_ref.at[pl.ds(0, tm * tok)], sem).wait()


def _dispatch(slots, h_tok, xs_buf, tok):
    tm = ROW_TILE
    n_tiles = h_tok.shape[0] // (tm * tok)
    return pl.pallas_call(
        functools.partial(_dispatch_kernel, tok=tok),
        grid=(n_tiles,),
        in_specs=[
            pl.BlockSpec((None, 1, TOP_K * tm), lambda i: (i, 0, 0), memory_space=pltpu.SMEM),
            pl.BlockSpec((tm * tok, LANES), lambda i: (i, 0)),
            pl.BlockSpec(memory_space=pl.ANY),
        ],
        out_specs=pl.BlockSpec(memory_space=pl.ANY),
        out_shape=jax.ShapeDtypeStruct(xs_buf.shape, xs_buf.dtype),
        scratch_shapes=[pltpu.SemaphoreType.DMA(())],
        input_output_aliases={2: 0},
        compiler_params=pltpu.CompilerParams(
            dimension_semantics=("arbitrary",), vmem_limit_bytes=VMEM_LIMIT),
        name="dispatch",
    )(slots, h_tok, xs_buf)


def _expert_kernel(te_ref, nv_ref, xs_ref, wgu_ref, wdn_ref, ys_ref, wgu_bf, wdn_bf):
    i = pl.program_id(0)
    d = wgu_ref.shape[0]
    tok = d // LANES
    tg = xs_ref.shape[0] // tok

    @pl.when((i == 0) | (te_ref[i] != te_ref[jnp.maximum(i - 1, 0)]))
    def _():
        wgu_bf[...] = wgu_ref[...].astype(BF16)
        wdn_bf[...] = wdn_ref[...].astype(BF16)

    @pl.when(i < nv_ref[0])
    def _():
        xs = _load_token_tiles(xs_ref, tg, tok)
        gu = _dot(xs.astype(BF16), wgu_bf[...])
        g = gu[:, :D_EXPERT]
        a = (g * (1.0 / (1.0 + jnp.exp(-g)))) * gu[:, D_EXPERT:]
        _store_token_tiles(ys_ref, _dot(a.astype(BF16), wdn_bf[...]))

    @pl.when(i >= nv_ref[0])
    def _():
        ys_ref[...] = jnp.zeros_like(ys_ref)


def _experts(tile_expert, n_valid, xs, wgu, wdn, layer):
    d = wgu.shape[2]
    tok = d // LANES
    tg = GROUP_TILE
    grid_spec = pltpu.PrefetchScalarGridSpec(
        num_scalar_prefetch=2,
        grid=(xs.shape[0] // (tg * tok),),
        in_specs=[
            pl.BlockSpec((tg * tok, LANES), lambda i, te, nv: (i, 0)),
            pl.BlockSpec((None, None, d, 2 * D_EXPERT), lambda i, te, nv: (layer, te[i], 0, 0)),
            pl.BlockSpec((None, None, D_EXPERT, d), lambda i, te, nv: (layer, te[i], 0, 0)),
        ],
        out_specs=pl.BlockSpec((tg * tok, LANES), lambda i, te, nv: (i, 0)),
        scratch_shapes=[pltpu.VMEM((d, 2 * D_EXPERT), BF16), pltpu.VMEM((D_EXPERT, d), BF16)],
    )
    return pl.pallas_call(
        _expert_kernel,
        grid_spec=grid_spec,
        out_shape=jax.ShapeDtypeStruct(xs.shape, F32),
        compiler_params=pltpu.CompilerParams(
            dimension_semantics=("arbitrary",), vmem_limit_bytes=VMEM_LIMIT),
        name="experts",
    )(tile_expert, n_valid, xs, wgu, wdn)


def _combine_kernel(slot_ref, x_ref, meta_ref, ys_ref, out_ref, buf, sem):
    tm, d = x_ref.shape
    tok = d // LANES

    def start(blk, c):
        for u in range(DMA_UNROLL):
            i = blk * DMA_UNROLL + u
            for k in range(TOP_K):
                _token_copy(ys_ref, slot_ref[0, k * tm + i], buf.at[k], i, tok, sem).start(priority=k)
        return c

    lax.fori_loop(0, tm // DMA_UNROLL, start, 0)
    for k in range(TOP_K):
        pltpu.make_async_copy(ys_ref.at[pl.ds(0, tm * tok)], buf.at[k], sem).wait()
    meta = meta_ref[...]
    lane = lax.broadcasted_iota(jnp.int32, (1, LANES), 1)
    g0 = _lane_col(meta, lane, META_G0, 0.0, jnp.sum)
    g1 = _lane_col(meta, lane, META_G1, 0.0, jnp.sum)
    y = g0 * _load_token_tiles(buf.at[0], tm, tok) + g1 * _load_token_tiles(buf.at[1], tm, tok)
    out_ref[...] = x_ref[...] + y


def _combine(slots, x1, meta, ys):
    t, d = x1.shape
    tm = ROW_TILE
    tok = d // LANES
    return pl.pallas_call(
        _combine_kernel,
        grid=(t // tm,),
        in_specs=[
            pl.BlockSpec((None, 1, TOP_K * tm), lambda i: (i, 0, 0), memory_space=pltpu.SMEM),
            pl.BlockSpec((tm, d), lambda i: (i, 0)),
            pl.BlockSpec((tm, LANES), lambda i: (i, 0)),
            pl.BlockSpec(memory_space=pl.ANY),
        ],
        out_specs=pl.BlockSpec((tm, d), lambda i: (i, 0)),
        out_shape=jax.ShapeDtypeStruct((t, d), F32),
        scratch_shapes=[pltpu.VMEM((TOP_K, tm * tok, LANES), F32), pltpu.SemaphoreType.DMA(())],
        compiler_params=pltpu.CompilerParams(
            dimension_semantics=("arbitrary",), vmem_limit_bytes=VMEM_LIMIT),
        name="combine",
    )(slots, x1, meta, ys)


def _rope_expand_matrix():
    src = jnp.arange(LANES)[:, None]
    dst = jnp.arange(LANES)[None, :]
    in_head = dst % HEAD_DIM
    cos_m = (src < ROPE_HALF) & (in_head < ROPE_DIM) & (in_head % ROPE_HALF == src)
    is_sin = (src >= ROPE_HALF) & (src < ROPE_DIM)
    sin_lo = is_sin & (in_head >= ROPE_HALF) & (in_head < ROPE_DIM) & (in_head - ROPE_HALF == src - ROPE_HALF)
    sin_hi = is_sin & (in_head < ROPE_HALF) & (in_head == src - ROPE_HALF)
    e = jnp.concatenate([cos_m.astype(F32), sin_lo.astype(F32), -sin_hi.astype(F32)], axis=1)
    return jnp.concatenate([e, e, e], axis=0).astype(BF16)


def _block_diag_ones():
    i = jnp.arange(2 * LANES)
    return (i[:, None] // HEAD_DIM == i[None, :] // HEAD_DIM).astype(BF16)


def _tile4(g):
    return jnp.tile(g.astype(F32), 2 * LANES // HEAD_DIM)


_B_HEAD_ORDER = tuple(h for p in range(N_HEADS // 2) for h in (p, p + N_HEADS // 2))


def _b_cols():
    order = jnp.asarray(_B_HEAD_ORDER)
    return (order[:, None] * HEAD_DIM + jnp.arange(HEAD_DIM)[None, :]).reshape(-1)


def kernel(x, positions, attn_norm, w_in, q_norm_a, k_norm_a, q_norm_b, k_norm_b, sinks_b,
           out_norm_a, out_norm_b, w_out, ffn_norm, w_router_group, w_router_expert,
           w_gate_up, w_down):
    b, s, d = x.shape
    t = b * s
    depth = w_in.shape[0]
    tok = d // LANES
    assert s % CHUNK == 0 and s % ROW_TILE == 0 and s % SWA_STEP == 0
    assert ROW_TILE % (2 * SUBLANES * max(DILATIONS)) == 0 and ROW_TILE % ROUTE_BLOCK == 0 and ROUTE_BLOCK % ROUTE_ROWS == 0
    assert d == 2 * MIX and tok == SUBLANES

    inv_freq = ROPE_THETA ** (-jnp.arange(0, ROPE_DIM, 2, dtype=F32) / ROPE_DIM)
    ang = positions.astype(F32)[None, :, :] * inv_freq[:, None, None]
    cos_sin = jnp.moveaxis(jnp.concatenate([jnp.cos(ang), jnp.sin(ang)], axis=0), 0, -1)
    rope_tab = jnp.pad(cos_sin, ((0, 0), (0, 0), (0, LANES - ROPE_DIM))).reshape(t, LANES)
    e3 = _rope_expand_matrix()
    bd = _block_diag_ones()
    row_i = jnp.arange(ROW_TILE)
    tri = (row_i[None, :] < row_i[:, None]).astype(BF16)
    bcols = _b_cols()
    q_scale = HEAD_DIM ** -0.5 * LOG2E
    n_sorted = t * TOP_K + N_EXPERTS * GROUP_TILE
    n_tiles = n_sorted // GROUP_TILE
    xs_buf = jnp.zeros((n_sorted * tok, LANES), F32)
    nb = s // BLOCK

    x2d = x.reshape(t, d)
    for l in range(depth):
        qb0 = 3 * MIX
        w_l = w_in[l]
        w_l = jnp.concatenate([w_l[:, :qb0], w_l[:, qb0:qb0 + MIX][:, bcols], w_l[:, qb0 + MIX:]], axis=1)
        head_gains = jnp.stack([_tile4(q_norm_a[l]) * q_scale, _tile4(k_norm_a[l]),
                                _tile4(q_norm_b[l]) * q_scale, _tile4(k_norm_b[l])])
        sink = sinks_b[l][jnp.asarray(_B_HEAD_ORDER)].astype(F32) * LOG2E
        gb_perm = out_norm_b[l][bcols]
        wo_l = jnp.concatenate([w_out[l][:MIX], w_out[l][MIX:][bcols]], axis=0).astype(BF16)
        wr = jnp.concatenate(
            [w_router_group[l], jnp.zeros((d, EXPERT_LANE0 - N_GROUPS), F32), w_router_expert[l],
             jnp.zeros((d, LANES - EXPERT_LANE0 - N_EXPERTS), F32)], axis=1)
        wr_hi = wr.astype(BF16)
        wr_split = jnp.concatenate([wr_hi, (wr - wr_hi.astype(F32)).astype(BF16)], axis=1)

        (qa, ka, va, qa4, ka4, va4, qa16, ka16, va16, qb, kb, vb) = _proj(
            x2d, b, attn_norm[l].reshape(1, d), w_l.astype(BF16), rope_tab, e3, bd, head_gains)
        blocks = lambda a: a.reshape(b, nb, BLOCK, a.shape[-1])
        class_blocks = lambda a: a.reshape(b, a.shape[1], a.shape[2] // BLOCK, BLOCK, MIX)
        oa = _dilated_attention(tuple(map(blocks, (qa, ka, va))),
                                tuple(map(class_blocks, (qa4, ka4, va4))),
                                tuple(map(class_blocks, (qa16, ka16, va16))))
        ob = _swa_attention(blocks(qb), blocks(kb), blocks(vb), sink)

        x1, h_tok, meta, meta_t, counts = _out_router(
            oa.reshape(t, MIX), ob.reshape(t, MIX), x2d, out_norm_a[l].reshape(1, MIX),
            gb_perm.reshape(1, MIX), wo_l, ffn_norm[l].reshape(1, d), wr_split, tri)

        cnt = counts[0, :N_EXPERTS].astype(jnp.int32)
        padded = ((cnt + GROUP_TILE - 1) // GROUP_TILE) * GROUP_TILE
        ends = jnp.cumsum(padded)
        offs = ends - padded
        eid = meta_t[:, META_E0:META_E1 + 1, :].astype(jnp.int32)
        rank = meta_t[:, META_R0:META_R1 + 1, :].astype(jnp.int32)
        is_expert = eid[..., None] == jnp.arange(N_EXPERTS, dtype=jnp.int32)
        slots = jnp.sum(jnp.where(is_expert, offs, 0), axis=-1) + rank
        slots = slots.reshape(t // ROW_TILE, 1, TOP_K * ROW_TILE)
        tile_start = jnp.arange(n_tiles, dtype=jnp.int32) * GROUP_TILE
        tile_expert = jnp.minimum(
            jnp.sum((tile_start[:, None] >= ends[None, :]).astype(jnp.int32), axis=1), N_EXPERTS - 1)
        n_valid = (ends[-1:] // GROUP_TILE).astype(jnp.int32)

        xs_buf = _dispatch(slots, h_tok, xs_buf, tok)
        ys = _experts(tile_expert, n_valid, xs_buf, w_gate_up, w_down, l)
        x2d = _combine(slots, x1, meta, ys)
    return x2d.reshape(b, s, d)
```

```python
import functools
import math

import jax
import jax.numpy as jnp
from jax import lax
from jax.experimental import pallas as pl
from jax.experimental.pallas import tpu as pltpu

F32 = jnp.float32
BF16 = jnp.bfloat16

HEAD_DIM = 64
N_HEADS = 8
MIX = N_HEADS * HEAD_DIM
N_KV_B = 2
KV_B = N_KV_B * HEAD_DIM
DILATIONS = (1, 4, 16)
BLOCK = 128
CHUNK = BLOCK * max(DILATIONS)
SWA_BACK = 127
SWA_STEP = 8 * BLOCK
ROPE_DIM = HEAD_DIM // 4
ROPE_HALF = ROPE_DIM // 2
ROPE_THETA = 500000.0
N_GROUPS = 4
EXPERTS_PER_GROUP = 8
N_EXPERTS = N_GROUPS * EXPERTS_PER_GROUP
TOP_K = 2
D_EXPERT = 512
EPS = 1e-6
LOG2E = math.log2(math.e)

LANES = 128
SUBLANES = 8
PAIR = 2 * HEAD_DIM
HALF_W = MIX // 2
EXPERT_LANE0 = 32
ROW_TILE = 512
ROUTE_BLOCK = 256
ROUTE_ROWS = 128
GROUP_TILE = 512
DMA_UNROLL = 8
VMEM_LIMIT = 48 * 1024 * 1024

META_E0, META_E1, META_G0, META_G1, META_R0, META_R1 = 0, 1, 2, 3, 4, 5


def _split2(a):
    hi = a.astype(BF16)
    lo = (a - hi.astype(F32)).astype(BF16)
    return hi, lo


def _split3(a):
    hi = a.astype(BF16)
    r = a - hi.astype(F32)
    mid = r.astype(BF16)
    lo = (r - mid.astype(F32)).astype(BF16)
    return hi, mid, lo


def _dot(a, b):
    return jnp.dot(a, b, preferred_element_type=F32)


def _rms_rows(a):
    return a * lax.rsqrt(jnp.mean(a * a, axis=-1, keepdims=True) + EPS)


def _lane_col(tile, lane, idx, fill, reduce):
    return reduce(jnp.where(lane == idx, tile, fill), axis=1, keepdims=True)


def _store_token_tiles(ref, val):
    rows, width = val.shape
    n = width // LANES
    for c in range(n):
        ref[pl.ds(c, rows, stride=n), :] = val[:, c * LANES:(c + 1) * LANES]


def _load_token_tiles(ref, rows, n):
    return jnp.concatenate([ref[pl.ds(c, rows, stride=n), :] for c in range(n)], axis=1)


def _proj_kernel(x_ref, g_ref, w_ref, rope_ref, e3_ref, bd_ref, hg_ref,
                 qa_ref, ka_ref, va_ref, qa4_ref, ka4_ref, va4_ref, qa16_ref, ka16_ref, va16_ref,
                 qb_ref, kb_ref, vb_ref, scr, scr4):
    h = _rms_rows(x_ref[...]) * g_ref[...]
    acc = _dot(h.astype(BF16), w_ref[...])
    tm = acc.shape[0]

    hi, mid, lo = _split3(rope_ref[...])
    tab = _dot(jnp.concatenate([hi, mid, lo], axis=1), e3_ref[...])
    lane = lax.broadcasted_iota(jnp.int32, (1, LANES), 1)
    not_rope = ((lane & (HEAD_DIM - 1)) >= ROPE_DIM).astype(F32)
    c = tab[:, :LANES] + not_rope
    s_lo = tab[:, LANES:2 * LANES]
    s_hi = tab[:, 2 * LANES:]
    bd = bd_ref[...]

    def norm_rope(a, gain):
        w = a.shape[1]
        a2_hi, a2_lo = _split2(a * a)
        ss = _dot(a2_hi, bd[:w, :w]) + _dot(a2_lo, bd[:w, :w])
        y = (a * lax.rsqrt(ss * (1.0 / HEAD_DIM) + EPS)) * gain
        outs = []
        for j in range(w // LANES):
            yj = y[:, j * LANES:(j + 1) * LANES]
            outs.append(yj * c + pltpu.roll(yj, ROPE_HALF, 1) * s_lo
                        + pltpu.roll(yj, LANES - ROPE_HALF, 1) * s_hi)
        return outs[0] if len(outs) == 1 else jnp.concatenate(outs, axis=1)

    def normed(col0, width, gain_row):
        outs = []
        for c0 in range(col0, col0 + width, 2 * LANES):
            wd = min(2 * LANES, col0 + width - c0)
            outs.append(norm_rope(acc[:, c0:c0 + wd], hg_ref[gain_row:gain_row + 1, :wd]))
        return outs[0] if len(outs) == 1 else jnp.concatenate(outs, axis=1)

    def emit_by_class(val, nat_ref, ref4, ref16):
        n_ct = MIX // LANES
        q4 = tm // 4
        nat_ref[...] = val.astype(BF16)
        for ct in range(n_ct):
            scr[ct] = val[:, ct * LANES:(ct + 1) * LANES]
        for r4 in range(4):
            rows = [scr[ct, pl.ds(r4, q4, stride=4), :] for ct in range(n_ct)]
            ref4[r4] = jnp.concatenate(rows, axis=1).astype(BF16)
            for ct in range(n_ct):
                scr4[ct, r4 * q4:(r4 + 1) * q4, :] = rows[ct]
        for r16 in range(16):
            first = (r16 % 4) * q4 + r16 // 4
            rows = [scr4[ct, pl.ds(first, tm // 16, stride=4), :] for ct in range(n_ct)]
            ref16[r16] = jnp.concatenate(rows, axis=1).astype(BF16)

    emit_by_class(normed(0, MIX, 0), qa_ref, qa4_ref, qa16_ref)
    emit_by_class(normed(MIX, MIX, 1), ka_ref, ka4_ref, ka16_ref)
    emit_by_class(acc[:, 2 * MIX:3 * MIX], va_ref, va4_ref, va16_ref)
    qb_ref[...] = normed(3 * MIX, MIX, 2).astype(BF16)
    kb_ref[...] = normed(4 * MIX, KV_B, 3).astype(BF16)
    vb_ref[...] = acc[:, 4 * MIX + KV_B:].astype(BF16)


def _proj(x2d, b, gain, w_bf16, rope_tab, e3, bd, head_gains):
    t, d = x2d.shape
    s = t // b
    pw = w_bf16.shape[1]
    tm = ROW_TILE
    nt = s // tm
    row = lambda bi, i: (bi * nt + i, 0)
    fixed = lambda bi, i: (0, 0)
    by_class = lambda bi, i: (bi, 0, i, 0)
    nat = lambda w: (jax.ShapeDtypeStruct((t, w), BF16), pl.BlockSpec((tm, w), row))
    grouped = lambda dil: (jax.ShapeDtypeStruct((b, dil, s // dil, MIX), BF16),
                           pl.BlockSpec((None, dil, tm // dil, MIX), by_class))
    outs = [nat(MIX)] * 3 + [grouped(4)] * 3 + [grouped(16)] * 3 + [nat(MIX), nat(KV_B), nat(KV_B)]
    return pl.pallas_call(
        _proj_kernel,
        grid=(b, nt),
        in_specs=[
            pl.BlockSpec((tm, d), row),
            pl.BlockSpec((1, d), fixed),
            pl.BlockSpec((d, pw), fixed),
            pl.BlockSpec((tm, LANES), row),
            pl.BlockSpec(e3.shape, fixed),
            pl.BlockSpec(bd.shape, fixed),
            pl.BlockSpec(head_gains.shape, fixed),
        ],
        out_specs=[o[1] for o in outs],
        out_shape=[o[0] for o in outs],
        scratch_shapes=[pltpu.VMEM((MIX // LANES, tm, LANES), F32)] * 2,
        compiler_params=pltpu.CompilerParams(
            dimension_semantics=("arbitrary", "arbitrary"), vmem_limit_bytes=VMEM_LIMIT),
        name="proj",
    )(x2d, gain, w_bf16, rope_tab, e3, bd, head_gains)


def _band(min_back, first_key):
    qi = lax.broadcasted_iota(jnp.int32, (BLOCK, 2 * BLOCK), 0)
    kj = lax.broadcasted_iota(jnp.int32, (BLOCK, 2 * BLOCK), 1)
    valid = (kj >= qi + min_back) & (kj <= qi + BLOCK)
    return valid if first_key is None else valid & (kj >= first_key)


def _head_scores(q_pair, kcat, sel, valid):
    qm = jnp.where(sel, q_pair, jnp.zeros_like(q_pair))
    s = lax.dot_general(qm, kcat, (((1,), (1,)), ((), ())), preferred_element_type=F32)
    return jnp.where(valid, s, -jnp.inf)


def _weighted_values(pr, v_pair, sel):
    return _dot(pr.astype(BF16), jnp.where(sel, v_pair, jnp.ones_like(v_pair)))


def _normalise_pair(res_even, res_odd, even, extra=None):
    num = jnp.where(even, res_even, res_odd)
    den = pltpu.roll(jnp.where(even, res_odd, res_even), HEAD_DIM, 1)
    return num / (den if extra is None else den + extra)


def _with_halo(prev, cur):
    return jnp.concatenate([prev, cur], axis=0)


def _swa_kernel(q_ref, kp_ref, k_ref, vp_ref, v_ref, sink_ref, o_ref):
    first_key = jnp.where(pl.program_id(1) > 0, 0, BLOCK)
    band = _band(BLOCK - SWA_BACK, None)
    band_first = _band(BLOCK - SWA_BACK, first_key)
    lane = lax.broadcasted_iota(jnp.int32, (1, LANES), 1)
    even = lane < HEAD_DIM
    for j in range(SWA_STEP // BLOCK):
        valid = band_first if j == 0 else band
        kcat = _with_halo(kp_ref[...] if j == 0 else k_ref[j - 1], k_ref[j])
        vcat = _with_halo(vp_ref[...] if j == 0 else v_ref[j - 1], v_ref[j])
        for p in range(N_HEADS // 2):
            cols = slice(p * PAIR, (p + 1) * PAIR)
            q_pair = q_ref[j, :, cols]
            res, sink_term = [], []
            for half in range(2):
                sink = sink_ref[2 * p + half]
                sel = even if half == 0 else jnp.logical_not(even)
                s = _head_scores(q_pair, kcat, sel, valid)
                m = jnp.maximum(jnp.max(s, axis=1, keepdims=True), sink)
                res.append(_weighted_values(jnp.exp2(s - m), vcat, sel))
                sink_term.append(jnp.exp2(sink - m))
            extra = jnp.where(even, sink_term[0], sink_term[1])
            o_ref[j, :, cols] = _normalise_pair(res[0], res[1], even, extra).astype(o_ref.dtype)


def _swa_attention(q, k, v, sink):
    b, nb = q.shape[:2]
    step = SWA_STEP // BLOCK
    cur = lambda w: pl.BlockSpec((None, step, BLOCK, w), lambda bi, c: (bi, c, 0, 0))
    prev = pl.BlockSpec((None, None, BLOCK, KV_B),
                        lambda bi, c: (bi, jnp.maximum(step * c - 1, 0), 0, 0))
    return pl.pallas_call(
        _swa_kernel,
        grid=(b, nb // step),
        in_specs=[cur(MIX), prev, cur(KV_B), prev, cur(KV_B), pl.BlockSpec(memory_space=pltpu.SMEM)],
        out_specs=cur(MIX),
        out_shape=jax.ShapeDtypeStruct(q.shape, BF16),
        compiler_params=pltpu.CompilerParams(
            dimension_semantics=("arbitrary", "arbitrary"), vmem_limit_bytes=VMEM_LIMIT),
        name="swa",
    )(q, k, k, v, v, sink)


def _dilated_kernel(qn, knp, kn, vnp, vn, q4, k4p, k4, v4p, v4, q16, k16p, k16, v16p, v16,
                    o_ref, m_st, r_st):
    n_pairs = HALF_W // PAIR
    first_key = jnp.where(pl.program_id(1) > 0, 0, BLOCK)
    band = _band(0, None)
    band_first = _band(0, first_key)
    lane = lax.broadcasted_iota(jnp.int32, (1, LANES), 1)
    even = lane < HEAD_DIM

    def tile(q, kcat, vcat, valid, row0, stride, init):
        rows = pl.ds(row0, BLOCK) if stride == 1 else pl.ds(row0, BLOCK, stride=stride)
        for p in range(n_pairs):
            cols = slice(p * PAIR, (p + 1) * PAIR)
            for half in range(2):
                h = 2 * p + half
                sel = even if half == 0 else jnp.logical_not(even)
                s = _head_scores(q[:, cols], kcat[:, cols], sel, valid)
                m_tile = jnp.max(s, axis=1, keepdims=True)
                if init:
                    m_new = jnp.broadcast_to(m_tile, (BLOCK, LANES))
                    res = _weighted_values(jnp.exp2(s - m_tile), vcat[:, cols], sel)
                else:
                    m_old = m_st[h, rows, :]
                    m_new = jnp.maximum(m_old, m_tile)
                    pr = jnp.exp2(s - jnp.concatenate([m_new, m_new], axis=1))
                    res = (_weighted_values(pr, vcat[:, cols], sel)
                           + jnp.exp2(m_old - m_new) * r_st[h, rows, :])
                m_st[h, rows, :] = m_new
                r_st[h, rows, :] = res

    def d16(r, carry):
        tile(q16[r], _with_halo(k16p[r], k16[r]), _with_halo(v16p[r], v16[r]), band_first, r, 16, True)
        return carry

    lax.fori_loop(0, 16, d16, 0, unroll=4)

    tile(qn[0], _with_halo(knp[...], kn[0]), _with_halo(vnp[...], vn[0]), band_first, 0, 1, False)

    def d1(j, carry):
        tile(qn[j], _with_halo(kn[j - 1], kn[j]), _with_halo(vn[j - 1], vn[j]),
             band, pl.multiple_of(j * BLOCK, BLOCK), 1, False)
        return carry

    lax.fori_loop(1, CHUNK // BLOCK, d1, 0, unroll=5)

    def d4(r, carry):
        tile(q4[r, 0], _with_halo(k4p[r], k4[r, 0]), _with_halo(v4p[r], v4[r, 0]), band_first, r, 4, False)
        for j in range(1, CHUNK // (4 * BLOCK)):
            tile(q4[r, j], _with_halo(k4[r, j - 1], k4[r, j]), _with_halo(v4[r, j - 1], v4[r, j]),
                 band, r + 4 * j * BLOCK, 4, False)
        return carry

    lax.fori_loop(0, 4, d4, 0)

    def finish(j, carry):
        rows = pl.ds(pl.multiple_of(j * BLOCK, BLOCK), BLOCK)
        for p in range(n_pairs):
            out = _normalise_pair(r_st[2 * p, rows, :], r_st[2 * p + 1, rows, :], even)
            o_ref[j, :, p * PAIR:(p + 1) * PAIR] = out.astype(o_ref.dtype)
        return carry

    lax.fori_loop(0, CHUNK // BLOCK, finish, 0, unroll=2)


def _dilated_attention(nat, by4, by16):
    b, nb = nat[0].shape[:2]
    per_chunk = CHUNK // BLOCK
    n_chunks = nb // per_chunk

    def specs(dil):
        step = per_chunk // dil
        if dil == 1:
            cur = pl.BlockSpec((None, step, BLOCK, HALF_W), lambda bi, c, hf: (bi, c, 0, hf))
            prev = pl.BlockSpec((None, None, BLOCK, HALF_W),
                                lambda bi, c, hf: (bi, jnp.maximum(step * c - 1, 0), 0, hf))
        else:
            blocks = None if step == 1 else step
            cur = pl.BlockSpec((None, dil, blocks, BLOCK, HALF_W), lambda bi, c, hf: (bi, 0, c, 0, hf))
            prev = pl.BlockSpec((None, dil, None, BLOCK, HALF_W),
                                lambda bi, c, hf: (bi, 0, jnp.maximum(step * c - 1, 0), 0, hf))
        return cur, prev

    args, in_specs = [], []
    for (q, k, v), dil in ((nat, 1), (by4, 4), (by16, 16)):
        cur, prev = specs(dil)
        args += [q, k, k, v, v]
        in_specs += [cur, prev, cur, prev, cur]
    return pl.pallas_call(
        _dilated_kernel,
        grid=(b, n_chunks, MIX // HALF_W),
        in_specs=in_specs,
        out_specs=specs(1)[0],
        out_shape=jax.ShapeDtypeStruct(nat[0].shape, BF16),
        scratch_shapes=[pltpu.VMEM((HALF_W // HEAD_DIM, CHUNK, LANES), F32),
                        pltpu.VMEM((HALF_W // HEAD_DIM, CHUNK, LANES), F32)],
        compiler_params=pltpu.CompilerParams(
            dimension_semantics=("arbitrary", "arbitrary", "arbitrary"),
            vmem_limit_bytes=VMEM_LIMIT),
        name="dilated",
    )(*args)


def _pick_experts(logits):
    n = logits.shape[0]
    lane = lax.broadcasted_iota(jnp.int32, (n, LANES), 1)
    far = jnp.int32(2 * LANES)
    neg = -jnp.inf

    def first_argmax(vals):
        mx = jnp.max(vals, axis=1, keepdims=True)
        idx = jnp.min(jnp.where(vals == mx, lane, far), axis=1, keepdims=True)
        return mx, idx

    gl = jnp.where(lane < N_GROUPS, logits, neg)
    gmax, gidx = first_argmax(gl)
    g_top = 1.0 / jnp.sum(jnp.exp(gl - gmax), axis=1, keepdims=True)

    lo_lane = EXPERT_LANE0 + EXPERTS_PER_GROUP * gidx
    el = jnp.where((lane >= lo_lane) & (lane < lo_lane + EXPERTS_PER_GROUP), logits, neg)
    m1, i1 = first_argmax(el)
    m2, i2 = first_argmax(jnp.where(lane == i1, neg, el))
    t2 = jnp.exp(m2 - m1)
    return i1 - EXPERT_LANE0, i2 - EXPERT_LANE0, g_top / (1.0 + t2), g_top * t2 / (1.0 + t2)


def _out_router_kernel(oa_ref, ob_ref, x_ref, ga_ref, gb_ref, wo_ref, gf_ref, wr_ref, tri_ref,
                       x1_ref, h_ref, meta_ref, meta_t_ref, cnt_ref, carry_ref):
    @pl.when(pl.program_id(0) == 0)
    def _():
        carry_ref[...] = jnp.zeros_like(carry_ref)

    tm, d = x_ref.shape
    tok = d // LANES
    logits = []
    for r0 in range(0, tm, ROUTE_BLOCK):
        rows = slice(r0, r0 + ROUTE_BLOCK)
        na = _rms_rows(oa_ref[rows, :].astype(F32)) * ga_ref[...]
        nb = _rms_rows(ob_ref[rows, :].astype(F32)) * gb_ref[...]
        mixed = jnp.concatenate([na, nb], axis=1).astype(BF16)
        x1 = x_ref[rows, :] + _dot(mixed, wo_ref[...])
        x1_ref[rows, :] = x1
        h = _rms_rows(x1) * gf_ref[...]
        _store_token_tiles(h_ref.at[r0 * tok:(r0 + ROUTE_BLOCK) * tok], h)

        h_hi, h_lo = _split2(h)
        parts = _dot(h_hi, wr_ref[...]) + _dot(h_lo, wr_ref[...])
        logits.append(parts[:, :LANES] + parts[:, LANES:])

    e0, e1, gate0, gate1 = _pick_experts(jnp.concatenate(logits, axis=0))

    lane = lax.broadcasted_iota(jnp.int32, (tm, LANES), 1)
    oh0 = lane == e0
    oh1 = lane == e1
    cum = _dot(tri_ref[...], jnp.concatenate([oh0.astype(BF16), oh1.astype(BF16)], axis=1))
    tot0 = jnp.sum(oh0.astype(F32), axis=0, keepdims=True)
    tot1 = jnp.sum(oh1.astype(F32), axis=0, keepdims=True)
    base = carry_ref[...]
    rank0 = jnp.sum(jnp.where(oh0, cum[:, :LANES] + base, 0.0), axis=1, keepdims=True)
    rank1 = jnp.sum(jnp.where(oh1, cum[:, LANES:] + (base + tot0), 0.0), axis=1, keepdims=True)
    base = base + tot0 + tot1
    carry_ref[...] = base
    cnt_ref[...] = base

    meta = jnp.zeros((tm, LANES), F32)
    for ln, val in ((META_E0, e0.astype(F32)), (META_E1, e1.astype(F32)), (META_G0, gate0),
                    (META_G1, gate1), (META_R0, rank0), (META_R1, rank1)):
        meta = jnp.where(lane == ln, val, meta)
    meta_ref[...] = meta
    meta_t_ref[...] = meta.T[:SUBLANES, :]


def _out_router(oa, ob, x2d, ga, gb, wo_bf16, gf, wr, tri):
    t, d = x2d.shape
    tm = ROW_TILE
    tok = d // LANES
    row = lambda i: (i, 0)
    fixed = lambda i: (0, 0)
    return pl.pallas_call(
        _out_router_kernel,
        grid=(t // tm,),
        in_specs=[
            pl.BlockSpec((tm, MIX), row),
            pl.BlockSpec((tm, MIX), row),
            pl.BlockSpec((tm, d), row),
            pl.BlockSpec((1, MIX), fixed),
            pl.BlockSpec((1, MIX), fixed),
            pl.BlockSpec((2 * MIX, d), fixed),
            pl.BlockSpec((1, d), fixed),
            pl.BlockSpec((d, 2 * LANES), fixed),
            pl.BlockSpec((tm, tm), fixed),
        ],
        out_specs=[
            pl.BlockSpec((tm, d), row),
            pl.BlockSpec((tm * tok, LANES), row),
            pl.BlockSpec((tm, LANES), row),
            pl.BlockSpec((None, SUBLANES, tm), lambda i: (i, 0, 0)),
            pl.BlockSpec((1, LANES), fixed),
        ],
        out_shape=[
            jax.ShapeDtypeStruct((t, d), F32),
            jax.ShapeDtypeStruct((t * tok, LANES), F32),
            jax.ShapeDtypeStruct((t, LANES), F32),
            jax.ShapeDtypeStruct((t // tm, SUBLANES, tm), F32),
            jax.ShapeDtypeStruct((1, LANES), F32),
        ],
        scratch_shapes=[pltpu.VMEM((1, LANES), F32)],
        compiler_params=pltpu.CompilerParams(
            dimension_semantics=("arbitrary",), vmem_limit_bytes=VMEM_LIMIT),
        name="out_router",
    )(oa, ob, x2d, ga, gb, wo_bf16, gf, wr, tri)


def _token_copy(src, i, dst, j, tok, sem):
    return pltpu.make_async_copy(src.at[pl.ds(pl.multiple_of(i * tok, tok), tok)],
                                 dst.at[pl.ds(pl.multiple_of(j * tok, tok), tok)], sem)


def _dispatch_kernel(slot_ref, h_ref, xs_in_ref, xs_ref, sem, *, tok):
    del xs_in_ref
    tm = h_ref.shape[0] // tok

    def start(blk, c):
        for u in range(DMA_UNROLL):
            i = blk * DMA_UNROLL + u
            for k in range(TOP_K):
                _token_copy(h_ref, i, xs_ref, slot_ref[0, k * tm + i], tok, sem).start(priority=k)
        return c

    lax.fori_loop(0, tm // DMA_UNROLL, start, 0)
    for k in range(TOP_K):
        pltpu.make_async_copy(h_ref, xs_ref.at[pl.ds(0, tm * tok)], sem).wait()


def _dispatch(slots, h_tok, xs_buf, tok):
    tm = ROW_TILE
    n_tiles = h_tok.shape[0] // (tm * tok)
    return pl.pallas_call(
        functools.partial(_dispatch_kernel, tok=tok),
        grid=(n_tiles,),
        in_specs=[
            pl.BlockSpec((None, 1, TOP_K * tm), lambda i: (i, 0, 0), memory_space=pltpu.SMEM),
            pl.BlockSpec((tm * tok, LANES), lambda i: (i, 0)),
            pl.BlockSpec(memory_space=pl.ANY),
        ],
        out_specs=pl.BlockSpec(memory_space=pl.ANY),
        out_shape=jax.ShapeDtypeStruct(xs_buf.shape, xs_buf.dtype),
        scratch_shapes=[pltpu.SemaphoreType.DMA(())],
        input_output_aliases={2: 0},
        compiler_params=pltpu.CompilerParams(
            dimension_semantics=("arbitrary",), vmem_limit_bytes=VMEM_LIMIT),
        name="dispatch",
    )(slots, h_tok, xs_buf)


def _expert_kernel(te_ref, nv_ref, xs_ref, wgu_ref, wdn_ref, ys_ref, wgu_bf, wdn_bf):
    i = pl.program_id(0)
    d = wgu_ref.shape[0]
    tok = d // LANES
    tg = xs_ref.shape[0] // tok

    @pl.when((i == 0) | (te_ref[i] != te_ref[jnp.maximum(i - 1, 0)]))
    def _():
        wgu_bf[...] = wgu_ref[...].astype(BF16)
        wdn_bf[...] = wdn_ref[...].astype(BF16)

    @pl.when(i < nv_ref[0])
    def _():
        xs = _load_token_tiles(xs_ref, tg, tok)
        gu = _dot(xs.astype(BF16), wgu_bf[...])
        g = gu[:, :D_EXPERT]
        a = (g * (1.0 / (1.0 + jnp.exp(-g)))) * gu[:, D_EXPERT:]
        _store_token_tiles(ys_ref, _dot(a.astype(BF16), wdn_bf[...]))

    @pl.when(i >= nv_ref[0])
    def _():
        ys_ref[...] = jnp.zeros_like(ys_ref)


def _experts(tile_expert, n_valid, xs, wgu, wdn, layer):
    d = wgu.shape[2]
    tok = d // LANES
    tg = GROUP_TILE
    grid_spec = pltpu.PrefetchScalarGridSpec(
        num_scalar_prefetch=2,
        grid=(xs.shape[0] // (tg * tok),),
        in_specs=[
            pl.BlockSpec((tg * tok, LANES), lambda i, te, nv: (i, 0)),
            pl.BlockSpec((None, None, d, 2 * D_EXPERT), lambda i, te, nv: (layer, te[i], 0, 0)),
            pl.BlockSpec((None, None, D_EXPERT, d), lambda i, te, nv: (layer, te[i], 0, 0)),
        ],
        out_specs=pl.BlockSpec((tg * tok, LANES), lambda i, te, nv: (i, 0)),
        scratch_shapes=[pltpu.VMEM((d, 2 * D_EXPERT), BF16), pltpu.VMEM((D_EXPERT, d), BF16)],
    )
    return pl.pallas_call(
        _expert_kernel,
        grid_spec=grid_spec,
        out_shape=jax.ShapeDtypeStruct(xs.shape, F32),
        compiler_params=pltpu.CompilerParams(
            dimension_semantics=("arbitrary",), vmem_limit_bytes=VMEM_LIMIT),
        name="experts",
    )(tile_expert, n_valid, xs, wgu, wdn)


def _combine_kernel(slot_ref, slot_next_ref, x_ref, meta_ref, ys_ref, out_ref, buf, sems):
    step = pl.program_id(0)
    tm, d = x_ref.shape
    tok = d // LANES

    def gather(slots, half):
        def start(blk, c):
            for u in range(DMA_UNROLL):
                i = blk * DMA_UNROLL + u
                for k in range(TOP_K):
                    _token_copy(ys_ref, slots[0, k * tm + i], buf.at[half, k], i, tok,
                                sems.at[half]).start(priority=k)
            return c

        lax.fori_loop(0, tm // DMA_UNROLL, start, 0)

    cur = step % 2

    @pl.when(step == 0)
    def _():
        gather(slot_ref, 0)

    @pl.when(step + 1 < pl.num_programs(0))
    def _():
        gather(slot_next_ref, 1 - cur)

    for k in range(TOP_K):
        pltpu.make_async_copy(ys_ref.at[pl.ds(0, tm * tok)], buf.at[cur, k], sems.at[cur]).wait()
    meta = meta_ref[...]
    lane = lax.broadcasted_iota(jnp.int32, (1, LANES), 1)
    g0 = _lane_col(meta, lane, META_G0, 0.0, jnp.sum)
    g1 = _lane_col(meta, lane, META_G1, 0.0, jnp.sum)
    y = (g0 * _load_token_tiles(buf.at[cur, 0], tm, tok)
         + g1 * _load_token_tiles(buf.at[cur, 1], tm, tok))
    out_ref[...] = x_ref[...] + y


def _combine(slots, x1, meta, ys):
    t, d = x1.shape
    tm = ROW_TILE
    tok = d // LANES
    n_tiles = t // tm
    return pl.pallas_call(
        _combine_kernel,
        grid=(n_tiles,),
        in_specs=[
            pl.BlockSpec((None, 1, TOP_K * tm), lambda i: (i, 0, 0), memory_space=pltpu.SMEM),
            pl.BlockSpec((None, 1, TOP_K * tm), lambda i: (jnp.minimum(i + 1, n_tiles - 1), 0, 0),
                         memory_space=pltpu.SMEM),
            pl.BlockSpec((tm, d), lambda i: (i, 0)),
            pl.BlockSpec((tm, LANES), lambda i: (i, 0)),
            pl.BlockSpec(memory_space=pl.ANY),
        ],
        out_specs=pl.BlockSpec((tm, d), lambda i: (i, 0)),
        out_shape=jax.ShapeDtypeStruct((t, d), F32),
        scratch_shapes=[pltpu.VMEM((2, TOP_K, tm * tok, LANES), F32), pltpu.SemaphoreType.DMA((2,))],
        compiler_params=pltpu.CompilerParams(
            dimension_semantics=("arbitrary",), vmem_limit_bytes=VMEM_LIMIT),
        name="combine",
    )(slots, slots, x1, meta, ys)


def _rope_expand_matrix():
    src = jnp.arange(LANES)[:, None]
    dst = jnp.arange(LANES)[None, :]
    in_head = dst % HEAD_DIM
    cos_m = (src < ROPE_HALF) & (in_head < ROPE_DIM) & (in_head % ROPE_HALF == src)
    is_sin = (src >= ROPE_HALF) & (src < ROPE_DIM)
    sin_lo = is_sin & (in_head >= ROPE_HALF) & (in_head < ROPE_DIM) & (in_head - ROPE_HALF == src - ROPE_HALF)
    sin_hi = is_sin & (in_head < ROPE_HALF) & (in_head == src - ROPE_HALF)
    e = jnp.concatenate([cos_m.astype(F32), sin_lo.astype(F32), -sin_hi.astype(F32)], axis=1)
    return jnp.concatenate([e, e, e], axis=0).astype(BF16)


def _block_diag_ones():
    i = jnp.arange(2 * LANES)
    return (i[:, None] // HEAD_DIM == i[None, :] // HEAD_DIM).astype(BF16)


def _tile4(g):
    return jnp.tile(g.astype(F32), 2 * LANES // HEAD_DIM)


_B_HEAD_ORDER = tuple(h for p in range(N_HEADS // 2) for h in (p, p + N_HEADS // 2))


def _b_cols():
    order = jnp.asarray(_B_HEAD_ORDER)
    return (order[:, None] * HEAD_DIM + jnp.arange(HEAD_DIM)[None, :]).reshape(-1)


def kernel(x, positions, attn_norm, w_in, q_norm_a, k_norm_a, q_norm_b, k_norm_b, sinks_b,
           out_norm_a, out_norm_b, w_out, ffn_norm, w_router_group, w_router_expert,
           w_gate_up, w_down):
    b, s, d = x.shape
    t = b * s
    depth = w_in.shape[0]
    tok = d // LANES
    assert s % CHUNK == 0 and s % ROW_TILE == 0 and s % SWA_STEP == 0
    assert ROW_TILE % (2 * SUBLANES * max(DILATIONS)) == 0 and ROW_TILE % ROUTE_BLOCK == 0 and ROUTE_BLOCK % ROUTE_ROWS == 0
    assert d == 2 * MIX and tok == SUBLANES

    inv_freq = ROPE_THETA ** (-jnp.arange(0, ROPE_DIM, 2, dtype=F32) / ROPE_DIM)
    ang = positions.astype(F32)[None, :, :] * inv_freq[:, None, None]
    cos_sin = jnp.moveaxis(jnp.concatenate([jnp.cos(ang), jnp.sin(ang)], axis=0), 0, -1)
    rope_tab = jnp.pad(cos_sin, ((0, 0), (0, 0), (0, LANES - ROPE_DIM))).reshape(t, LANES)
    e3 = _rope_expand_matrix()
    bd = _block_diag_ones()
    row_i = jnp.arange(ROW_TILE)
    tri = (row_i[None, :] < row_i[:, None]).astype(BF16)
    bcols = _b_cols()
    q_scale = HEAD_DIM ** -0.5 * LOG2E
    n_sorted = t * TOP_K + N_EXPERTS * GROUP_TILE
    n_tiles = n_sorted // GROUP_TILE
    xs_buf = jnp.zeros((n_sorted * tok, LANES), F32)
    nb = s // BLOCK

    x2d = x.reshape(t, d)
    for l in range(depth):
        qb0 = 3 * MIX
        w_l = w_in[l]
        w_l = jnp.concatenate([w_l[:, :qb0], w_l[:, qb0:qb0 + MIX][:, bcols], w_l[:, qb0 + MIX:]], axis=1)
        head_gains = jnp.stack([_tile4(q_norm_a[l]) * q_scale, _tile4(k_norm_a[l]),
                                _tile4(q_norm_b[l]) * q_scale, _tile4(k_norm_b[l])])
        sink = sinks_b[l][jnp.asarray(_B_HEAD_ORDER)].astype(F32) * LOG2E
        gb_perm = out_norm_b[l][bcols]
        wo_l = jnp.concatenate([w_out[l][:MIX], w_out[l][MIX:][bcols]], axis=0).astype(BF16)
        wr = jnp.concatenate(
            [w_router_group[l], jnp.zeros((d, EXPERT_LANE0 - N_GROUPS), F32), w_router_expert[l],
             jnp.zeros((d, LANES - EXPERT_LANE0 - N_EXPERTS), F32)], axis=1)
        wr_hi = wr.astype(BF16)
        wr_split = jnp.concatenate([wr_hi, (wr - wr_hi.astype(F32)).astype(BF16)], axis=1)

        (qa, ka, va, qa4, ka4, va4, qa16, ka16, va16, qb, kb, vb) = _proj(
            x2d, b, attn_norm[l].reshape(1, d), w_l.astype(BF16), rope_tab, e3, bd, head_gains)
        blocks = lambda a: a.reshape(b, nb, BLOCK, a.shape[-1])
        class_blocks = lambda a: a.reshape(b, a.shape[1], a.shape[2] // BLOCK, BLOCK, MIX)
        oa = _dilated_attention(tuple(map(blocks, (qa, ka, va))),
                                tuple(map(class_blocks, (qa4, ka4, va4))),
                                tuple(map(class_blocks, (qa16, ka16, va16))))
        ob = _swa_attention(blocks(qb), blocks(kb), blocks(vb), sink)

        x1, h_tok, meta, meta_t, counts = _out_router(
            oa.reshape(t, MIX), ob.reshape(t, MIX), x2d, out_norm_a[l].reshape(1, MIX),
            gb_perm.reshape(1, MIX), wo_l, ffn_norm[l].reshape(1, d), wr_split, tri)

        cnt = counts[0, :N_EXPERTS].astype(jnp.int32)
        padded = ((cnt + GROUP_TILE - 1) // GROUP_TILE) * GROUP_TILE
        ends = jnp.cumsum(padded)
        offs = ends - padded
        eid = meta_t[:, META_E0:META_E1 + 1, :].astype(jnp.int32)
        rank = meta_t[:, META_R0:META_R1 + 1, :].astype(jnp.int32)
        is_expert = eid[..., None] == jnp.arange(N_EXPERTS, dtype=jnp.int32)
        slots = jnp.sum(jnp.where(is_expert, offs, 0), axis=-1) + rank
        slots = slots.reshape(t // ROW_TILE, 1, TOP_K * ROW_TILE)
        tile_start = jnp.arange(n_tiles, dtype=jnp.int32) * GROUP_TILE
        tile_expert = jnp.minimum(
            jnp.sum((tile_start[:, None] >= ends[None, :]).astype(jnp.int32), axis=1), N_EXPERTS - 1)
        n_valid = (ends[-1:] // GROUP_TILE).astype(jnp.int32)

        xs_buf = _dispatch(slots, h_tok, xs_buf, tok)
        ys = _experts(tile_expert, n_valid, xs_buf, w_gate_up, w_down, l)
        x2d = _combine(slots, x1, meta, ys)
    return x2d.reshape(b, s, d)
```

```python
import functools
import math

import jax
import jax.numpy as jnp
from jax import lax
from jax.experimental import pallas as pl
from jax.experimental.pallas import tpu as pltpu

F32 = jnp.float32
BF16 = jnp.bfloat16

HEAD_DIM = 64
N_HEADS = 8
MIX = N_HEADS * HEAD_DIM
N_KV_B = 2
KV_B = N_KV_B * HEAD_DIM
DILATIONS = (1, 4, 16)
BLOCK = 128
CHUNK = BLOCK * max(DILATIONS)
SWA_BACK = 127
SWA_STEP = 8 * BLOCK
ROPE_DIM = HEAD_DIM // 4
ROPE_HALF = ROPE_DIM // 2
ROPE_THETA = 500000.0
N_GROUPS = 4
EXPERTS_PER_GROUP = 8
N_EXPERTS = N_GROUPS * EXPERTS_PER_GROUP
TOP_K = 2
D_EXPERT = 512
EPS = 1e-6
LOG2E = math.log2(math.e)

LANES = 128
SUBLANES = 8
PAIR = 2 * HEAD_DIM
HALF_W = MIX // 2
EXPERT_LANE0 = 32
ROW_TILE = 512
ROUTE_BLOCK = 256
ROUTE_ROWS = 128
GROUP_TILE = 512
DMA_UNROLL = 8
VMEM_LIMIT = 48 * 1024 * 1024

META_E0, META_E1, META_G0, META_G1, META_R0, META_R1 = 0, 1, 2, 3, 4, 5


def _split2(a):
    hi = a.astype(BF16)
    lo = (a - hi.astype(F32)).astype(BF16)
    return hi, lo


def _split3(a):
    hi = a.astype(BF16)
    r = a - hi.astype(F32)
    mid = r.astype(BF16)
    lo = (r - mid.astype(F32)).astype(BF16)
    return hi, mid, lo


def _dot(a, b):
    return jnp.dot(a, b, preferred_element_type=F32)


def _rms_rows(a):
    return a * lax.rsqrt(jnp.mean(a * a, axis=-1, keepdims=True) + EPS)


def _lane_col(tile, lane, idx, fill, reduce):
    return reduce(jnp.where(lane == idx, tile, fill), axis=1, keepdims=True)


def _pack_bf16_pairs(val):
    w = val.shape[1] // 2
    bits = lax.bitcast_convert_type(val, jnp.uint32) + jnp.uint32(0x8000)
    return (bits[:, :w] >> 16) | (bits[:, w:] & jnp.uint32(0xFFFF0000))


def _unpack_bf16_pairs(words):
    lo = lax.bitcast_convert_type(words << 16, F32)
    hi = lax.bitcast_convert_type(words & jnp.uint32(0xFFFF0000), F32)
    return jnp.concatenate([lo, hi], axis=1)


def _store_token_tiles(ref, val):
    rows, width = val.shape
    n = width // LANES
    for c in range(n):
        ref[pl.ds(c, rows, stride=n), :] = val[:, c * LANES:(c + 1) * LANES]


def _load_token_tiles(ref, rows, n):
    return jnp.concatenate([ref[pl.ds(c, rows, stride=n), :] for c in range(n)], axis=1)


def _proj_kernel(x_ref, g_ref, w_ref, rope_ref, e3_ref, bd_ref, hg_ref,
                 qa_ref, ka_ref, va_ref, qa4_ref, ka4_ref, va4_ref, qa16_ref, ka16_ref, va16_ref,
                 qb_ref, kb_ref, vb_ref, scr, scr4):
    h = _rms_rows(x_ref[...]) * g_ref[...]
    acc = _dot(h.astype(BF16), w_ref[...])
    tm = acc.shape[0]

    hi, mid, lo = _split3(rope_ref[...])
    tab = _dot(jnp.concatenate([hi, mid, lo], axis=1), e3_ref[...])
    lane = lax.broadcasted_iota(jnp.int32, (1, LANES), 1)
    not_rope = ((lane & (HEAD_DIM - 1)) >= ROPE_DIM).astype(F32)
    c = tab[:, :LANES] + not_rope
    s_lo = tab[:, LANES:2 * LANES]
    s_hi = tab[:, 2 * LANES:]
    bd = bd_ref[...]

    def norm_rope(a, gain):
        w = a.shape[1]
        a2_hi, a2_lo = _split2(a * a)
        ss = _dot(a2_hi, bd[:w, :w]) + _dot(a2_lo, bd[:w, :w])
        y = (a * lax.rsqrt(ss * (1.0 / HEAD_DIM) + EPS)) * gain
        outs = []
        for j in range(w // LANES):
            yj = y[:, j * LANES:(j + 1) * LANES]
            outs.append(yj * c + pltpu.roll(yj, ROPE_HALF, 1) * s_lo
                        + pltpu.roll(yj, LANES - ROPE_HALF, 1) * s_hi)
        return outs[0] if len(outs) == 1 else jnp.concatenate(outs, axis=1)

    def normed(col0, width, gain_row):
        outs = []
        for c0 in range(col0, col0 + width, 2 * LANES):
            wd = min(2 * LANES, col0 + width - c0)
            outs.append(norm_rope(acc[:, c0:c0 + wd], hg_ref[gain_row:gain_row + 1, :wd]))
        return outs[0] if len(outs) == 1 else jnp.concatenate(outs, axis=1)

    def emit_by_class(val, nat_ref, ref4, ref16):
        n_ct = MIX // LANES
        q4 = tm // 4
        nat_ref[...] = val.astype(BF16)
        for ct in range(n_ct):
            scr[ct] = val[:, ct * LANES:(ct + 1) * LANES]
        for r4 in range(4):
            rows = [scr[ct, pl.ds(r4, q4, stride=4), :] for ct in range(n_ct)]
            ref4[r4] = jnp.concatenate(rows, axis=1).astype(BF16)
            for ct in range(n_ct):
                scr4[ct, r4 * q4:(r4 + 1) * q4, :] = rows[ct]
        for r16 in range(16):
            first = (r16 % 4) * q4 + r16 // 4
            rows = [scr4[ct, pl.ds(first, tm // 16, stride=4), :] for ct in range(n_ct)]
            ref16[r16] = jnp.concatenate(rows, axis=1).astype(BF16)

    emit_by_class(normed(0, MIX, 0), qa_ref, qa4_ref, qa16_ref)
    emit_by_class(normed(MIX, MIX, 1), ka_ref, ka4_ref, ka16_ref)
    emit_by_class(acc[:, 2 * MIX:3 * MIX], va_ref, va4_ref, va16_ref)
    qb_ref[...] = normed(3 * MIX, MIX, 2).astype(BF16)
    kb_ref[...] = normed(4 * MIX, KV_B, 3).astype(BF16)
    vb_ref[...] = acc[:, 4 * MIX + KV_B:].astype(BF16)


def _proj(x2d, b, gain, w_bf16, rope_tab, e3, bd, head_gains):
    t, d = x2d.shape
    s = t // b
    pw = w_bf16.shape[1]
    tm = ROW_TILE
    nt = s // tm
    row = lambda bi, i: (bi * nt + i, 0)
    fixed = lambda bi, i: (0, 0)
    by_class = lambda bi, i: (bi, 0, i, 0)
    nat = lambda w: (jax.ShapeDtypeStruct((t, w), BF16), pl.BlockSpec((tm, w), row))
    grouped = lambda dil: (jax.ShapeDtypeStruct((b, dil, s // dil, MIX), BF16),
                           pl.BlockSpec((None, dil, tm // dil, MIX), by_class))
    outs = [nat(MIX)] * 3 + [grouped(4)] * 3 + [grouped(16)] * 3 + [nat(MIX), nat(KV_B), nat(KV_B)]
    return pl.pallas_call(
        _proj_kernel,
        grid=(b, nt),
        in_specs=[
            pl.BlockSpec((tm, d), row),
            pl.BlockSpec((1, d), fixed),
            pl.BlockSpec((d, pw), fixed),
            pl.BlockSpec((tm, LANES), row),
            pl.BlockSpec(e3.shape, fixed),
            pl.BlockSpec(bd.shape, fixed),
            pl.BlockSpec(head_gains.shape, fixed),
        ],
        out_specs=[o[1] for o in outs],
        out_shape=[o[0] for o in outs],
        scratch_shapes=[pltpu.VMEM((MIX // LANES, tm, LANES), F32)] * 2,
        compiler_params=pltpu.CompilerParams(
            dimension_semantics=("arbitrary", "arbitrary"), vmem_limit_bytes=VMEM_LIMIT),
        name="proj",
    )(x2d, gain, w_bf16, rope_tab, e3, bd, head_gains)


def _band(min_back, first_key):
    qi = lax.broadcasted_iota(jnp.int32, (BLOCK, 2 * BLOCK), 0)
    kj = lax.broadcasted_iota(jnp.int32, (BLOCK, 2 * BLOCK), 1)
    valid = (kj >= qi + min_back) & (kj <= qi + BLOCK)
    return valid if first_key is None else valid & (kj >= first_key)


def _head_scores(q_pair, kcat, sel, valid):
    qm = jnp.where(sel, q_pair, jnp.zeros_like(q_pair))
    s = lax.dot_general(qm, kcat, (((1,), (1,)), ((), ())), preferred_element_type=F32)
    return jnp.where(valid, s, -jnp.inf)


def _weighted_values(pr, v_pair, sel):
    return _dot(pr.astype(BF16), jnp.where(sel, v_pair, jnp.ones_like(v_pair)))


def _normalise_pair(res_even, res_odd, even, extra=None):
    num = jnp.where(even, res_even, res_odd)
    den = pltpu.roll(jnp.where(even, res_odd, res_even), HEAD_DIM, 1)
    return num / (den if extra is None else den + extra)


def _with_halo(prev, cur):
    return jnp.concatenate([prev, cur], axis=0)


def _swa_kernel(q_ref, kp_ref, k_ref, vp_ref, v_ref, sink_ref, o_ref):
    first_key = jnp.where(pl.program_id(1) > 0, 0, BLOCK)
    band = _band(BLOCK - SWA_BACK, None)
    band_first = _band(BLOCK - SWA_BACK, first_key)
    lane = lax.broadcasted_iota(jnp.int32, (1, LANES), 1)
    even = lane < HEAD_DIM
    for j in range(SWA_STEP // BLOCK):
        valid = band_first if j == 0 else band
        kcat = _with_halo(kp_ref[...] if j == 0 else k_ref[j - 1], k_ref[j])
        vcat = _with_halo(vp_ref[...] if j == 0 else v_ref[j - 1], v_ref[j])
        for p in range(N_HEADS // 2):
            cols = slice(p * PAIR, (p + 1) * PAIR)
            q_pair = q_ref[j, :, cols]
            res, sink_term = [], []
            for half in range(2):
                sink = sink_ref[2 * p + half]
                sel = even if half == 0 else jnp.logical_not(even)
                s = _head_scores(q_pair, kcat, sel, valid)
                m = jnp.maximum(jnp.max(s, axis=1, keepdims=True), sink)
                res.append(_weighted_values(jnp.exp2(s - m), vcat, sel))
                sink_term.append(jnp.exp2(sink - m))
            extra = jnp.where(even, sink_term[0], sink_term[1])
            o_ref[j, :, cols] = _normalise_pair(res[0], res[1], even, extra).astype(o_ref.dtype)


def _swa_attention(q, k, v, sink):
    b, nb = q.shape[:2]
    step = SWA_STEP // BLOCK
    cur = lambda w: pl.BlockSpec((None, step, BLOCK, w), lambda bi, c: (bi, c, 0, 0))
    prev = pl.BlockSpec((None, None, BLOCK, KV_B),
                        lambda bi, c: (bi, jnp.maximum(step * c - 1, 0), 0, 0))
    return pl.pallas_call(
        _swa_kernel,
        grid=(b, nb // step),
        in_specs=[cur(MIX), prev, cur(KV_B), prev, cur(KV_B), pl.BlockSpec(memory_space=pltpu.SMEM)],
        out_specs=cur(MIX),
        out_shape=jax.ShapeDtypeStruct(q.shape, BF16),
        compiler_params=pltpu.CompilerParams(
            dimension_semantics=("arbitrary", "arbitrary"), vmem_limit_bytes=VMEM_LIMIT),
        name="swa",
    )(q, k, k, v, v, sink)


def _dilated_kernel(qn, knp, kn, vnp, vn, q4, k4p, k4, v4p, v4, q16, k16p, k16, v16p, v16,
                    o_ref, m_st, r_st):
    n_pairs = HALF_W // PAIR
    first_key = jnp.where(pl.program_id(1) > 0, 0, BLOCK)
    band = _band(0, None)
    band_first = _band(0, first_key)
    lane = lax.broadcasted_iota(jnp.int32, (1, LANES), 1)
    even = lane < HEAD_DIM

    def tile(q, kcat, vcat, valid, row0, stride, init):
        rows = pl.ds(row0, BLOCK) if stride == 1 else pl.ds(row0, BLOCK, stride=stride)
        for p in range(n_pairs):
            cols = slice(p * PAIR, (p + 1) * PAIR)
            for half in range(2):
                h = 2 * p + half
                sel = even if half == 0 else jnp.logical_not(even)
                s = _head_scores(q[:, cols], kcat[:, cols], sel, valid)
                m_tile = jnp.max(s, axis=1, keepdims=True)
                if init:
                    m_new = jnp.broadcast_to(m_tile, (BLOCK, LANES))
                    res = _weighted_values(jnp.exp2(s - m_tile), vcat[:, cols], sel)
                else:
                    m_old = m_st[h, rows, :]
                    m_new = jnp.maximum(m_old, m_tile)
                    pr = jnp.exp2(s - jnp.concatenate([m_new, m_new], axis=1))
                    res = (_weighted_values(pr, vcat[:, cols], sel)
                           + jnp.exp2(m_old - m_new) * r_st[h, rows, :])
                m_st[h, rows, :] = m_new
                r_st[h, rows, :] = res

    def d16(r, carry):
        tile(q16[r], _with_halo(k16p[r], k16[r]), _with_halo(v16p[r], v16[r]), band_first, r, 16, True)
        return carry

    lax.fori_loop(0, 16, d16, 0, unroll=4)

    tile(qn[0], _with_halo(knp[...], kn[0]), _with_halo(vnp[...], vn[0]), band_first, 0, 1, False)

    def d1(j, carry):
        tile(qn[j], _with_halo(kn[j - 1], kn[j]), _with_halo(vn[j - 1], vn[j]),
             band, pl.multiple_of(j * BLOCK, BLOCK), 1, False)
        return carry

    lax.fori_loop(1, CHUNK // BLOCK, d1, 0, unroll=5)

    def d4(r, carry):
        tile(q4[r, 0], _with_halo(k4p[r], k4[r, 0]), _with_halo(v4p[r], v4[r, 0]), band_first, r, 4, False)
        for j in range(1, CHUNK // (4 * BLOCK)):
            tile(q4[r, j], _with_halo(k4[r, j - 1], k4[r, j]), _with_halo(v4[r, j - 1], v4[r, j]),
                 band, r + 4 * j * BLOCK, 4, False)
        return carry

    lax.fori_loop(0, 4, d4, 0)

    def finish(j, carry):
        rows = pl.ds(pl.multiple_of(j * BLOCK, BLOCK), BLOCK)
        for p in range(n_pairs):
            out = _normalise_pair(r_st[2 * p, rows, :], r_st[2 * p + 1, rows, :], even)
            o_ref[j, :, p * PAIR:(p + 1) * PAIR] = out.astype(o_ref.dtype)
        return carry

    lax.fori_loop(0, CHUNK // BLOCK, finish, 0, unroll=2)


def _dilated_attention(nat, by4, by16):
    b, nb = nat[0].shape[:2]
    per_chunk = CHUNK // BLOCK
    n_chunks = nb // per_chunk

    def specs(dil):
        step = per_chunk // dil
        if dil == 1:
            cur = pl.BlockSpec((None, step, BLOCK, HALF_W), lambda bi, c, hf: (bi, c, 0, hf))
            prev = pl.BlockSpec((None, None, BLOCK, HALF_W),
                                lambda bi, c, hf: (bi, jnp.maximum(step * c - 1, 0), 0, hf))
        else:
            blocks = None if step == 1 else step
            cur = pl.BlockSpec((None, dil, blocks, BLOCK, HALF_W), lambda bi, c, hf: (bi, 0, c, 0, hf))
            prev = pl.BlockSpec((None, dil, None, BLOCK, HALF_W),
                                lambda bi, c, hf: (bi, 0, jnp.maximum(step * c - 1, 0), 0, hf))
        return cur, prev

    args, in_specs = [], []
    for (q, k, v), dil in ((nat, 1), (by4, 4), (by16, 16)):
        cur, prev = specs(dil)
        args += [q, k, k, v, v]
        in_specs += [cur, prev, cur, prev, cur]
    return pl.pallas_call(
        _dilated_kernel,
        grid=(b, n_chunks, MIX // HALF_W),
        in_specs=in_specs,
        out_specs=specs(1)[0],
        out_shape=jax.ShapeDtypeStruct(nat[0].shape, BF16),
        scratch_shapes=[pltpu.VMEM((HALF_W // HEAD_DIM, CHUNK, LANES), F32),
                        pltpu.VMEM((HALF_W // HEAD_DIM, CHUNK, LANES), F32)],
        compiler_params=pltpu.CompilerParams(
            dimension_semantics=("arbitrary", "arbitrary", "arbitrary"),
            vmem_limit_bytes=VMEM_LIMIT),
        name="dilated",
    )(*args)


def _pick_experts(logits):
    n = logits.shape[0]
    lane = lax.broadcasted_iota(jnp.int32, (n, LANES), 1)
    far = jnp.int32(2 * LANES)
    neg = -jnp.inf

    def first_argmax(vals):
        mx = jnp.max(vals, axis=1, keepdims=True)
        idx = jnp.min(jnp.where(vals == mx, lane, far), axis=1, keepdims=True)
        return mx, idx

    gl = jnp.where(lane < N_GROUPS, logits, neg)
    gmax, gidx = first_argmax(gl)
    g_top = 1.0 / jnp.sum(jnp.exp(gl - gmax), axis=1, keepdims=True)

    lo_lane = EXPERT_LANE0 + EXPERTS_PER_GROUP * gidx
    el = jnp.where((lane >= lo_lane) & (lane < lo_lane + EXPERTS_PER_GROUP), logits, neg)
    m1, i1 = first_argmax(el)
    m2, i2 = first_argmax(jnp.where(lane == i1, neg, el))
    t2 = jnp.exp(m2 - m1)
    return i1 - EXPERT_LANE0, i2 - EXPERT_LANE0, g_top / (1.0 + t2), g_top * t2 / (1.0 + t2)


def _out_router_kernel(oa_ref, ob_ref, x_ref, ga_ref, gb_ref, wo_ref, gf_ref, wr_ref, tri_ref,
                       x1_ref, h_ref, meta_ref, meta_t_ref, cnt_ref, carry_ref):
    @pl.when(pl.program_id(0) == 0)
    def _():
        carry_ref[...] = jnp.zeros_like(carry_ref)

    tm, d = x_ref.shape
    tok = d // (2 * LANES)
    logits = []
    for r0 in range(0, tm, ROUTE_BLOCK):
        rows = slice(r0, r0 + ROUTE_BLOCK)
        na = _rms_rows(oa_ref[rows, :].astype(F32)) * ga_ref[...]
        nb = _rms_rows(ob_ref[rows, :].astype(F32)) * gb_ref[...]
        mixed = jnp.concatenate([na, nb], axis=1).astype(BF16)
        x1 = x_ref[rows, :] + _dot(mixed, wo_ref[...])
        x1_ref[rows, :] = x1
        h = _rms_rows(x1) * gf_ref[...]
        _store_token_tiles(h_ref.at[r0 * tok:(r0 + ROUTE_BLOCK) * tok], _pack_bf16_pairs(h))

        h_hi, h_lo = _split2(h)
        parts = _dot(h_hi, wr_ref[...]) + _dot(h_lo, wr_ref[...])
        logits.append(parts[:, :LANES] + parts[:, LANES:])

    e0, e1, gate0, gate1 = _pick_experts(jnp.concatenate(logits, axis=0))

    lane = lax.broadcasted_iota(jnp.int32, (tm, LANES), 1)
    oh0 = lane == e0
    oh1 = lane == e1
    cum = _dot(tri_ref[...], jnp.concatenate([oh0.astype(BF16), oh1.astype(BF16)], axis=1))
    tot0 = jnp.sum(oh0.astype(F32), axis=0, keepdims=True)
    tot1 = jnp.sum(oh1.astype(F32), axis=0, keepdims=True)
    base = carry_ref[...]
    rank0 = jnp.sum(jnp.where(oh0, cum[:, :LANES] + base, 0.0), axis=1, keepdims=True)
    rank1 = jnp.sum(jnp.where(oh1, cum[:, LANES:] + (base + tot0), 0.0), axis=1, keepdims=True)
    base = base + tot0 + tot1
    carry_ref[...] = base
    cnt_ref[...] = base

    meta = jnp.zeros((tm, LANES), F32)
    for ln, val in ((META_E0, e0.astype(F32)), (META_E1, e1.astype(F32)), (META_G0, gate0),
                    (META_G1, gate1), (META_R0, rank0), (META_R1, rank1)):
        meta = jnp.where(lane == ln, val, meta)
    meta_ref[...] = meta
    meta_t_ref[...] = meta.T[:SUBLANES, :]


def _out_router(oa, ob, x2d, ga, gb, wo_bf16, gf, wr, tri):
    t, d = x2d.shape
    tm = ROW_TILE
    tok = d // (2 * LANES)
    row = lambda i: (i, 0)
    fixed = lambda i: (0, 0)
    return pl.pallas_call(
        _out_router_kernel,
        grid=(t // tm,),
        in_specs=[
            pl.BlockSpec((tm, MIX), row),
            pl.BlockSpec((tm, MIX), row),
            pl.BlockSpec((tm, d), row),
            pl.BlockSpec((1, MIX), fixed),
            pl.BlockSpec((1, MIX), fixed),
            pl.BlockSpec((2 * MIX, d), fixed),
            pl.BlockSpec((1, d), fixed),
            pl.BlockSpec((d, 2 * LANES), fixed),
            pl.BlockSpec((tm, tm), fixed),
        ],
        out_specs=[
            pl.BlockSpec((tm, d), row),
            pl.BlockSpec((tm * tok, LANES), row),
            pl.BlockSpec((tm, LANES), row),
            pl.BlockSpec((None, SUBLANES, tm), lambda i: (i, 0, 0)),
            pl.BlockSpec((1, LANES), fixed),
        ],
        out_shape=[
            jax.ShapeDtypeStruct((t, d), F32),
            jax.ShapeDtypeStruct((t * tok, LANES), jnp.uint32),
            jax.ShapeDtypeStruct((t, LANES), F32),
            jax.ShapeDtypeStruct((t // tm, SUBLANES, tm), F32),
            jax.ShapeDtypeStruct((1, LANES), F32),
        ],
        scratch_shapes=[pltpu.VMEM((1, LANES), F32)],
        compiler_params=pltpu.CompilerParams(
            dimension_semantics=("arbitrary",), vmem_limit_bytes=VMEM_LIMIT),
        name="out_router",
    )(oa, ob, x2d, ga, gb, wo_bf16, gf, wr, tri)


def _token_copy(src, i, dst, j, tok, sem):
    return pltpu.make_async_copy(src.at[pl.ds(pl.multiple_of(i * tok, tok), tok)],
                                 dst.at[pl.ds(pl.multiple_of(j * tok, tok), tok)], sem)


def _dispatch_kernel(slot_ref, h_ref, xs_in_ref, xs_ref, sem, *, tok):
    del xs_in_ref
    tm = h_ref.shape[0] // tok

    def start(blk, c):
        for u in range(DMA_UNROLL):
            i = blk * DMA_UNROLL + u
            for k in range(TOP_K):
                _token_copy(h_ref, i, xs_ref, slot_ref[0, k * tm + i], tok, sem).start(priority=k)
        return c

    lax.fori_loop(0, tm // DMA_UNROLL, start, 0)
    for k in range(TOP_K):
        pltpu.make_async_copy(h_ref, xs_ref.at[pl.ds(0, tm * tok)], sem).wait()


def _dispatch(slots, h_tok, xs_buf, tok):
    tm = ROW_TILE
    n_tiles = h_tok.shape[0] // (tm * tok)
    return pl.pallas_call(
        functools.partial(_dispatch_kernel, tok=tok),
        grid=(n_tiles,),
        in_specs=[
            pl.BlockSpec((None, 1, TOP_K * tm), lambda i: (i, 0, 0), memory_space=pltpu.SMEM),
            pl.BlockSpec((tm * tok, LANES), lambda i: (i, 0)),
            pl.BlockSpec(memory_space=pl.ANY),
        ],
        out_specs=pl.BlockSpec(memory_space=pl.ANY),
        out_shape=jax.ShapeDtypeStruct(xs_buf.shape, xs_buf.dtype),
        scratch_shapes=[pltpu.SemaphoreType.DMA(())],
        input_output_aliases={2: 0},
        compiler_params=pltpu.CompilerParams(
            dimension_semantics=("arbitrary",), vmem_limit_bytes=VMEM_LIMIT),
        name="dispatch",
    )(slots, h_tok, xs_buf)


def _expert_kernel(te_ref, nv_ref, xs_ref, wgu_ref, wdn_ref, ys_ref, wgu_bf, wdn_bf):
    i = pl.program_id(0)
    d = wgu_ref.shape[0]
    tok = d // (2 * LANES)
    tg = xs_ref.shape[0] // tok

    @pl.when((i == 0) | (te_ref[i] != te_ref[jnp.maximum(i - 1, 0)]))
    def _():
        wgu_bf[...] = wgu_ref[...].astype(BF16)
        wdn_bf[...] = wdn_ref[...].astype(BF16)

    @pl.when(i < nv_ref[0])
    def _():
        xs = _unpack_bf16_pairs(_load_token_tiles(xs_ref, tg, tok))
        gu = _dot(xs.astype(BF16), wgu_bf[...])
        g = gu[:, :D_EXPERT]
        a = (g * (1.0 / (1.0 + jnp.exp(-g)))) * gu[:, D_EXPERT:]
        _store_token_tiles(ys_ref, _pack_bf16_pairs(_dot(a.astype(BF16), wdn_bf[...])))

    @pl.when(i >= nv_ref[0])
    def _():
        ys_ref[...] = jnp.zeros_like(ys_ref)


def _experts(tile_expert, n_valid, xs, wgu, wdn, layer):
    d = wgu.shape[2]
    tok = d // (2 * LANES)
    tg = GROUP_TILE
    grid_spec = pltpu.PrefetchScalarGridSpec(
        num_scalar_prefetch=2,
        grid=(xs.shape[0] // (tg * tok),),
        in_specs=[
            pl.BlockSpec((tg * tok, LANES), lambda i, te, nv: (i, 0)),
            pl.BlockSpec((None, None, d, 2 * D_EXPERT), lambda i, te, nv: (layer, te[i], 0, 0)),
            pl.BlockSpec((None, None, D_EXPERT, d), lambda i, te, nv: (layer, te[i], 0, 0)),
        ],
        out_specs=pl.BlockSpec((tg * tok, LANES), lambda i, te, nv: (i, 0)),
        scratch_shapes=[pltpu.VMEM((d, 2 * D_EXPERT), BF16), pltpu.VMEM((D_EXPERT, d), BF16)],
    )
    return pl.pallas_call(
        _expert_kernel,
        grid_spec=grid_spec,
        out_shape=jax.ShapeDtypeStruct(xs.shape, xs.dtype),
        compiler_params=pltpu.CompilerParams(
            dimension_semantics=("arbitrary",), vmem_limit_bytes=VMEM_LIMIT),
        name="experts",
    )(tile_expert, n_valid, xs, wgu, wdn)


def _combine_kernel(slot_ref, slot_next_ref, x_ref, meta_ref, ys_ref, out_ref, buf, sems):
    step = pl.program_id(0)
    tm, d = x_ref.shape
    tok = d // (2 * LANES)

    def gather(slots, half):
        def start(blk, c):
            for u in range(DMA_UNROLL):
                i = blk * DMA_UNROLL + u
                for k in range(TOP_K):
                    _token_copy(ys_ref, slots[0, k * tm + i], buf.at[half, k], i, tok,
                                sems.at[half]).start(priority=k)
            return c

        lax.fori_loop(0, tm // DMA_UNROLL, start, 0)

    cur = step % 2

    @pl.when(step == 0)
    def _():
        gather(slot_ref, 0)

    @pl.when(step + 1 < pl.num_programs(0))
    def _():
        gather(slot_next_ref, 1 - cur)

    for k in range(TOP_K):
        pltpu.make_async_copy(ys_ref.at[pl.ds(0, tm * tok)], buf.at[cur, k], sems.at[cur]).wait()
    meta = meta_ref[...]
    lane = lax.broadcasted_iota(jnp.int32, (1, LANES), 1)
    g0 = _lane_col(meta, lane, META_G0, 0.0, jnp.sum)
    g1 = _lane_col(meta, lane, META_G1, 0.0, jnp.sum)
    y = (g0 * _unpack_bf16_pairs(_load_token_tiles(buf.at[cur, 0], tm, tok))
         + g1 * _unpack_bf16_pairs(_load_token_tiles(buf.at[cur, 1], tm, tok)))
    out_ref[...] = x_ref[...] + y


def _combine(slots, x1, meta, ys):
    t, d = x1.shape
    tm = ROW_TILE
    tok = d // (2 * LANES)
    n_tiles = t // tm
    return pl.pallas_call(
        _combine_kernel,
        grid=(n_tiles,),
        in_specs=[
            pl.BlockSpec((None, 1, TOP_K * tm), lambda i: (i, 0, 0), memory_space=pltpu.SMEM),
            pl.BlockSpec((None, 1, TOP_K * tm), lambda i: (jnp.minimum(i + 1, n_tiles - 1), 0, 0),
                         memory_space=pltpu.SMEM),
            pl.BlockSpec((tm, d), lambda i: (i, 0)),
            pl.BlockSpec((tm, LANES), lambda i: (i, 0)),
            pl.BlockSpec(memory_space=pl.ANY),
        ],
        out_specs=pl.BlockSpec((tm, d), lambda i: (i, 0)),
        out_shape=jax.ShapeDtypeStruct((t, d), F32),
        scratch_shapes=[pltpu.VMEM((2, TOP_K, tm * tok, LANES), ys.dtype),
                        pltpu.SemaphoreType.DMA((2,))],
        compiler_params=pltpu.CompilerParams(
            dimension_semantics=("arbitrary",), vmem_limit_bytes=VMEM_LIMIT),
        name="combine",
    )(slots, slots, x1, meta, ys)


def _rope_expand_matrix():
    src = jnp.arange(LANES)[:, None]
    dst = jnp.arange(LANES)[None, :]
    in_head = dst % HEAD_DIM
    cos_m = (src < ROPE_HALF) & (in_head < ROPE_DIM) & (in_head % ROPE_HALF == src)
    is_sin = (src >= ROPE_HALF) & (src < ROPE_DIM)
    sin_lo = is_sin & (in_head >= ROPE_HALF) & (in_head < ROPE_DIM) & (in_head - ROPE_HALF == src - ROPE_HALF)
    sin_hi = is_sin & (in_head < ROPE_HALF) & (in_head == src - ROPE_HALF)
    e = jnp.concatenate([cos_m.astype(F32), sin_lo.astype(F32), -sin_hi.astype(F32)], axis=1)
    return jnp.concatenate([e, e, e], axis=0).astype(BF16)


def _block_diag_ones():
    i = jnp.arange(2 * LANES)
    return (i[:, None] // HEAD_DIM == i[None, :] // HEAD_DIM).astype(BF16)


def _tile4(g):
    return jnp.tile(g.astype(F32), 2 * LANES // HEAD_DIM)


_B_HEAD_ORDER = tuple(h for p in range(N_HEADS // 2) for h in (p, p + N_HEADS // 2))


def _b_cols():
    order = jnp.asarray(_B_HEAD_ORDER)
    return (order[:, None] * HEAD_DIM + jnp.arange(HEAD_DIM)[None, :]).reshape(-1)


def kernel(x, positions, attn_norm, w_in, q_norm_a, k_norm_a, q_norm_b, k_norm_b, sinks_b,
           out_norm_a, out_norm_b, w_out, ffn_norm, w_router_group, w_router_expert,
           w_gate_up, w_down):
    b, s, d = x.shape
    t = b * s
    depth = w_in.shape[0]
    tok = d // (2 * LANES)
    assert s % CHUNK == 0 and s % ROW_TILE == 0 and s % SWA_STEP == 0
    assert ROW_TILE % (2 * SUBLANES * max(DILATIONS)) == 0 and ROW_TILE % ROUTE_BLOCK == 0 and ROUTE_BLOCK % ROUTE_ROWS == 0
    assert d == 2 * MIX and d % (2 * LANES) == 0

    inv_freq = ROPE_THETA ** (-jnp.arange(0, ROPE_DIM, 2, dtype=F32) / ROPE_DIM)
    ang = positions.astype(F32)[None, :, :] * inv_freq[:, None, None]
    cos_sin = jnp.moveaxis(jnp.concatenate([jnp.cos(ang), jnp.sin(ang)], axis=0), 0, -1)
    rope_tab = jnp.pad(cos_sin, ((0, 0), (0, 0), (0, LANES - ROPE_DIM))).reshape(t, LANES)
    e3 = _rope_expand_matrix()
    bd = _block_diag_ones()
    row_i = jnp.arange(ROW_TILE)
    tri = (row_i[None, :] < row_i[:, None]).astype(BF16)
    bcols = _b_cols()
    q_scale = HEAD_DIM ** -0.5 * LOG2E
    n_sorted = t * TOP_K + N_EXPERTS * GROUP_TILE
    n_tiles = n_sorted // GROUP_TILE
    xs_buf = jnp.zeros((n_sorted * tok, LANES), jnp.uint32)
    nb = s // BLOCK

    x2d = x.reshape(t, d)
    for l in range(depth):
        qb0 = 3 * MIX
        w_l = w_in[l]
        w_l = jnp.concatenate([w_l[:, :qb0], w_l[:, qb0:qb0 + MIX][:, bcols], w_l[:, qb0 + MIX:]], axis=1)
        head_gains = jnp.stack([_tile4(q_norm_a[l]) * q_scale, _tile4(k_norm_a[l]),
                                _tile4(q_norm_b[l]) * q_scale, _tile4(k_norm_b[l])])
        sink = sinks_b[l][jnp.asarray(_B_HEAD_ORDER)].astype(F32) * LOG2E
        gb_perm = out_norm_b[l][bcols]
        wo_l = jnp.concatenate([w_out[l][:MIX], w_out[l][MIX:][bcols]], axis=0).astype(BF16)
        wr = jnp.concatenate(
            [w_router_group[l], jnp.zeros((d, EXPERT_LANE0 - N_GROUPS), F32), w_router_expert[l],
             jnp.zeros((d, LANES - EXPERT_LANE0 - N_EXPERTS), F32)], axis=1)
        wr_hi = wr.astype(BF16)
        wr_split = jnp.concatenate([wr_hi, (wr - wr_hi.astype(F32)).astype(BF16)], axis=1)

        (qa, ka, va, qa4, ka4, va4, qa16, ka16, va16, qb, kb, vb) = _proj(
            x2d, b, attn_norm[l].reshape(1, d), w_l.astype(BF16), rope_tab, e3, bd, head_gains)
        blocks = lambda a: a.reshape(b, nb, BLOCK, a.shape[-1])
        class_blocks = lambda a: a.reshape(b, a.shape[1], a.shape[2] // BLOCK, BLOCK, MIX)
        oa = _dilated_attention(tuple(map(blocks, (qa, ka, va))),
                                tuple(map(class_blocks, (qa4, ka4, va4))),
                                tuple(map(class_blocks, (qa16, ka16, va16))))
        ob = _swa_attention(blocks(qb), blocks(kb), blocks(vb), sink)

        x1, h_tok, meta, meta_t, counts = _out_router(
            oa.reshape(t, MIX), ob.reshape(t, MIX), x2d, out_norm_a[l].reshape(1, MIX),
            gb_perm.reshape(1, MIX), wo_l, ffn_norm[l].reshape(1, d), wr_split, tri)

        cnt = counts[0, :N_EXPERTS].astype(jnp.int32)
        padded = ((cnt + GROUP_TILE - 1) // GROUP_TILE) * GROUP_TILE
        ends = jnp.cumsum(padded)
        offs = ends - padded
        eid = meta_t[:, META_E0:META_E1 + 1, :].astype(jnp.int32)
        rank = meta_t[:, META_R0:META_R1 + 1, :].astype(jnp.int32)
        is_expert = eid[..., None] == jnp.arange(N_EXPERTS, dtype=jnp.int32)
        slots = jnp.sum(jnp.where(is_expert, offs, 0), axis=-1) + rank
        slots = slots.reshape(t // ROW_TILE, 1, TOP_K * ROW_TILE)
        tile_start = jnp.arange(n_tiles, dtype=jnp.int32) * GROUP_TILE
        tile_expert = jnp.minimum(
            jnp.sum((tile_start[:, None] >= ends[None, :]).astype(jnp.int32), axis=1), N_EXPERTS - 1)
        n_valid = (ends[-1:] // GROUP_TILE).astype(jnp.int32)

        xs_buf = _dispatch(slots, h_tok, xs_buf, tok)
        ys = _experts(tile_expert, n_valid, xs_buf, w_gate_up, w_down, l)
        x2d = _combine(slots, x1, meta, ys)
    return x2d.reshape(b, s, d)
```

```python
import functools
import math

import jax
import jax.numpy as jnp
from jax import lax
from jax.experimental import pallas as pl
from jax.experimental.pallas import tpu as pltpu

F32 = jnp.float32
BF16 = jnp.bfloat16

HEAD_DIM = 64
N_HEADS = 8
MIX = N_HEADS * HEAD_DIM
N_KV_B = 2
KV_B = N_KV_B * HEAD_DIM
DILATIONS = (1, 4, 16)
BLOCK = 128
CHUNK = BLOCK * max(DILATIONS)
SWA_BACK = 127
SWA_STEP = 8 * BLOCK
ROPE_DIM = HEAD_DIM // 4
ROPE_HALF = ROPE_DIM // 2
ROPE_THETA = 500000.0
N_GROUPS = 4
EXPERTS_PER_GROUP = 8
N_EXPERTS = N_GROUPS * EXPERTS_PER_GROUP
TOP_K = 2
D_EXPERT = 512
EPS = 1e-6
LOG2E = math.log2(math.e)

LANES = 128
SUBLANES = 8
PAIR = 2 * HEAD_DIM
HALF_W = MIX // 2
EXPERT_LANE0 = 32
ROW_TILE = 512
ROUTE_BLOCK = 256
ROUTE_ROWS = 128
GROUP_TILE = 512
DMA_UNROLL = 8
VMEM_LIMIT = 48 * 1024 * 1024

META_E0, META_E1, META_G0, META_G1, META_R0, META_R1 = 0, 1, 2, 3, 4, 5


def _split2(a):
    hi = a.astype(BF16)
    lo = (a - hi.astype(F32)).astype(BF16)
    return hi, lo


def _split3(a):
    hi = a.astype(BF16)
    r = a - hi.astype(F32)
    mid = r.astype(BF16)
    lo = (r - mid.astype(F32)).astype(BF16)
    return hi, mid, lo


def _dot(a, b):
    return jnp.dot(a, b, preferred_element_type=F32)


def _rms_rows(a):
    return a * lax.rsqrt(jnp.mean(a * a, axis=-1, keepdims=True) + EPS)


def _lane_col(tile, lane, idx, fill, reduce):
    return reduce(jnp.where(lane == idx, tile, fill), axis=1, keepdims=True)


def _pack_bf16_pairs(val):
    w = val.shape[1] // 2
    bits = lax.bitcast_convert_type(val, jnp.uint32) + jnp.uint32(0x8000)
    return (bits[:, :w] >> 16) | (bits[:, w:] & jnp.uint32(0xFFFF0000))


def _unpack_bf16_pairs(words):
    lo = lax.bitcast_convert_type(words << 16, F32)
    hi = lax.bitcast_convert_type(words & jnp.uint32(0xFFFF0000), F32)
    return jnp.concatenate([lo, hi], axis=1)


def _store_token_tiles(ref, val):
    rows, width = val.shape
    n = width // LANES
    for c in range(n):
        ref[pl.ds(c, rows, stride=n), :] = val[:, c * LANES:(c + 1) * LANES]


def _load_token_tiles(ref, rows, n):
    return jnp.concatenate([ref[pl.ds(c, rows, stride=n), :] for c in range(n)], axis=1)


def _proj_kernel(x_ref, g_ref, w_ref, rope_ref, e3_ref, bd_ref, hg_ref,
                 qa_ref, ka_ref, va_ref, qa4_ref, ka4_ref, va4_ref, qa16_ref, ka16_ref, va16_ref,
                 qb_ref, kb_ref, vb_ref, scr, scr4):
    h = _rms_rows(x_ref[...]) * g_ref[...]
    acc = _dot(h.astype(BF16), w_ref[...])
    tm = acc.shape[0]

    hi, mid, lo = _split3(rope_ref[...])
    tab = _dot(jnp.concatenate([hi, mid, lo], axis=1), e3_ref[...])
    lane = lax.broadcasted_iota(jnp.int32, (1, LANES), 1)
    not_rope = ((lane & (HEAD_DIM - 1)) >= ROPE_DIM).astype(F32)
    c = tab[:, :LANES] + not_rope
    s_lo = tab[:, LANES:2 * LANES]
    s_hi = tab[:, 2 * LANES:]
    bd = bd_ref[...]

    def norm_rope(a, gain):
        w = a.shape[1]
        a2_hi, a2_lo = _split2(a * a)
        ss = _dot(a2_hi, bd[:w, :w]) + _dot(a2_lo, bd[:w, :w])
        y = (a * lax.rsqrt(ss * (1.0 / HEAD_DIM) + EPS)) * gain
        outs = []
        for j in range(w // LANES):
            yj = y[:, j * LANES:(j + 1) * LANES]
            outs.append(yj * c + pltpu.roll(yj, ROPE_HALF, 1) * s_lo
                        + pltpu.roll(yj, LANES - ROPE_HALF, 1) * s_hi)
        return outs[0] if len(outs) == 1 else jnp.concatenate(outs, axis=1)

    def normed(col0, width, gain_row):
        outs = []
        for c0 in range(col0, col0 + width, 2 * LANES):
            wd = min(2 * LANES, col0 + width - c0)
            outs.append(norm_rope(acc[:, c0:c0 + wd], hg_ref[gain_row:gain_row + 1, :wd]))
        return outs[0] if len(outs) == 1 else jnp.concatenate(outs, axis=1)

    def emit_by_class(val, nat_ref, ref4, ref16):
        n_ct = MIX // LANES
        q4 = tm // 4
        nat_ref[...] = val.astype(BF16)
        for ct in range(n_ct):
            scr[ct] = val[:, ct * LANES:(ct + 1) * LANES]
        for r4 in range(4):
            rows = [scr[ct, pl.ds(r4, q4, stride=4), :] for ct in range(n_ct)]
            ref4[r4] = jnp.concatenate(rows, axis=1).astype(BF16)
            for ct in range(n_ct):
                scr4[ct, r4 * q4:(r4 + 1) * q4, :] = rows[ct]
        for r16 in range(16):
            first = (r16 % 4) * q4 + r16 // 4
            rows = [scr4[ct, pl.ds(first, tm // 16, stride=4), :] for ct in range(n_ct)]
            ref16[r16] = jnp.concatenate(rows, axis=1).astype(BF16)

    emit_by_class(normed(0, MIX, 0), qa_ref, qa4_ref, qa16_ref)
    emit_by_class(normed(MIX, MIX, 1), ka_ref, ka4_ref, ka16_ref)
    emit_by_class(acc[:, 2 * MIX:3 * MIX], va_ref, va4_ref, va16_ref)
    qb_ref[...] = normed(3 * MIX, MIX, 2).astype(BF16)
    kb_ref[...] = normed(4 * MIX, KV_B, 3).astype(BF16)
    vb_ref[...] = acc[:, 4 * MIX + KV_B:].astype(BF16)


def _proj(x2d, b, gain, w_bf16, rope_tab, e3, bd, head_gains):
    t, d = x2d.shape
    s = t // b
    pw = w_bf16.shape[1]
    tm = ROW_TILE
    nt = s // tm
    row = lambda bi, i: (bi * nt + i, 0)
    fixed = lambda bi, i: (0, 0)
    by_class = lambda bi, i: (bi, 0, i, 0)
    nat = lambda w: (jax.ShapeDtypeStruct((t, w), BF16), pl.BlockSpec((tm, w), row))
    grouped = lambda dil: (jax.ShapeDtypeStruct((b, dil, s // dil, MIX), BF16),
                           pl.BlockSpec((None, dil, tm // dil, MIX), by_class))
    outs = [nat(MIX)] * 3 + [grouped(4)] * 3 + [grouped(16)] * 3 + [nat(MIX), nat(KV_B), nat(KV_B)]
    return pl.pallas_call(
        _proj_kernel,
        grid=(b, nt),
        in_specs=[
            pl.BlockSpec((tm, d), row),
            pl.BlockSpec((1, d), fixed),
            pl.BlockSpec((d, pw), fixed),
            pl.BlockSpec((tm, LANES), row),
            pl.BlockSpec(e3.shape, fixed),
            pl.BlockSpec(bd.shape, fixed),
            pl.BlockSpec(head_gains.shape, fixed),
        ],
        out_specs=[o[1] for o in outs],
        out_shape=[o[0] for o in outs],
        scratch_shapes=[pltpu.VMEM((MIX // LANES, tm, LANES), F32)] * 2,
        compiler_params=pltpu.CompilerParams(
            dimension_semantics=("arbitrary", "arbitrary"), vmem_limit_bytes=VMEM_LIMIT),
        name="proj",
    )(x2d, gain, w_bf16, rope_tab, e3, bd, head_gains)


def _band(min_back, first_key):
    qi = lax.broadcasted_iota(jnp.int32, (BLOCK, 2 * BLOCK), 0)
    kj = lax.broadcasted_iota(jnp.int32, (BLOCK, 2 * BLOCK), 1)
    valid = (kj >= qi + min_back) & (kj <= qi + BLOCK)
    return valid if first_key is None else valid & (kj >= first_key)


def _head_scores(q_pair, kcat, sel, valid):
    qm = jnp.where(sel, q_pair, jnp.zeros_like(q_pair))
    s = lax.dot_general(qm, kcat, (((1,), (1,)), ((), ())), preferred_element_type=F32)
    return jnp.where(valid, s, -jnp.inf)


def _weighted_values(pr, v_pair, sel):
    return _dot(pr.astype(BF16), jnp.where(sel, v_pair, jnp.ones_like(v_pair)))


def _normalise_pair(res_even, res_odd, even, extra=None):
    num = jnp.where(even, res_even, res_odd)
    den = pltpu.roll(jnp.where(even, res_odd, res_even), HEAD_DIM, 1)
    return num / (den if extra is None else den + extra)


def _with_halo(prev, cur):
    return jnp.concatenate([prev, cur], axis=0)


def _swa_kernel(q_ref, kp_ref, k_ref, vp_ref, v_ref, sink_ref, o_ref):
    first_key = jnp.where(pl.program_id(1) > 0, 0, BLOCK)
    band = _band(BLOCK - SWA_BACK, None)
    band_first = _band(BLOCK - SWA_BACK, first_key)
    lane = lax.broadcasted_iota(jnp.int32, (1, LANES), 1)
    even = lane < HEAD_DIM
    for j in range(SWA_STEP // BLOCK):
        valid = band_first if j == 0 else band
        kcat = _with_halo(kp_ref[...] if j == 0 else k_ref[j - 1], k_ref[j])
        vcat = _with_halo(vp_ref[...] if j == 0 else v_ref[j - 1], v_ref[j])
        for p in range(N_HEADS // 2):
            cols = slice(p * PAIR, (p + 1) * PAIR)
            q_pair = q_ref[j, :, cols]
            res, sink_term = [], []
            for half in range(2):
                sink = sink_ref[2 * p + half]
                sel = even if half == 0 else jnp.logical_not(even)
                s = _head_scores(q_pair, kcat, sel, valid)
                m = jnp.maximum(jnp.max(s, axis=1, keepdims=True), sink)
                res.append(_weighted_values(jnp.exp2(s - m), vcat, sel))
                sink_term.append(jnp.exp2(sink - m))
            extra = jnp.where(even, sink_term[0], sink_term[1])
            o_ref[j, :, cols] = _normalise_pair(res[0], res[1], even, extra).astype(o_ref.dtype)


def _swa_attention(q, k, v, sink):
    b, nb = q.shape[:2]
    step = SWA_STEP // BLOCK
    cur = lambda w: pl.BlockSpec((None, step, BLOCK, w), lambda bi, c: (bi, c, 0, 0))
    prev = pl.BlockSpec((None, None, BLOCK, KV_B),
                        lambda bi, c: (bi, jnp.maximum(step * c - 1, 0), 0, 0))
    return pl.pallas_call(
        _swa_kernel,
        grid=(b, nb // step),
        in_specs=[cur(MIX), prev, cur(KV_B), prev, cur(KV_B), pl.BlockSpec(memory_space=pltpu.SMEM)],
        out_specs=cur(MIX),
        out_shape=jax.ShapeDtypeStruct(q.shape, BF16),
        compiler_params=pltpu.CompilerParams(
            dimension_semantics=("arbitrary", "arbitrary"), vmem_limit_bytes=VMEM_LIMIT),
        name="swa",
    )(q, k, k, v, v, sink)


def _dilated_kernel(qn, knp, kn, vnp, vn, q4, k4p, k4, v4p, v4, q16, k16p, k16, v16p, v16,
                    o_ref, m_st, r_st):
    n_pairs = HALF_W // PAIR
    first_key = jnp.where(pl.program_id(1) > 0, 0, BLOCK)
    band = _band(0, None)
    band_first = _band(0, first_key)
    lane = lax.broadcasted_iota(jnp.int32, (1, LANES), 1)
    even = lane < HEAD_DIM

    def tile(q, kcat, vcat, valid, row0, stride, init):
        rows = pl.ds(row0, BLOCK) if stride == 1 else pl.ds(row0, BLOCK, stride=stride)
        for p in range(n_pairs):
            cols = slice(p * PAIR, (p + 1) * PAIR)
            for half in range(2):
                h = 2 * p + half
                sel = even if half == 0 else jnp.logical_not(even)
                s = _head_scores(q[:, cols], kcat[:, cols], sel, valid)
                m_tile = jnp.max(s, axis=1, keepdims=True)
                if init:
                    m_new = jnp.broadcast_to(m_tile, (BLOCK, LANES))
                    res = _weighted_values(jnp.exp2(s - m_tile), vcat[:, cols], sel)
                else:
                    m_old = m_st[h, rows, :]
                    m_new = jnp.maximum(m_old, m_tile)
                    pr = jnp.exp2(s - jnp.concatenate([m_new, m_new], axis=1))
                    res = (_weighted_values(pr, vcat[:, cols], sel)
                           + jnp.exp2(m_old - m_new) * r_st[h, rows, :])
                m_st[h, rows, :] = m_new
                r_st[h, rows, :] = res

    def d16(r, carry):
        tile(q16[r], _with_halo(k16p[r], k16[r]), _with_halo(v16p[r], v16[r]), band_first, r, 16, True)
        return carry

    lax.fori_loop(0, 16, d16, 0, unroll=True)

    tile(qn[0], _with_halo(knp[...], kn[0]), _with_halo(vnp[...], vn[0]), band_first, 0, 1, False)

    def d1(j, carry):
        tile(qn[j], _with_halo(kn[j - 1], kn[j]), _with_halo(vn[j - 1], vn[j]),
             band, pl.multiple_of(j * BLOCK, BLOCK), 1, False)
        return carry

    lax.fori_loop(1, CHUNK // BLOCK, d1, 0, unroll=True)

    def d4(r, carry):
        tile(q4[r, 0], _with_halo(k4p[r], k4[r, 0]), _with_halo(v4p[r], v4[r, 0]), band_first, r, 4, False)
        for j in range(1, CHUNK // (4 * BLOCK)):
            tile(q4[r, j], _with_halo(k4[r, j - 1], k4[r, j]), _with_halo(v4[r, j - 1], v4[r, j]),
                 band, r + 4 * j * BLOCK, 4, False)
        return carry

    lax.fori_loop(0, 4, d4, 0, unroll=True)

    def finish(j, carry):
        rows = pl.ds(pl.multiple_of(j * BLOCK, BLOCK), BLOCK)
        for p in range(n_pairs):
            out = _normalise_pair(r_st[2 * p, rows, :], r_st[2 * p + 1, rows, :], even)
            o_ref[j, :, p * PAIR:(p + 1) * PAIR] = out.astype(o_ref.dtype)
        return carry

    lax.fori_loop(0, CHUNK // BLOCK, finish, 0, unroll=4)


def _dilated_attention(nat, by4, by16):
    b, nb = nat[0].shape[:2]
    per_chunk = CHUNK // BLOCK
    n_chunks = nb // per_chunk

    def specs(dil):
        step = per_chunk // dil
        if dil == 1:
            cur = pl.BlockSpec((None, step, BLOCK, HALF_W), lambda bi, c, hf: (bi, c, 0, hf))
            prev = pl.BlockSpec((None, None, BLOCK, HALF_W),
                                lambda bi, c, hf: (bi, jnp.maximum(step * c - 1, 0), 0, hf))
        else:
            blocks = None if step == 1 else step
            cur = pl.BlockSpec((None, dil, blocks, BLOCK, HALF_W), lambda bi, c, hf: (bi, 0, c, 0, hf))
            prev = pl.BlockSpec((None, dil, None, BLOCK, HALF_W),
                                lambda bi, c, hf: (bi, 0, jnp.maximum(step * c - 1, 0), 0, hf))
        return cur, prev

    args, in_specs = [], []
    for (q, k, v), dil in ((nat, 1), (by4, 4), (by16, 16)):
        cur, prev = specs(dil)
        args += [q, k, k, v, v]
        in_specs += [cur, prev, cur, prev, cur]
    return pl.pallas_call(
        _dilated_kernel,
        grid=(b, n_chunks, MIX // HALF_W),
        in_specs=in_specs,
        out_specs=specs(1)[0],
        out_shape=jax.ShapeDtypeStruct(nat[0].shape, BF16),
        scratch_shapes=[pltpu.VMEM((HALF_W // HEAD_DIM, CHUNK, LANES), F32),
                        pltpu.VMEM((HALF_W // HEAD_DIM, CHUNK, LANES), F32)],
        compiler_params=pltpu.CompilerParams(
            dimension_semantics=("arbitrary", "arbitrary", "arbitrary"),
            vmem_limit_bytes=VMEM_LIMIT),
        name="dilated",
    )(*args)


def _pick_experts(logits):
    n = logits.shape[0]
    lane = lax.broadcasted_iota(jnp.int32, (n, LANES), 1)
    far = jnp.int32(2 * LANES)
    neg = -jnp.inf

    def first_argmax(vals):
        mx = jnp.max(vals, axis=1, keepdims=True)
        idx = jnp.min(jnp.where(vals == mx, lane, far), axis=1, keepdims=True)
        return mx, idx

    gl = jnp.where(lane < N_GROUPS, logits, neg)
    gmax, gidx = first_argmax(gl)
    g_top = 1.0 / jnp.sum(jnp.exp(gl - gmax), axis=1, keepdims=True)

    lo_lane = EXPERT_LANE0 + EXPERTS_PER_GROUP * gidx
    el = jnp.where((lane >= lo_lane) & (lane < lo_lane + EXPERTS_PER_GROUP), logits, neg)
    m1, i1 = first_argmax(el)
    m2, i2 = first_argmax(jnp.where(lane == i1, neg, el))
    t2 = jnp.exp(m2 - m1)
    return i1 - EXPERT_LANE0, i2 - EXPERT_LANE0, g_top / (1.0 + t2), g_top * t2 / (1.0 + t2)


def _out_router_kernel(oa_ref, ob_ref, x_ref, ga_ref, gb_ref, wo_ref, gf_ref, wr_ref, tri_ref,
                       x1_ref, h_ref, meta_ref, meta_t_ref, cnt_ref, carry_ref):
    @pl.when(pl.program_id(0) == 0)
    def _():
        carry_ref[...] = jnp.zeros_like(carry_ref)

    tm, d = x_ref.shape
    tok = d // (2 * LANES)
    logits = []
    for r0 in range(0, tm, ROUTE_BLOCK):
        rows = slice(r0, r0 + ROUTE_BLOCK)
        na = _rms_rows(oa_ref[rows, :].astype(F32)) * ga_ref[...]
        nb = _rms_rows(ob_ref[rows, :].astype(F32)) * gb_ref[...]
        mixed = jnp.concatenate([na, nb], axis=1).astype(BF16)
        x1 = x_ref[rows, :] + _dot(mixed, wo_ref[...])
        x1_ref[rows, :] = x1
        h = _rms_rows(x1) * gf_ref[...]
        _store_token_tiles(h_ref.at[r0 * tok:(r0 + ROUTE_BLOCK) * tok], _pack_bf16_pairs(h))

        h_hi, h_lo = _split2(h)
        parts = _dot(h_hi, wr_ref[...]) + _dot(h_lo, wr_ref[...])
        logits.append(parts[:, :LANES] + parts[:, LANES:])

    e0, e1, gate0, gate1 = _pick_experts(jnp.concatenate(logits, axis=0))

    lane = lax.broadcasted_iota(jnp.int32, (tm, LANES), 1)
    oh0 = lane == e0
    oh1 = lane == e1
    cum = _dot(tri_ref[...], jnp.concatenate([oh0.astype(BF16), oh1.astype(BF16)], axis=1))
    tot0 = jnp.sum(oh0.astype(F32), axis=0, keepdims=True)
    tot1 = jnp.sum(oh1.astype(F32), axis=0, keepdims=True)
    base = carry_ref[...]
    rank0 = jnp.sum(jnp.where(oh0, cum[:, :LANES] + base, 0.0), axis=1, keepdims=True)
    rank1 = jnp.sum(jnp.where(oh1, cum[:, LANES:] + (base + tot0), 0.0), axis=1, keepdims=True)
    base = base + tot0 + tot1
    carry_ref[...] = base
    cnt_ref[...] = base

    meta = jnp.zeros((tm, LANES), F32)
    for ln, val in ((META_E0, e0.astype(F32)), (META_E1, e1.astype(F32)), (META_G0, gate0),
                    (META_G1, gate1), (META_R0, rank0), (META_R1, rank1)):
        meta = jnp.where(lane == ln, val, meta)
    meta_ref[...] = meta
    meta_t_ref[...] = meta.T[:SUBLANES, :]


def _out_router(oa, ob, x2d, ga, gb, wo_bf16, gf, wr, tri):
    t, d = x2d.shape
    tm = ROW_TILE
    tok = d // (2 * LANES)
    row = lambda i: (i, 0)
    fixed = lambda i: (0, 0)
    return pl.pallas_call(
        _out_router_kernel,
        grid=(t // tm,),
        in_specs=[
            pl.BlockSpec((tm, MIX), row),
            pl.BlockSpec((tm, MIX), row),
            pl.BlockSpec((tm, d), row),
            pl.BlockSpec((1, MIX), fixed),
            pl.BlockSpec((1, MIX), fixed),
            pl.BlockSpec((2 * MIX, d), fixed),
            pl.BlockSpec((1, d), fixed),
            pl.BlockSpec((d, 2 * LANES), fixed),
            pl.BlockSpec((tm, tm), fixed),
        ],
        out_specs=[
            pl.BlockSpec((tm, d), row),
            pl.BlockSpec((tm * tok, LANES), row),
            pl.BlockSpec((tm, LANES), row),
            pl.BlockSpec((None, SUBLANES, tm), lambda i: (i, 0, 0)),
            pl.BlockSpec((1, LANES), fixed),
        ],
        out_shape=[
            jax.ShapeDtypeStruct((t, d), F32),
            jax.ShapeDtypeStruct((t * tok, LANES), jnp.uint32),
            jax.ShapeDtypeStruct((t, LANES), F32),
            jax.ShapeDtypeStruct((t // tm, SUBLANES, tm), F32),
            jax.ShapeDtypeStruct((1, LANES), F32),
        ],
        scratch_shapes=[pltpu.VMEM((1, LANES), F32)],
        compiler_params=pltpu.CompilerParams(
            dimension_semantics=("arbitrary",), vmem_limit_bytes=VMEM_LIMIT),
        name="out_router",
    )(oa, ob, x2d, ga, gb, wo_bf16, gf, wr, tri)


def _token_copy(src, i, dst, j, tok, sem):
    return pltpu.make_async_copy(src.at[pl.ds(pl.multiple_of(i * tok, tok), tok)],
                                 dst.at[pl.ds(pl.multiple_of(j * tok, tok), tok)], sem)


def _dispatch_kernel(slot_ref, h_ref, xs_in_ref, xs_ref, sem, *, tok):
    del xs_in_ref
    tm = h_ref.shape[0] // tok

    def start(blk, c):
        for u in range(DMA_UNROLL):
            i = blk * DMA_UNROLL + u
            for k in range(TOP_K):
                _token_copy(h_ref, i, xs_ref, slot_ref[0, k * tm + i], tok, sem).start(priority=k)
        return c

    lax.fori_loop(0, tm // DMA_UNROLL, start, 0)
    for k in range(TOP_K):
        pltpu.make_async_copy(h_ref, xs_ref.at[pl.ds(0, tm * tok)], sem).wait()


def _dispatch(slots, h_tok, xs_buf, tok):
    tm = ROW_TILE
    n_tiles = h_tok.shape[0] // (tm * tok)
    return pl.pallas_call(
        functools.partial(_dispatch_kernel, tok=tok),
        grid=(n_tiles,),
        in_specs=[
            pl.BlockSpec((None, 1, TOP_K * tm), lambda i: (i, 0, 0), memory_space=pltpu.SMEM),
            pl.BlockSpec((tm * tok, LANES), lambda i: (i, 0)),
            pl.BlockSpec(memory_space=pl.ANY),
        ],
        out_specs=pl.BlockSpec(memory_space=pl.ANY),
        out_shape=jax.ShapeDtypeStruct(xs_buf.shape, xs_buf.dtype),
        scratch_shapes=[pltpu.SemaphoreType.DMA(())],
        input_output_aliases={2: 0},
        compiler_params=pltpu.CompilerParams(
            dimension_semantics=("arbitrary",), vmem_limit_bytes=VMEM_LIMIT),
        name="dispatch",
    )(slots, h_tok, xs_buf)


def _expert_kernel(te_ref, nv_ref, xs_ref, wgu_ref, wdn_ref, ys_ref, wgu_bf, wdn_bf):
    i = pl.program_id(0)
    d = wgu_ref.shape[0]
    tok = d // (2 * LANES)
    tg = xs_ref.shape[0] // tok

    @pl.when((i == 0) | (te_ref[i] != te_ref[jnp.maximum(i - 1, 0)]))
    def _():
        wgu_bf[...] = wgu_ref[...].astype(BF16)
        wdn_bf[...] = wdn_ref[...].astype(BF16)

    @pl.when(i < nv_ref[0])
    def _():
        xs = _unpack_bf16_pairs(_load_token_tiles(xs_ref, tg, tok))
        gu = _dot(xs.astype(BF16), wgu_bf[...])
        g = gu[:, :D_EXPERT]
        a = (g * (1.0 / (1.0 + jnp.exp(-g)))) * gu[:, D_EXPERT:]
        _store_token_tiles(ys_ref, _pack_bf16_pairs(_dot(a.astype(BF16), wdn_bf[...])))

    @pl.when(i >= nv_ref[0])
    def _():
        ys_ref[...] = jnp.zeros_like(ys_ref)


def _experts(tile_expert, n_valid, xs, wgu, wdn, layer):
    d = wgu.shape[2]
    tok = d // (2 * LANES)
    tg = GROUP_TILE
    grid_spec = pltpu.PrefetchScalarGridSpec(
        num_scalar_prefetch=2,
        grid=(xs.shape[0] // (tg * tok),),
        in_specs=[
            pl.BlockSpec((tg * tok, LANES), lambda i, te, nv: (i, 0)),
            pl.BlockSpec((None, None, d, 2 * D_EXPERT), lambda i, te, nv: (layer, te[i], 0, 0)),
            pl.BlockSpec((None, None, D_EXPERT, d), lambda i, te, nv: (layer, te[i], 0, 0)),
        ],
        out_specs=pl.BlockSpec((tg * tok, LANES), lambda i, te, nv: (i, 0)),
        scratch_shapes=[pltpu.VMEM((d, 2 * D_EXPERT), BF16), pltpu.VMEM((D_EXPERT, d), BF16)],
    )
    return pl.pallas_call(
        _expert_kernel,
        grid_spec=grid_spec,
        out_shape=jax.ShapeDtypeStruct(xs.shape, xs.dtype),
        compiler_params=pltpu.CompilerParams(
            dimension_semantics=("arbitrary",), vmem_limit_bytes=VMEM_LIMIT),
        name="experts",
    )(tile_expert, n_valid, xs, wgu, wdn)


def _combine_kernel(slot_ref, slot_next_ref, x_ref, meta_ref, ys_ref, out_ref, buf, sems):
    step = pl.program_id(0)
    tm, d = x_ref.shape
    tok = d // (2 * LANES)

    def gather(slots, half):
        def start(blk, c):
            for u in range(DMA_UNROLL):
                i = blk * DMA_UNROLL + u
                for k in range(TOP_K):
                    _token_copy(ys_ref, slots[0, k * tm + i], buf.at[half, k], i, tok,
                                sems.at[half]).start(priority=k)
            return c

        lax.fori_loop(0, tm // DMA_UNROLL, start, 0)

    cur = step % 2

    @pl.when(step == 0)
    def _():
        gather(slot_ref, 0)

    @pl.when(step + 1 < pl.num_programs(0))
    def _():
        gather(slot_next_ref, 1 - cur)

    for k in range(TOP_K):
        pltpu.make_async_copy(ys_ref.at[pl.ds(0, tm * tok)], buf.at[cur, k], sems.at[cur]).wait()
    meta = meta_ref[...]
    lane = lax.broadcasted_iota(jnp.int32, (1, LANES), 1)
    g0 = _lane_col(meta, lane, META_G0, 0.0, jnp.sum)
    g1 = _lane_col(meta, lane, META_G1, 0.0, jnp.sum)
    y = (g0 * _unpack_bf16_pairs(_load_token_tiles(buf.at[cur, 0], tm, tok))
         + g1 * _unpack_bf16_pairs(_load_token_tiles(buf.at[cur, 1], tm, tok)))
    out_ref[...] = x_ref[...] + y


def _combine(slots, x1, meta, ys):
    t, d = x1.shape
    tm = ROW_TILE
    tok = d // (2 * LANES)
    n_tiles = t // tm
    return pl.pallas_call(
        _combine_kernel,
        grid=(n_tiles,),
        in_specs=[
            pl.BlockSpec((None, 1, TOP_K * tm), lambda i: (i, 0, 0), memory_space=pltpu.SMEM),
            pl.BlockSpec((None, 1, TOP_K * tm), lambda i: (jnp.minimum(i + 1, n_tiles - 1), 0, 0),
                         memory_space=pltpu.SMEM),
            pl.BlockSpec((tm, d), lambda i: (i, 0)),
            pl.BlockSpec((tm, LANES), lambda i: (i, 0)),
            pl.BlockSpec(memory_space=pl.ANY),
        ],
        out_specs=pl.BlockSpec((tm, d), lambda i: (i, 0)),
        out_shape=jax.ShapeDtypeStruct((t, d), F32),
        scratch_shapes=[pltpu.VMEM((2, TOP_K, tm * tok, LANES), ys.dtype),
                        pltpu.SemaphoreType.DMA((2,))],
        compiler_params=pltpu.CompilerParams(
            dimension_semantics=("arbitrary",), vmem_limit_bytes=VMEM_LIMIT),
        name="combine",
    )(slots, slots, x1, meta, ys)


def _rope_expand_matrix():
    src = jnp.arange(LANES)[:, None]
    dst = jnp.arange(LANES)[None, :]
    in_head = dst % HEAD_DIM
    cos_m = (src < ROPE_HALF) & (in_head < ROPE_DIM) & (in_head % ROPE_HALF == src)
    is_sin = (src >= ROPE_HALF) & (src < ROPE_DIM)
    sin_lo = is_sin & (in_head >= ROPE_HALF) & (in_head < ROPE_DIM) & (in_head - ROPE_HALF == src - ROPE_HALF)
    sin_hi = is_sin & (in_head < ROPE_HALF) & (in_head == src - ROPE_HALF)
    e = jnp.concatenate([cos_m.astype(F32), sin_lo.astype(F32), -sin_hi.astype(F32)], axis=1)
    return jnp.concatenate([e, e, e], axis=0).astype(BF16)


def _block_diag_ones():
    i = jnp.arange(2 * LANES)
    return (i[:, None] // HEAD_DIM == i[None, :] // HEAD_DIM).astype(BF16)


def _tile4(g):
    return jnp.tile(g.astype(F32), 2 * LANES // HEAD_DIM)


_B_HEAD_ORDER = tuple(h for p in range(N_HEADS // 2) for h in (p, p + N_HEADS // 2))


def _b_cols():
    order = jnp.asarray(_B_HEAD_ORDER)
    return (order[:, None] * HEAD_DIM + jnp.arange(HEAD_DIM)[None, :]).reshape(-1)


def kernel(x, positions, attn_norm, w_in, q_norm_a, k_norm_a, q_norm_b, k_norm_b, sinks_b,
           out_norm_a, out_norm_b, w_out, ffn_norm, w_router_group, w_router_expert,
           w_gate_up, w_down):
    b, s, d = x.shape
    t = b * s
    depth = w_in.shape[0]
    tok = d // (2 * LANES)
    assert s % CHUNK == 0 and s % ROW_TILE == 0 and s % SWA_STEP == 0
    assert ROW_TILE % (2 * SUBLANES * max(DILATIONS)) == 0 and ROW_TILE % ROUTE_BLOCK == 0 and ROUTE_BLOCK % ROUTE_ROWS == 0
    assert d == 2 * MIX and d % (2 * LANES) == 0

    inv_freq = ROPE_THETA ** (-jnp.arange(0, ROPE_DIM, 2, dtype=F32) / ROPE_DIM)
    ang = positions.astype(F32)[None, :, :] * inv_freq[:, None, None]
    cos_sin = jnp.moveaxis(jnp.concatenate([jnp.cos(ang), jnp.sin(ang)], axis=0), 0, -1)
    rope_tab = jnp.pad(cos_sin, ((0, 0), (0, 0), (0, LANES - ROPE_DIM))).reshape(t, LANES)
    e3 = _rope_expand_matrix()
    bd = _block_diag_ones()
    row_i = jnp.arange(ROW_TILE)
    tri = (row_i[None, :] < row_i[:, None]).astype(BF16)
    bcols = _b_cols()
    q_scale = HEAD_DIM ** -0.5 * LOG2E
    n_sorted = t * TOP_K + N_EXPERTS * GROUP_TILE
    n_tiles = n_sorted // GROUP_TILE
    xs_buf = jnp.zeros((n_sorted * tok, LANES), jnp.uint32)
    nb = s // BLOCK

    x2d = x.reshape(t, d)
    for l in range(depth):
        qb0 = 3 * MIX
        w_l = w_in[l]
        w_l = jnp.concatenate([w_l[:, :qb0], w_l[:, qb0:qb0 + MIX][:, bcols], w_l[:, qb0 + MIX:]], axis=1)
        head_gains = jnp.stack([_tile4(q_norm_a[l]) * q_scale, _tile4(k_norm_a[l]),
                                _tile4(q_norm_b[l]) * q_scale, _tile4(k_norm_b[l])])
        sink = sinks_b[l][jnp.asarray(_B_HEAD_ORDER)].astype(F32) * LOG2E
        gb_perm = out_norm_b[l][bcols]
        wo_l = jnp.concatenate([w_out[l][:MIX], w_out[l][MIX:][bcols]], axis=0).astype(BF16)
        wr = jnp.concatenate(
            [w_router_group[l], jnp.zeros((d, EXPERT_LANE0 - N_GROUPS), F32), w_router_expert[l],
             jnp.zeros((d, LANES - EXPERT_LANE0 - N_EXPERTS), F32)], axis=1)
        wr_hi = wr.astype(BF16)
        wr_split = jnp.concatenate([wr_hi, (wr - wr_hi.astype(F32)).astype(BF16)], axis=1)

        (qa, ka, va, qa4, ka4, va4, qa16, ka16, va16, qb, kb, vb) = _proj(
            x2d, b, attn_norm[l].reshape(1, d), w_l.astype(BF16), rope_tab, e3, bd, head_gains)
        blocks = lambda a: a.reshape(b, nb, BLOCK, a.shape[-1])
        class_blocks = lambda a: a.reshape(b, a.shape[1], a.shape[2] // BLOCK, BLOCK, MIX)
        oa = _dilated_attention(tuple(map(blocks, (qa, ka, va))),
                                tuple(map(class_blocks, (qa4, ka4, va4))),
                                tuple(map(class_blocks, (qa16, ka16, va16))))
        ob = _swa_attention(blocks(qb), blocks(kb), blocks(vb), sink)

        x1, h_tok, meta, meta_t, counts = _out_router(
            oa.reshape(t, MIX), ob.reshape(t, MIX), x2d, out_norm_a[l].reshape(1, MIX),
            gb_perm.reshape(1, MIX), wo_l, ffn_norm[l].reshape(1, d), wr_split, tri)

        cnt = counts[0, :N_EXPERTS].astype(jnp.int32)
        padded = ((cnt + GROUP_TILE - 1) // GROUP_TILE) * GROUP_TILE
        ends = jnp.cumsum(padded)
        offs = ends - padded
        eid = meta_t[:, META_E0:META_E1 + 1, :].astype(jnp.int32)
        rank = meta_t[:, META_R0:META_R1 + 1, :].astype(jnp.int32)
        is_expert = eid[..., None] == jnp.arange(N_EXPERTS, dtype=jnp.int32)
        slots = jnp.sum(jnp.where(is_expert, offs, 0), axis=-1) + rank
        slots = slots.reshape(t // ROW_TILE, 1, TOP_K * ROW_TILE)
        tile_start = jnp.arange(n_tiles, dtype=jnp.int32) * GROUP_TILE
        tile_expert = jnp.minimum(
            jnp.sum((tile_start[:, None] >= ends[None, :]).astype(jnp.int32), axis=1), N_EXPERTS - 1)
        n_valid = (ends[-1:] // GROUP_TILE).astype(jnp.int32)

        xs_buf = _dispatch(slots, h_tok, xs_buf, tok)
        ys = _experts(tile_expert, n_valid, xs_buf, w_gate_up, w_down, l)
        x2d = _combine(slots, x1, meta, ys)
    return x2d.reshape(b, s, d)
```

```python
import functools
import math

import jax
import jax.numpy as jnp
from jax import lax
from jax.experimental import pallas as pl
from jax.experimental.pallas import tpu as pltpu

F32 = jnp.float32
BF16 = jnp.bfloat16

HEAD_DIM = 64
N_HEADS = 8
MIX = N_HEADS * HEAD_DIM
N_KV_B = 2
KV_B = N_KV_B * HEAD_DIM
DILATIONS = (1, 4, 16)
BLOCK = 128
CHUNK = BLOCK * max(DILATIONS)
SWA_BACK = 127
SWA_STEP = 8 * BLOCK
ROPE_DIM = HEAD_DIM // 4
ROPE_HALF = ROPE_DIM // 2
ROPE_THETA = 500000.0
N_GROUPS = 4
EXPERTS_PER_GROUP = 8
N_EXPERTS = N_GROUPS * EXPERTS_PER_GROUP
TOP_K = 2
D_EXPERT = 512
EPS = 1e-6
LOG2E = math.log2(math.e)

LANES = 128
SUBLANES = 8
PAIR = 2 * HEAD_DIM
HALF_W = MIX // 2
EXPERT_LANE0 = 32
ROW_TILE = 512
ROUTE_BLOCK = 256
ROUTE_ROWS = 128
GROUP_TILE = 512
DMA_UNROLL = 8
VMEM_LIMIT = 48 * 1024 * 1024

META_E0, META_E1, META_G0, META_G1, META_R0, META_R1 = 0, 1, 2, 3, 4, 5


def _split2(a):
    hi = a.astype(BF16)
    lo = (a - hi.astype(F32)).astype(BF16)
    return hi, lo


def _split3(a):
    hi = a.astype(BF16)
    r = a - hi.astype(F32)
    mid = r.astype(BF16)
    lo = (r - mid.astype(F32)).astype(BF16)
    return hi, mid, lo


def _dot(a, b):
    return jnp.dot(a, b, preferred_element_type=F32)


def _rms_rows(a):
    return a * lax.rsqrt(jnp.mean(a * a, axis=-1, keepdims=True) + EPS)


def _lane_col(tile, lane, idx, fill, reduce):
    return reduce(jnp.where(lane == idx, tile, fill), axis=1, keepdims=True)


def _pack_bf16_pairs(val):
    w = val.shape[1] // 2
    bits = lax.bitcast_convert_type(val, jnp.uint32) + jnp.uint32(0x8000)
    return (bits[:, :w] >> 16) | (bits[:, w:] & jnp.uint32(0xFFFF0000))


def _unpack_bf16_pairs(words):
    lo = lax.bitcast_convert_type(words << 16, F32)
    hi = lax.bitcast_convert_type(words & jnp.uint32(0xFFFF0000), F32)
    return jnp.concatenate([lo, hi], axis=1)


def _store_token_tiles(ref, val):
    rows, width = val.shape
    n = width // LANES
    for c in range(n):
        ref[pl.ds(c, rows, stride=n), :] = val[:, c * LANES:(c + 1) * LANES]


def _load_token_tiles(ref, rows, n):
    return jnp.concatenate([ref[pl.ds(c, rows, stride=n), :] for c in range(n)], axis=1)


def _proj_kernel(x_ref, g_ref, w_ref, rope_ref, e3_ref, bd_ref, hg_ref,
                 qa_ref, ka_ref, va_ref, qa4_ref, ka4_ref, va4_ref, qa16_ref, ka16_ref, va16_ref,
                 qb_ref, kb_ref, vb_ref, scr, scr4):
    h = _rms_rows(x_ref[...]) * g_ref[...]
    acc = _dot(h.astype(BF16), w_ref[...])
    tm = acc.shape[0]

    hi, mid, lo = _split3(rope_ref[...])
    tab = _dot(jnp.concatenate([hi, mid, lo], axis=1), e3_ref[...])
    lane = lax.broadcasted_iota(jnp.int32, (1, LANES), 1)
    not_rope = ((lane & (HEAD_DIM - 1)) >= ROPE_DIM).astype(F32)
    c = tab[:, :LANES] + not_rope
    s_lo = tab[:, LANES:2 * LANES]
    s_hi = tab[:, 2 * LANES:]
    bd = bd_ref[...]

    def norm_rope(a, gain):
        w = a.shape[1]
        a2_hi, a2_lo = _split2(a * a)
        ss = _dot(a2_hi, bd[:w, :w]) + _dot(a2_lo, bd[:w, :w])
        y = (a * lax.rsqrt(ss * (1.0 / HEAD_DIM) + EPS)) * gain
        outs = []
        for j in range(w // LANES):
            yj = y[:, j * LANES:(j + 1) * LANES]
            outs.append(yj * c + pltpu.roll(yj, ROPE_HALF, 1) * s_lo
                        + pltpu.roll(yj, LANES - ROPE_HALF, 1) * s_hi)
        return outs[0] if len(outs) == 1 else jnp.concatenate(outs, axis=1)

    def normed(col0, width, gain_row):
        outs = []
        for c0 in range(col0, col0 + width, 2 * LANES):
            wd = min(2 * LANES, col0 + width - c0)
            outs.append(norm_rope(acc[:, c0:c0 + wd], hg_ref[gain_row:gain_row + 1, :wd]))
        return outs[0] if len(outs) == 1 else jnp.concatenate(outs, axis=1)

    def emit_by_class(val, nat_ref, ref4, ref16):
        n_ct = MIX // LANES
        q4 = tm // 4
        nat_ref[...] = val.astype(BF16)
        for ct in range(n_ct):
            scr[ct] = val[:, ct * LANES:(ct + 1) * LANES]
        for r4 in range(4):
            rows = [scr[ct, pl.ds(r4, q4, stride=4), :] for ct in range(n_ct)]
            ref4[r4] = jnp.concatenate(rows, axis=1).astype(BF16)
            for ct in range(n_ct):
                scr4[ct, r4 * q4:(r4 + 1) * q4, :] = rows[ct]
        for r16 in range(16):
            first = (r16 % 4) * q4 + r16 // 4
            rows = [scr4[ct, pl.ds(first, tm // 16, stride=4), :] for ct in range(n_ct)]
            ref16[r16] = jnp.concatenate(rows, axis=1).astype(BF16)

    emit_by_class(normed(0, MIX, 0), qa_ref, qa4_ref, qa16_ref)
    emit_by_class(normed(MIX, MIX, 1), ka_ref, ka4_ref, ka16_ref)
    emit_by_class(acc[:, 2 * MIX:3 * MIX], va_ref, va4_ref, va16_ref)
    qb_ref[...] = normed(3 * MIX, MIX, 2).astype(BF16)
    kb_ref[...] = normed(4 * MIX, KV_B, 3).astype(BF16)
    vb_ref[...] = acc[:, 4 * MIX + KV_B:].astype(BF16)


def _proj(x2d, b, gain, w_bf16, rope_tab, e3, bd, head_gains):
    t, d = x2d.shape
    s = t // b
    pw = w_bf16.shape[1]
    tm = ROW_TILE
    nt = s // tm
    row = lambda bi, i: (bi * nt + i, 0)
    fixed = lambda bi, i: (0, 0)
    by_class = lambda bi, i: (bi, 0, i, 0)
    nat = lambda w: (jax.ShapeDtypeStruct((t, w), BF16), pl.BlockSpec((tm, w), row))
    grouped = lambda dil: (jax.ShapeDtypeStruct((b, dil, s // dil, MIX), BF16),
                           pl.BlockSpec((None, dil, tm // dil, MIX), by_class))
    outs = [nat(MIX)] * 3 + [grouped(4)] * 3 + [grouped(16)] * 3 + [nat(MIX), nat(KV_B), nat(KV_B)]
    return pl.pallas_call(
        _proj_kernel,
        grid=(b, nt),
        in_specs=[
            pl.BlockSpec((tm, d), row),
            pl.BlockSpec((1, d), fixed),
            pl.BlockSpec((d, pw), fixed),
            pl.BlockSpec((tm, LANES), row),
            pl.BlockSpec(e3.shape, fixed),
            pl.BlockSpec(bd.shape, fixed),
            pl.BlockSpec(head_gains.shape, fixed),
        ],
        out_specs=[o[1] for o in outs],
        out_shape=[o[0] for o in outs],
        scratch_shapes=[pltpu.VMEM((MIX // LANES, tm, LANES), F32)] * 2,
        compiler_params=pltpu.CompilerParams(
            dimension_semantics=("arbitrary", "arbitrary"), vmem_limit_bytes=VMEM_LIMIT),
        name="proj",
    )(x2d, gain, w_bf16, rope_tab, e3, bd, head_gains)


def _band(min_back, first_key):
    qi = lax.broadcasted_iota(jnp.int32, (BLOCK, 2 * BLOCK), 0)
    kj = lax.broadcasted_iota(jnp.int32, (BLOCK, 2 * BLOCK), 1)
    valid = (kj >= qi + min_back) & (kj <= qi + BLOCK)
    return valid if first_key is None else valid & (kj >= first_key)


def _head_scores(q_pair, kcat, sel, valid):
    qm = jnp.where(sel, q_pair, jnp.zeros_like(q_pair))
    s = lax.dot_general(qm, kcat, (((1,), (1,)), ((), ())), preferred_element_type=F32)
    return jnp.where(valid, s, -jnp.inf)


def _weighted_values(pr, v_pair, sel):
    return _dot(pr.astype(BF16), jnp.where(sel, v_pair, jnp.ones_like(v_pair)))


def _normalise_pair(res_even, res_odd, even, extra=None):
    num = jnp.where(even, res_even, res_odd)
    den = pltpu.roll(jnp.where(even, res_odd, res_even), HEAD_DIM, 1)
    return num / (den if extra is None else den + extra)


def _with_halo(prev, cur):
    return jnp.concatenate([prev, cur], axis=0)


def _swa_kernel(q_ref, kp_ref, k_ref, vp_ref, v_ref, sink_ref, o_ref):
    first_key = jnp.where(pl.program_id(1) > 0, 0, BLOCK)
    band = _band(BLOCK - SWA_BACK, None)
    band_first = _band(BLOCK - SWA_BACK, first_key)
    lane = lax.broadcasted_iota(jnp.int32, (1, LANES), 1)
    even = lane < HEAD_DIM
    for j in range(SWA_STEP // BLOCK):
        valid = band_first if j == 0 else band
        kcat = _with_halo(kp_ref[...] if j == 0 else k_ref[j - 1], k_ref[j])
        vcat = _with_halo(vp_ref[...] if j == 0 else v_ref[j - 1], v_ref[j])
        for p in range(N_HEADS // 2):
            cols = slice(p * PAIR, (p + 1) * PAIR)
            q_pair = q_ref[j, :, cols]
            res, sink_term = [], []
            for half in range(2):
                sink = sink_ref[2 * p + half]
                sel = even if half == 0 else jnp.logical_not(even)
                s = _head_scores(q_pair, kcat, sel, valid)
                m = jnp.maximum(jnp.max(s, axis=1, keepdims=True), sink)
                res.append(_weighted_values(jnp.exp2(s - m), vcat, sel))
                sink_term.append(jnp.exp2(sink - m))
            extra = jnp.where(even, sink_term[0], sink_term[1])
            o_ref[j, :, cols] = _normalise_pair(res[0], res[1], even, extra).astype(o_ref.dtype)


def _swa_attention(q, k, v, sink):
    b, nb = q.shape[:2]
    step = SWA_STEP // BLOCK
    cur = lambda w: pl.BlockSpec((None, step, BLOCK, w), lambda bi, c: (bi, c, 0, 0))
    prev = pl.BlockSpec((None, None, BLOCK, KV_B),
                        lambda bi, c: (bi, jnp.maximum(step * c - 1, 0), 0, 0))
    return pl.pallas_call(
        _swa_kernel,
        grid=(b, nb // step),
        in_specs=[cur(MIX), prev, cur(KV_B), prev, cur(KV_B), pl.BlockSpec(memory_space=pltpu.SMEM)],
        out_specs=cur(MIX),
        out_shape=jax.ShapeDtypeStruct(q.shape, BF16),
        compiler_params=pltpu.CompilerParams(
            dimension_semantics=("arbitrary", "arbitrary"), vmem_limit_bytes=VMEM_LIMIT),
        name="swa",
    )(q, k, k, v, v, sink)


def _dilated_kernel(qn, knp, kn, vnp, vn, q4, k4p, k4, v4p, v4, q16, k16p, k16, v16p, v16,
                    o_ref, m_st, r_st):
    n_pairs = HALF_W // PAIR
    first_key = jnp.where(pl.program_id(1) > 0, 0, BLOCK)
    band = _band(0, None)
    band_first = _band(0, first_key)
    lane = lax.broadcasted_iota(jnp.int32, (1, LANES), 1)
    even = lane < HEAD_DIM

    def tile(q, kcat, vcat, valid, row0, stride, init):
        rows = pl.ds(row0, BLOCK) if stride == 1 else pl.ds(row0, BLOCK, stride=stride)
        for p in range(n_pairs):
            cols = slice(p * PAIR, (p + 1) * PAIR)
            for half in range(2):
                h = 2 * p + half
                sel = even if half == 0 else jnp.logical_not(even)
                s = _head_scores(q[:, cols], kcat[:, cols], sel, valid)
                m_tile = jnp.max(s, axis=1, keepdims=True)
                if init:
                    m_new = jnp.broadcast_to(m_tile, (BLOCK, LANES))
                    res = _weighted_values(jnp.exp2(s - m_tile), vcat[:, cols], sel)
                else:
                    m_old = m_st[h, rows, :]
                    m_new = jnp.maximum(m_old, m_tile)
                    pr = jnp.exp2(s - jnp.concatenate([m_new, m_new], axis=1))
                    res = (_weighted_values(pr, vcat[:, cols], sel)
                           + jnp.exp2(m_old - m_new) * r_st[h, rows, :])
                m_st[h, rows, :] = m_new
                r_st[h, rows, :] = res

    def d16(r, carry):
        tile(q16[r], _with_halo(k16p[r], k16[r]), _with_halo(v16p[r], v16[r]), band_first, r, 16, True)
        return carry

    lax.fori_loop(0, 16, d16, 0, unroll=True)

    tile(qn[0], _with_halo(knp[...], kn[0]), _with_halo(vnp[...], vn[0]), band_first, 0, 1, False)

    def d1(j, carry):
        tile(qn[j], _with_halo(kn[j - 1], kn[j]), _with_halo(vn[j - 1], vn[j]),
             band, pl.multiple_of(j * BLOCK, BLOCK), 1, False)
        return carry

    lax.fori_loop(1, CHUNK // BLOCK, d1, 0, unroll=True)

    def d4(r, carry):
        tile(q4[r, 0], _with_halo(k4p[r], k4[r, 0]), _with_halo(v4p[r], v4[r, 0]), band_first, r, 4, False)
        for j in range(1, CHUNK // (4 * BLOCK)):
            tile(q4[r, j], _with_halo(k4[r, j - 1], k4[r, j]), _with_halo(v4[r, j - 1], v4[r, j]),
                 band, r + 4 * j * BLOCK, 4, False)
        return carry

    lax.fori_loop(0, 4, d4, 0, unroll=True)

    def finish(j, carry):
        rows = pl.ds(pl.multiple_of(j * BLOCK, BLOCK), BLOCK)
        for p in range(n_pairs):
            out = _normalise_pair(r_st[2 * p, rows, :], r_st[2 * p + 1, rows, :], even)
            o_ref[j, :, p * PAIR:(p + 1) * PAIR] = out.astype(o_ref.dtype)
        return carry

    lax.fori_loop(0, CHUNK // BLOCK, finish, 0, unroll=4)


def _dilated_attention(nat, by4, by16):
    b, nb = nat[0].shape[:2]
    per_chunk = CHUNK // BLOCK
    n_chunks = nb // per_chunk

    def specs(dil):
        step = per_chunk // dil
        if dil == 1:
            cur = pl.BlockSpec((None, step, BLOCK, HALF_W), lambda bi, c, hf: (bi, c, 0, hf))
            prev = pl.BlockSpec((None, None, BLOCK, HALF_W),
                                lambda bi, c, hf: (bi, jnp.maximum(step * c - 1, 0), 0, hf))
        else:
            blocks = None if step == 1 else step
            cur = pl.BlockSpec((None, dil, blocks, BLOCK, HALF_W), lambda bi, c, hf: (bi, 0, c, 0, hf))
            prev = pl.BlockSpec((None, dil, None, BLOCK, HALF_W),
                                lambda bi, c, hf: (bi, 0, jnp.maximum(step * c - 1, 0), 0, hf))
        return cur, prev

    args, in_specs = [], []
    for (q, k, v), dil in ((nat, 1), (by4, 4), (by16, 16)):
        cur, prev = specs(dil)
        args += [q, k, k, v, v]
        in_specs += [cur, prev, cur, prev, cur]
    return pl.pallas_call(
        _dilated_kernel,
        grid=(b, n_chunks, MIX // HALF_W),
        in_specs=in_specs,
        out_specs=specs(1)[0],
        out_shape=jax.ShapeDtypeStruct(nat[0].shape, BF16),
        scratch_shapes=[pltpu.VMEM((HALF_W // HEAD_DIM, CHUNK, LANES), F32),
                        pltpu.VMEM((HALF_W // HEAD_DIM, CHUNK, LANES), F32)],
        compiler_params=pltpu.CompilerParams(
            dimension_semantics=("arbitrary", "arbitrary", "arbitrary"),
            vmem_limit_bytes=VMEM_LIMIT),
        name="dilated",
    )(*args)


def _route(lt, tri, base):
    n = lt.shape[1]
    row8 = lax.broadcasted_iota(jnp.int32, (SUBLANES, n), 0)
    far = jnp.int32(LANES)
    neg = -jnp.inf

    def first_argmax(vals):
        mx = jnp.max(vals, axis=0, keepdims=True)
        idx = jnp.min(jnp.where(vals == mx, row8, far), axis=0, keepdims=True)
        return mx, idx

    gl = jnp.where(row8 < N_GROUPS, lt[:SUBLANES, :], neg)
    gmax, gidx = first_argmax(gl)
    g_top = 1.0 / jnp.sum(jnp.exp(gl - gmax), axis=0, keepdims=True)

    el = lt[EXPERT_LANE0:EXPERT_LANE0 + EXPERTS_PER_GROUP, :]
    for g in range(1, N_GROUPS):
        r0 = EXPERT_LANE0 + g * EXPERTS_PER_GROUP
        el = jnp.where(gidx == g, lt[r0:r0 + EXPERTS_PER_GROUP, :], el)
    m1, i1 = first_argmax(el)
    m2, i2 = first_argmax(jnp.where(row8 == i1, neg, el))
    t2 = jnp.exp(m2 - m1)
    gate0 = g_top / (1.0 + t2)
    gate1 = g_top * t2 / (1.0 + t2)
    e0 = gidx * EXPERTS_PER_GROUP + i1
    e1 = gidx * EXPERTS_PER_GROUP + i2

    row_e = lax.broadcasted_iota(jnp.int32, (N_EXPERTS, n), 0)
    oh0 = row_e == e0
    oh1 = row_e == e1
    cum = _dot(jnp.concatenate([oh0.astype(BF16), oh1.astype(BF16)], axis=0), tri)
    oh0 = oh0.astype(F32)
    oh1 = oh1.astype(F32)
    tot0 = jnp.sum(oh0, axis=1, keepdims=True)
    tot1 = jnp.sum(oh1, axis=1, keepdims=True)
    base_n = jnp.concatenate([base] * (n // LANES), axis=1)
    rank0 = jnp.sum(oh0 * (cum[:N_EXPERTS] + base_n), axis=0, keepdims=True)
    rank1 = jnp.sum(oh1 * (cum[N_EXPERTS:] + (base_n + tot0)), axis=0, keepdims=True)

    meta_t = jnp.zeros((SUBLANES, n), F32)
    for r, val in ((META_E0, e0.astype(F32)), (META_E1, e1.astype(F32)), (META_G0, gate0),
                   (META_G1, gate1), (META_R0, rank0), (META_R1, rank1)):
        meta_t = jnp.where(row8 == r, val, meta_t)
    return meta_t, base + (tot0 + tot1)


def _out_router_kernel(oa_ref, ob_ref, x_ref, ga_ref, gb_ref, wo_ref, gf_ref, wr_ref, tri_ref,
                       x1_ref, h_ref, meta_ref, meta_t_ref, cnt_ref, carry_ref):
    @pl.when(pl.program_id(0) == 0)
    def _():
        carry_ref[...] = jnp.zeros_like(carry_ref)

    tm, d = x_ref.shape
    tok = d // (2 * LANES)
    logits = []
    for r0 in range(0, tm, ROUTE_BLOCK):
        rows = slice(r0, r0 + ROUTE_BLOCK)
        na = _rms_rows(oa_ref[rows, :].astype(F32)) * ga_ref[...]
        nb = _rms_rows(ob_ref[rows, :].astype(F32)) * gb_ref[...]
        mixed = jnp.concatenate([na, nb], axis=1).astype(BF16)
        x1 = x_ref[rows, :] + _dot(mixed, wo_ref[...])
        x1_ref[rows, :] = x1
        h = _rms_rows(x1) * gf_ref[...]
        _store_token_tiles(h_ref.at[r0 * tok:(r0 + ROUTE_BLOCK) * tok], _pack_bf16_pairs(h))

        h_hi, h_lo = _split2(h)
        parts = _dot(h_hi, wr_ref[...]) + _dot(h_lo, wr_ref[...])
        logits.append(parts[:, :LANES] + parts[:, LANES:])

    lt = jnp.concatenate(logits, axis=0).T
    meta_t, base = _route(lt, tri_ref[...], carry_ref[...])
    carry_ref[...] = base
    cnt_ref[...] = base
    meta_t_ref[...] = meta_t
    meta_ref[...] = jnp.concatenate([meta_t, jnp.zeros((LANES - SUBLANES, tm), F32)], axis=0).T


def _out_router(oa, ob, x2d, ga, gb, wo_bf16, gf, wr, tri):
    t, d = x2d.shape
    tm = ROW_TILE
    tok = d // (2 * LANES)
    row = lambda i: (i, 0)
    fixed = lambda i: (0, 0)
    return pl.pallas_call(
        _out_router_kernel,
        grid=(t // tm,),
        in_specs=[
            pl.BlockSpec((tm, MIX), row),
            pl.BlockSpec((tm, MIX), row),
            pl.BlockSpec((tm, d), row),
            pl.BlockSpec((1, MIX), fixed),
            pl.BlockSpec((1, MIX), fixed),
            pl.BlockSpec((2 * MIX, d), fixed),
            pl.BlockSpec((1, d), fixed),
            pl.BlockSpec((d, 2 * LANES), fixed),
            pl.BlockSpec((tm, tm), fixed),
        ],
        out_specs=[
            pl.BlockSpec((tm, d), row),
            pl.BlockSpec((tm * tok, LANES), row),
            pl.BlockSpec((tm, LANES), row),
            pl.BlockSpec((None, SUBLANES, tm), lambda i: (i, 0, 0)),
            pl.BlockSpec((N_EXPERTS, LANES), fixed),
        ],
        out_shape=[
            jax.ShapeDtypeStruct((t, d), F32),
            jax.ShapeDtypeStruct((t * tok, LANES), jnp.uint32),
            jax.ShapeDtypeStruct((t, LANES), F32),
            jax.ShapeDtypeStruct((t // tm, SUBLANES, tm), F32),
            jax.ShapeDtypeStruct((N_EXPERTS, LANES), F32),
        ],
        scratch_shapes=[pltpu.VMEM((N_EXPERTS, LANES), F32)],
        compiler_params=pltpu.CompilerParams(
            dimension_semantics=("arbitrary",), vmem_limit_bytes=VMEM_LIMIT),
        name="out_router",
    )(oa, ob, x2d, ga, gb, wo_bf16, gf, wr, tri)


def _token_copy(src, i, dst, j, tok, sem):
    return pltpu.make_async_copy(src.at[pl.ds(pl.multiple_of(i * tok, tok), tok)],
                                 dst.at[pl.ds(pl.multiple_of(j * tok, tok), tok)], sem)


def _dispatch_kernel(slot_ref, h_ref, xs_in_ref, xs_ref, sem, *, tok):
    del xs_in_ref
    tm = h_ref.shape[0] // tok

    def start(blk, c):
        for u in range(DMA_UNROLL):
            i = blk * DMA_UNROLL + u
            for k in range(TOP_K):
                _token_copy(h_ref, i, xs_ref, slot_ref[0, k * tm + i], tok, sem).start(priority=k)
        return c

    lax.fori_loop(0, tm // DMA_UNROLL, start, 0)
    for k in range(TOP_K):
        pltpu.make_async_copy(h_ref, xs_ref.at[pl.ds(0, tm * tok)], sem).wait()


def _dispatch(slots, h_tok, xs_buf, tok):
    tm = ROW_TILE
    n_tiles = h_tok.shape[0] // (tm * tok)
    return pl.pallas_call(
        functools.partial(_dispatch_kernel, tok=tok),
        grid=(n_tiles,),
        in_specs=[
            pl.BlockSpec((None, 1, TOP_K * tm), lambda i: (i, 0, 0), memory_space=pltpu.SMEM),
            pl.BlockSpec((tm * tok, LANES), lambda i: (i, 0)),
            pl.BlockSpec(memory_space=pl.ANY),
        ],
        out_specs=pl.BlockSpec(memory_space=pl.ANY),
        out_shape=jax.ShapeDtypeStruct(xs_buf.shape, xs_buf.dtype),
        scratch_shapes=[pltpu.SemaphoreType.DMA(())],
        input_output_aliases={2: 0},
        compiler_params=pltpu.CompilerParams(
            dimension_semantics=("arbitrary",), vmem_limit_bytes=VMEM_LIMIT),
        name="dispatch",
    )(slots, h_tok, xs_buf)


def _expert_kernel(te_ref, nv_ref, xs_ref, wgu_ref, wdn_ref, ys_ref, wgu_bf, wdn_bf):
    i = pl.program_id(0)
    d = wgu_ref.shape[0]
    tok = d // (2 * LANES)
    tg = xs_ref.shape[0] // tok

    @pl.when((i == 0) | (te_ref[i] != te_ref[jnp.maximum(i - 1, 0)]))
    def _():
        wgu_bf[...] = wgu_ref[...].astype(BF16)
        wdn_bf[...] = wdn_ref[...].astype(BF16)

    @pl.when(i < nv_ref[0])
    def _():
        xs = _unpack_bf16_pairs(_load_token_tiles(xs_ref, tg, tok))
        gu = _dot(xs.astype(BF16), wgu_bf[...])
        g = gu[:, :D_EXPERT]
        a = (g * (1.0 / (1.0 + jnp.exp(-g)))) * gu[:, D_EXPERT:]
        _store_token_tiles(ys_ref, _pack_bf16_pairs(_dot(a.astype(BF16), wdn_bf[...])))

    @pl.when(i >= nv_ref[0])
    def _():
        ys_ref[...] = jnp.zeros_like(ys_ref)


def _experts(tile_expert, n_valid, xs, wgu, wdn, layer):
    d = wgu.shape[2]
    tok = d // (2 * LANES)
    tg = GROUP_TILE
    grid_spec = pltpu.PrefetchScalarGridSpec(
        num_scalar_prefetch=2,
        grid=(xs.shape[0] // (tg * tok),),
        in_specs=[
            pl.BlockSpec((tg * tok, LANES), lambda i, te, nv: (i, 0)),
            pl.BlockSpec((None, None, d, 2 * D_EXPERT), lambda i, te, nv: (layer, te[i], 0, 0)),
            pl.BlockSpec((None, None, D_EXPERT, d), lambda i, te, nv: (layer, te[i], 0, 0)),
        ],
        out_specs=pl.BlockSpec((tg * tok, LANES), lambda i, te, nv: (i, 0)),
        scratch_shapes=[pltpu.VMEM((d, 2 * D_EXPERT), BF16), pltpu.VMEM((D_EXPERT, d), BF16)],
    )
    return pl.pallas_call(
        _expert_kernel,
        grid_spec=grid_spec,
        out_shape=jax.ShapeDtypeStruct(xs.shape, xs.dtype),
        compiler_params=pltpu.CompilerParams(
            dimension_semantics=("arbitrary",), vmem_limit_bytes=VMEM_LIMIT),
        name="experts",
    )(tile_expert, n_valid, xs, wgu, wdn)


def _combine_kernel(slot_ref, slot_next_ref, x_ref, meta_ref, ys_ref, out_ref, buf, sems):
    step = pl.program_id(0)
    tm, d = x_ref.shape
    tok = d // (2 * LANES)

    def gather(slots, half):
        def start(blk, c):
            for u in range(DMA_UNROLL):
                i = blk * DMA_UNROLL + u
                for k in range(TOP_K):
                    _token_copy(ys_ref, slots[0, k * tm + i], buf.at[half, k], i, tok,
                                sems.at[half]).start(priority=k)
            return c

        lax.fori_loop(0, tm // DMA_UNROLL, start, 0)

    cur = step % 2

    @pl.when(step == 0)
    def _():
        gather(slot_ref, 0)

    @pl.when(step + 1 < pl.num_programs(0))
    def _():
        gather(slot_next_ref, 1 - cur)

    for k in range(TOP_K):
        pltpu.make_async_copy(ys_ref.at[pl.ds(0, tm * tok)], buf.at[cur, k], sems.at[cur]).wait()
    meta = meta_ref[...]
    lane = lax.broadcasted_iota(jnp.int32, (1, LANES), 1)
    g0 = _lane_col(meta, lane, META_G0, 0.0, jnp.sum)
    g1 = _lane_col(meta, lane, META_G1, 0.0, jnp.sum)
    y = (g0 * _unpack_bf16_pairs(_load_token_tiles(buf.at[cur, 0], tm, tok))
         + g1 * _unpack_bf16_pairs(_load_token_tiles(buf.at[cur, 1], tm, tok)))
    out_ref[...] = x_ref[...] + y


def _combine(slots, x1, meta, ys):
    t, d = x1.shape
    tm = ROW_TILE
    tok = d // (2 * LANES)
    n_tiles = t // tm
    return pl.pallas_call(
        _combine_kernel,
        grid=(n_tiles,),
        in_specs=[
            pl.BlockSpec((None, 1, TOP_K * tm), lambda i: (i, 0, 0), memory_space=pltpu.SMEM),
            pl.BlockSpec((None, 1, TOP_K * tm), lambda i: (jnp.minimum(i + 1, n_tiles - 1), 0, 0),
                         memory_space=pltpu.SMEM),
            pl.BlockSpec((tm, d), lambda i: (i, 0)),
            pl.BlockSpec((tm, LANES), lambda i: (i, 0)),
            pl.BlockSpec(memory_space=pl.ANY),
        ],
        out_specs=pl.BlockSpec((tm, d), lambda i: (i, 0)),
        out_shape=jax.ShapeDtypeStruct((t, d), F32),
        scratch_shapes=[pltpu.VMEM((2, TOP_K, tm * tok, LANES), ys.dtype),
                        pltpu.SemaphoreType.DMA((2,))],
        compiler_params=pltpu.CompilerParams(
            dimension_semantics=("arbitrary",), vmem_limit_bytes=VMEM_LIMIT),
        name="combine",
    )(slots, slots, x1, meta, ys)


def _rope_expand_matrix():
    src = jnp.arange(LANES)[:, None]
    dst = jnp.arange(LANES)[None, :]
    in_head = dst % HEAD_DIM
    cos_m = (src < ROPE_HALF) & (in_head < ROPE_DIM) & (in_head % ROPE_HALF == src)
    is_sin = (src >= ROPE_HALF) & (src < ROPE_DIM)
    sin_lo = is_sin & (in_head >= ROPE_HALF) & (in_head < ROPE_DIM) & (in_head - ROPE_HALF == src - ROPE_HALF)
    sin_hi = is_sin & (in_head < ROPE_HALF) & (in_head == src - ROPE_HALF)
    e = jnp.concatenate([cos_m.astype(F32), sin_lo.astype(F32), -sin_hi.astype(F32)], axis=1)
    return jnp.concatenate([e, e, e], axis=0).astype(BF16)


def _block_diag_ones():
    i = jnp.arange(2 * LANES)
    return (i[:, None] // HEAD_DIM == i[None, :] // HEAD_DIM).astype(BF16)


def _tile4(g):
    return jnp.tile(g.astype(F32), 2 * LANES // HEAD_DIM)


_B_HEAD_ORDER = tuple(h for p in range(N_HEADS // 2) for h in (p, p + N_HEADS // 2))


def _b_cols():
    order = jnp.asarray(_B_HEAD_ORDER)
    return (order[:, None] * HEAD_DIM + jnp.arange(HEAD_DIM)[None, :]).reshape(-1)


def kernel(x, positions, attn_norm, w_in, q_norm_a, k_norm_a, q_norm_b, k_norm_b, sinks_b,
           out_norm_a, out_norm_b, w_out, ffn_norm, w_router_group, w_router_expert,
           w_gate_up, w_down):
    b, s, d = x.shape
    t = b * s
    depth = w_in.shape[0]
    tok = d // (2 * LANES)
    assert s % CHUNK == 0 and s % ROW_TILE == 0 and s % SWA_STEP == 0
    assert ROW_TILE % (2 * SUBLANES * max(DILATIONS)) == 0 and ROW_TILE % ROUTE_BLOCK == 0 and ROUTE_BLOCK % ROUTE_ROWS == 0
    assert d == 2 * MIX and d % (2 * LANES) == 0

    inv_freq = ROPE_THETA ** (-jnp.arange(0, ROPE_DIM, 2, dtype=F32) / ROPE_DIM)
    ang = positions.astype(F32)[None, :, :] * inv_freq[:, None, None]
    cos_sin = jnp.moveaxis(jnp.concatenate([jnp.cos(ang), jnp.sin(ang)], axis=0), 0, -1)
    rope_tab = jnp.pad(cos_sin, ((0, 0), (0, 0), (0, LANES - ROPE_DIM))).reshape(t, LANES)
    e3 = _rope_expand_matrix()
    bd = _block_diag_ones()
    row_i = jnp.arange(ROW_TILE)
    tri = (row_i[:, None] < row_i[None, :]).astype(BF16)
    bcols = _b_cols()
    q_scale = HEAD_DIM ** -0.5 * LOG2E
    n_sorted = t * TOP_K + N_EXPERTS * GROUP_TILE
    n_tiles = n_sorted // GROUP_TILE
    xs_buf = jnp.zeros((n_sorted * tok, LANES), jnp.uint32)
    nb = s // BLOCK

    x2d = x.reshape(t, d)
    for l in range(depth):
        qb0 = 3 * MIX
        w_l = w_in[l]
        w_l = jnp.concatenate([w_l[:, :qb0], w_l[:, qb0:qb0 + MIX][:, bcols], w_l[:, qb0 + MIX:]], axis=1)
        head_gains = jnp.stack([_tile4(q_norm_a[l]) * q_scale, _tile4(k_norm_a[l]),
                                _tile4(q_norm_b[l]) * q_scale, _tile4(k_norm_b[l])])
        sink = sinks_b[l][jnp.asarray(_B_HEAD_ORDER)].astype(F32) * LOG2E
        gb_perm = out_norm_b[l][bcols]
        wo_l = jnp.concatenate([w_out[l][:MIX], w_out[l][MIX:][bcols]], axis=0).astype(BF16)
        wr = jnp.concatenate(
            [w_router_group[l], jnp.zeros((d, EXPERT_LANE0 - N_GROUPS), F32), w_router_expert[l],
             jnp.zeros((d, LANES - EXPERT_LANE0 - N_EXPERTS), F32)], axis=1)
        wr_hi = wr.astype(BF16)
        wr_split = jnp.concatenate([wr_hi, (wr - wr_hi.astype(F32)).astype(BF16)], axis=1)

        (qa, ka, va, qa4, ka4, va4, qa16, ka16, va16, qb, kb, vb) = _proj(
            x2d, b, attn_norm[l].reshape(1, d), w_l.astype(BF16), rope_tab, e3, bd, head_gains)
        blocks = lambda a: a.reshape(b, nb, BLOCK, a.shape[-1])
        class_blocks = lambda a: a.reshape(b, a.shape[1], a.shape[2] // BLOCK, BLOCK, MIX)
        oa = _dilated_attention(tuple(map(blocks, (qa, ka, va))),
                                tuple(map(class_blocks, (qa4, ka4, va4))),
                                tuple(map(class_blocks, (qa16, ka16, va16))))
        ob = _swa_attention(blocks(qb), blocks(kb), blocks(vb), sink)

        x1, h_tok, meta, meta_t, counts = _out_router(
            oa.reshape(t, MIX), ob.reshape(t, MIX), x2d, out_norm_a[l].reshape(1, MIX),
            gb_perm.reshape(1, MIX), wo_l, ffn_norm[l].reshape(1, d), wr_split, tri)

        cnt = counts[:, 0].astype(jnp.int32)
        padded = ((cnt + GROUP_TILE - 1) // GROUP_TILE) * GROUP_TILE
        ends = jnp.cumsum(padded)
        offs = ends - padded
        eid = meta_t[:, META_E0:META_E1 + 1, :].astype(jnp.int32)
        rank = meta_t[:, META_R0:META_R1 + 1, :].astype(jnp.int32)
        is_expert = eid[..., None] == jnp.arange(N_EXPERTS, dtype=jnp.int32)
        slots = jnp.sum(jnp.where(is_expert, offs, 0), axis=-1) + rank
        slots = slots.reshape(t // ROW_TILE, 1, TOP_K * ROW_TILE)
        tile_start = jnp.arange(n_tiles, dtype=jnp.int32) * GROUP_TILE
        tile_expert = jnp.minimum(
            jnp.sum((tile_start[:, None] >= ends[None, :]).astype(jnp.int32), axis=1), N_EXPERTS - 1)
        n_valid = (ends[-1:] // GROUP_TILE).astype(jnp.int32)

        xs_buf = _dispatch(slots, h_tok, xs_buf, tok)
        ys = _experts(tile_expert, n_valid, xs_buf, w_gate_up, w_down, l)
        x2d = _combine(slots, x1, meta, ys)
    return x2d.reshape(b, s, d)
```

```python
import functools
import math

import jax
import jax.numpy as jnp
from jax import lax
from jax.experimental import pallas as pl
from jax.experimental.pallas import tpu as pltpu

F32 = jnp.float32
BF16 = jnp.bfloat16

HEAD_DIM = 64
N_HEADS = 8
MIX = N_HEADS * HEAD_DIM
N_KV_B = 2
KV_B = N_KV_B * HEAD_DIM
DILATIONS = (1, 4, 16)
BLOCK = 128
CHUNK = BLOCK * max(DILATIONS)
SWA_BACK = 127
SWA_STEP = 8 * BLOCK
ROPE_DIM = HEAD_DIM // 4
ROPE_HALF = ROPE_DIM // 2
ROPE_THETA = 500000.0
N_GROUPS = 4
EXPERTS_PER_GROUP = 8
N_EXPERTS = N_GROUPS * EXPERTS_PER_GROUP
TOP_K = 2
D_EXPERT = 512
EPS = 1e-6
LOG2E = math.log2(math.e)

LANES = 128
SUBLANES = 8
PAIR = 2 * HEAD_DIM
HALF_W = MIX // 2
EXPERT_LANE0 = 32
ROW_TILE = 512
ROUTE_BLOCK = 256
ROUTE_ROWS = 128
GROUP_TILE = 512
DMA_UNROLL = 8
DISPATCH_TILES = 4
VMEM_LIMIT = 48 * 1024 * 1024

META_E0, META_E1, META_G0, META_G1, META_R0, META_R1 = 0, 1, 2, 3, 4, 5


def _split2(a):
    hi = a.astype(BF16)
    lo = (a - hi.astype(F32)).astype(BF16)
    return hi, lo


def _split3(a):
    hi = a.astype(BF16)
    r = a - hi.astype(F32)
    mid = r.astype(BF16)
    lo = (r - mid.astype(F32)).astype(BF16)
    return hi, mid, lo


def _dot(a, b):
    return jnp.dot(a, b, preferred_element_type=F32)


def _rms_rows(a):
    return a * lax.rsqrt(jnp.mean(a * a, axis=-1, keepdims=True) + EPS)


def _lane_col(tile, lane, idx, fill, reduce):
    return reduce(jnp.where(lane == idx, tile, fill), axis=1, keepdims=True)


def _pack_bf16_pairs(val):
    w = val.shape[1] // 2
    bits = lax.bitcast_convert_type(val, jnp.uint32) + jnp.uint32(0x8000)
    return (bits[:, :w] >> 16) | (bits[:, w:] & jnp.uint32(0xFFFF0000))


def _unpack_bf16_pairs(words):
    lo = lax.bitcast_convert_type(words << 16, F32)
    hi = lax.bitcast_convert_type(words & jnp.uint32(0xFFFF0000), F32)
    return jnp.concatenate([lo, hi], axis=1)


def _store_token_tiles(ref, val):
    rows, width = val.shape
    n = width // LANES
    for c in range(n):
        ref[pl.ds(c, rows, stride=n), :] = val[:, c * LANES:(c + 1) * LANES]


def _load_token_tiles(ref, rows, n):
    return jnp.concatenate([ref[pl.ds(c, rows, stride=n), :] for c in range(n)], axis=1)


def _proj_kernel(x_ref, g_ref, w_ref, rope_ref, e3_ref, bd_ref, hg_ref,
                 qa_ref, ka_ref, va_ref, qa4_ref, ka4_ref, va4_ref, qa16_ref, ka16_ref, va16_ref,
                 qb_ref, kb_ref, vb_ref, scr, scr4):
    h = _rms_rows(x_ref[...]) * g_ref[...]
    acc = _dot(h.astype(BF16), w_ref[...])
    tm = acc.shape[0]

    hi, mid, lo = _split3(rope_ref[...])
    tab = _dot(jnp.concatenate([hi, mid, lo], axis=1), e3_ref[...])
    lane = lax.broadcasted_iota(jnp.int32, (1, LANES), 1)
    not_rope = ((lane & (HEAD_DIM - 1)) >= ROPE_DIM).astype(F32)
    c = tab[:, :LANES] + not_rope
    s_lo = tab[:, LANES:2 * LANES]
    s_hi = tab[:, 2 * LANES:]
    bd = bd_ref[...]

    def norm_rope(a, gain):
        w = a.shape[1]
        a2_hi, a2_lo = _split2(a * a)
        ss = _dot(a2_hi, bd[:w, :w]) + _dot(a2_lo, bd[:w, :w])
        y = (a * lax.rsqrt(ss * (1.0 / HEAD_DIM) + EPS)) * gain
        outs = []
        for j in range(w // LANES):
            yj = y[:, j * LANES:(j + 1) * LANES]
            outs.append(yj * c + pltpu.roll(yj, ROPE_HALF, 1) * s_lo
                        + pltpu.roll(yj, LANES - ROPE_HALF, 1) * s_hi)
        return outs[0] if len(outs) == 1 else jnp.concatenate(outs, axis=1)

    def normed(col0, width, gain_row):
        outs = []
        for c0 in range(col0, col0 + width, 2 * LANES):
            wd = min(2 * LANES, col0 + width - c0)
            outs.append(norm_rope(acc[:, c0:c0 + wd], hg_ref[gain_row:gain_row + 1, :wd]))
        return outs[0] if len(outs) == 1 else jnp.concatenate(outs, axis=1)

    def emit_by_class(val, nat_ref, ref4, ref16):
        n_ct = MIX // LANES
        q4 = tm // 4
        nat_ref[...] = val.astype(BF16)
        for ct in range(n_ct):
            scr[ct] = val[:, ct * LANES:(ct + 1) * LANES]
        for r4 in range(4):
            rows = [scr[ct, pl.ds(r4, q4, stride=4), :] for ct in range(n_ct)]
            ref4[r4] = jnp.concatenate(rows, axis=1).astype(BF16)
            for ct in range(n_ct):
                scr4[ct, r4 * q4:(r4 + 1) * q4, :] = rows[ct]
        for r16 in range(16):
            first = (r16 % 4) * q4 + r16 // 4
            rows = [scr4[ct, pl.ds(first, tm // 16, stride=4), :] for ct in range(n_ct)]
            ref16[r16] = jnp.concatenate(rows, axis=1).astype(BF16)

    emit_by_class(normed(0, MIX, 0), qa_ref, qa4_ref, qa16_ref)
    emit_by_class(normed(MIX, MIX, 1), ka_ref, ka4_ref, ka16_ref)
    emit_by_class(acc[:, 2 * MIX:3 * MIX], va_ref, va4_ref, va16_ref)
    qb_ref[...] = normed(3 * MIX, MIX, 2).astype(BF16)
    kb_ref[...] = normed(4 * MIX, KV_B, 3).astype(BF16)
    vb_ref[...] = acc[:, 4 * MIX + KV_B:].astype(BF16)


def _proj(x2d, b, gain, w_bf16, rope_tab, e3, bd, head_gains):
    t, d = x2d.shape
    s = t // b
    pw = w_bf16.shape[1]
    tm = ROW_TILE
    nt = s // tm
    row = lambda bi, i: (bi * nt + i, 0)
    fixed = lambda bi, i: (0, 0)
    by_class = lambda bi, i: (bi, 0, i, 0)
    nat = lambda w: (jax.ShapeDtypeStruct((t, w), BF16), pl.BlockSpec((tm, w), row))
    grouped = lambda dil: (jax.ShapeDtypeStruct((b, dil, s // dil, MIX), BF16),
                           pl.BlockSpec((None, dil, tm // dil, MIX), by_class))
    outs = [nat(MIX)] * 3 + [grouped(4)] * 3 + [grouped(16)] * 3 + [nat(MIX), nat(KV_B), nat(KV_B)]
    return pl.pallas_call(
        _proj_kernel,
        grid=(b, nt),
        in_specs=[
            pl.BlockSpec((tm, d), row),
            pl.BlockSpec((1, d), fixed),
            pl.BlockSpec((d, pw), fixed),
            pl.BlockSpec((tm, LANES), row),
            pl.BlockSpec(e3.shape, fixed),
            pl.BlockSpec(bd.shape, fixed),
            pl.BlockSpec(head_gains.shape, fixed),
        ],
        out_specs=[o[1] for o in outs],
        out_shape=[o[0] for o in outs],
        scratch_shapes=[pltpu.VMEM((MIX // LANES, tm, LANES), F32)] * 2,
        compiler_params=pltpu.CompilerParams(
            dimension_semantics=("arbitrary", "arbitrary"), vmem_limit_bytes=VMEM_LIMIT),
        name="proj",
    )(x2d, gain, w_bf16, rope_tab, e3, bd, head_gains)


def _band(min_back, first_key):
    qi = lax.broadcasted_iota(jnp.int32, (BLOCK, 2 * BLOCK), 0)
    kj = lax.broadcasted_iota(jnp.int32, (BLOCK, 2 * BLOCK), 1)
    valid = (kj >= qi + min_back) & (kj <= qi + BLOCK)
    return valid if first_key is None else valid & (kj >= first_key)


def _head_scores(q_pair, kcat, sel, valid):
    qm = jnp.where(sel, q_pair, jnp.zeros_like(q_pair))
    s = lax.dot_general(qm, kcat, (((1,), (1,)), ((), ())), preferred_element_type=F32)
    return jnp.where(valid, s, -jnp.inf)


def _weighted_values(pr, v_pair, sel):
    return _dot(pr.astype(BF16), jnp.where(sel, v_pair, jnp.ones_like(v_pair)))


def _normalise_pair(res_even, res_odd, even, extra=None):
    num = jnp.where(even, res_even, res_odd)
    den = pltpu.roll(jnp.where(even, res_odd, res_even), HEAD_DIM, 1)
    return num / (den if extra is None else den + extra)


def _with_halo(prev, cur):
    return jnp.concatenate([prev, cur], axis=0)


def _swa_kernel(q_ref, kp_ref, k_ref, vp_ref, v_ref, sink_ref, o_ref, m_st, r_st):
    first_key = jnp.where(pl.program_id(1) > 0, 0, BLOCK)
    band = _band(BLOCK - SWA_BACK, None)
    band_first = _band(BLOCK - SWA_BACK, first_key)
    lane = lax.broadcasted_iota(jnp.int32, (1, LANES), 1)
    even = lane < HEAD_DIM
    n_tiles = SWA_STEP // BLOCK
    for j in range(n_tiles):
        rows = slice(j * BLOCK, (j + 1) * BLOCK)
        valid = band_first if j == 0 else band
        kcat = _with_halo(kp_ref[...] if j == 0 else k_ref[j - 1], k_ref[j])
        vcat = _with_halo(vp_ref[...] if j == 0 else v_ref[j - 1], v_ref[j])
        for p in range(N_HEADS // 2):
            q_pair = q_ref[j, :, p * PAIR:(p + 1) * PAIR]
            for half in range(2):
                h = 2 * p + half
                sel = even if half == 0 else jnp.logical_not(even)
                s = _head_scores(q_pair, kcat, sel, valid)
                m = jnp.maximum(jnp.max(s, axis=1, keepdims=True), sink_ref[h])
                m_st[h, rows, :] = jnp.broadcast_to(m, (BLOCK, LANES))
                r_st[h, rows, :] = _weighted_values(jnp.exp2(s - m), vcat, sel)

    def finish(j, carry):
        rows = pl.ds(pl.multiple_of(j * BLOCK, BLOCK), BLOCK)
        for p in range(N_HEADS // 2):
            sink_pair = jnp.where(even, sink_ref[2 * p], sink_ref[2 * p + 1])
            m_pair = jnp.where(even, m_st[2 * p, rows, :], m_st[2 * p + 1, rows, :])
            out = _normalise_pair(r_st[2 * p, rows, :], r_st[2 * p + 1, rows, :], even,
                                  jnp.exp2(sink_pair - m_pair))
            o_ref[j, :, p * PAIR:(p + 1) * PAIR] = out.astype(o_ref.dtype)
        return carry

    lax.fori_loop(0, n_tiles, finish, 0, unroll=2)


def _swa_attention(q, k, v, sink):
    b, nb = q.shape[:2]
    step = SWA_STEP // BLOCK
    cur = lambda w: pl.BlockSpec((None, step, BLOCK, w), lambda bi, c: (bi, c, 0, 0))
    prev = pl.BlockSpec((None, None, BLOCK, KV_B),
                        lambda bi, c: (bi, jnp.maximum(step * c - 1, 0), 0, 0))
    return pl.pallas_call(
        _swa_kernel,
        grid=(b, nb // step),
        in_specs=[cur(MIX), prev, cur(KV_B), prev, cur(KV_B), pl.BlockSpec(memory_space=pltpu.SMEM)],
        out_specs=cur(MIX),
        out_shape=jax.ShapeDtypeStruct(q.shape, BF16),
        scratch_shapes=[pltpu.VMEM((N_HEADS, SWA_STEP, LANES), F32)] * 2,
        compiler_params=pltpu.CompilerParams(
            dimension_semantics=("arbitrary", "arbitrary"), vmem_limit_bytes=VMEM_LIMIT),
        name="swa",
    )(q, k, k, v, v, sink)


def _dilated_kernel(qn, knp, kn, vnp, vn, q4, k4p, k4, v4p, v4, q16, k16p, k16, v16p, v16,
                    o_ref, m_st, r_st):
    n_pairs = HALF_W // PAIR
    first_key = jnp.where(pl.program_id(1) > 0, 0, BLOCK)
    band = _band(0, None)
    band_first = _band(0, first_key)
    lane = lax.broadcasted_iota(jnp.int32, (1, LANES), 1)
    even = lane < HEAD_DIM

    def tile(q, kcat, vcat, valid, row0, stride, init):
        rows = pl.ds(row0, BLOCK) if stride == 1 else pl.ds(row0, BLOCK, stride=stride)
        for p in range(n_pairs):
            cols = slice(p * PAIR, (p + 1) * PAIR)
            for half in range(2):
                h = 2 * p + half
                sel = even if half == 0 else jnp.logical_not(even)
                s = _head_scores(q[:, cols], kcat[:, cols], sel, valid)
                m_tile = jnp.max(s, axis=1, keepdims=True)
                if init:
                    m_new = jnp.broadcast_to(m_tile, (BLOCK, LANES))
                    res = _weighted_values(jnp.exp2(s - m_tile), vcat[:, cols], sel)
                else:
                    m_old = m_st[h, rows, :]
                    m_new = jnp.maximum(m_old, m_tile)
                    pr = jnp.exp2(s - jnp.concatenate([m_new, m_new], axis=1))
                    res = (_weighted_values(pr, vcat[:, cols], sel)
                           + jnp.exp2(m_old - m_new) * r_st[h, rows, :])
                m_st[h, rows, :] = m_new
                r_st[h, rows, :] = res

    def d16(r, carry):
        tile(q16[r], _with_halo(k16p[r], k16[r]), _with_halo(v16p[r], v16[r]), band_first, r, 16, True)
        return carry

    lax.fori_loop(0, 16, d16, 0, unroll=True)

    tile(qn[0], _with_halo(knp[...], kn[0]), _with_halo(vnp[...], vn[0]), band_first, 0, 1, False)

    def d1(j, carry):
        tile(qn[j], _with_halo(kn[j - 1], kn[j]), _with_halo(vn[j - 1], vn[j]),
             band, pl.multiple_of(j * BLOCK, BLOCK), 1, False)
        return carry

    lax.fori_loop(1, CHUNK // BLOCK, d1, 0, unroll=True)

    def d4(r, carry):
        tile(q4[r, 0], _with_halo(k4p[r], k4[r, 0]), _with_halo(v4p[r], v4[r, 0]), band_first, r, 4, False)
        for j in range(1, CHUNK // (4 * BLOCK)):
            tile(q4[r, j], _with_halo(k4[r, j - 1], k4[r, j]), _with_halo(v4[r, j - 1], v4[r, j]),
                 band, r + 4 * j * BLOCK, 4, False)
        return carry

    lax.fori_loop(0, 4, d4, 0, unroll=True)

    def finish(j, carry):
        rows = pl.ds(pl.multiple_of(j * BLOCK, BLOCK), BLOCK)
        for p in range(n_pairs):
            out = _normalise_pair(r_st[2 * p, rows, :], r_st[2 * p + 1, rows, :], even)
            o_ref[j, :, p * PAIR:(p + 1) * PAIR] = out.astype(o_ref.dtype)
        return carry

    lax.fori_loop(0, CHUNK // BLOCK, finish, 0, unroll=4)


def _dilated_attention(nat, by4, by16):
    b, nb = nat[0].shape[:2]
    per_chunk = CHUNK // BLOCK
    n_chunks = nb // per_chunk

    def specs(dil):
        step = per_chunk // dil
        if dil == 1:
            cur = pl.BlockSpec((None, step, BLOCK, HALF_W), lambda bi, c, hf: (bi, c, 0, hf))
            prev = pl.BlockSpec((None, None, BLOCK, HALF_W),
                                lambda bi, c, hf: (bi, jnp.maximum(step * c - 1, 0), 0, hf))
        else:
            blocks = None if step == 1 else step
            cur = pl.BlockSpec((None, dil, blocks, BLOCK, HALF_W), lambda bi, c, hf: (bi, 0, c, 0, hf))
            prev = pl.BlockSpec((None, dil, None, BLOCK, HALF_W),
                                lambda bi, c, hf: (bi, 0, jnp.maximum(step * c - 1, 0), 0, hf))
        return cur, prev

    args, in_specs = [], []
    for (q, k, v), dil in ((nat, 1), (by4, 4), (by16, 16)):
        cur, prev = specs(dil)
        args += [q, k, k, v, v]
        in_specs += [cur, prev, cur, prev, cur]
    return pl.pallas_call(
        _dilated_kernel,
        grid=(b, n_chunks, MIX // HALF_W),
        in_specs=in_specs,
        out_specs=specs(1)[0],
        out_shape=jax.ShapeDtypeStruct(nat[0].shape, BF16),
        scratch_shapes=[pltpu.VMEM((HALF_W // HEAD_DIM, CHUNK, LANES), F32),
                        pltpu.VMEM((HALF_W // HEAD_DIM, CHUNK, LANES), F32)],
        compiler_params=pltpu.CompilerParams(
            dimension_semantics=("arbitrary", "arbitrary", "arbitrary"),
            vmem_limit_bytes=VMEM_LIMIT),
        name="dilated",
    )(*args)


def _route(lt, tri, base):
    n = lt.shape[1]
    row8 = lax.broadcasted_iota(jnp.int32, (SUBLANES, n), 0)
    far = jnp.int32(LANES)
    neg = -jnp.inf

    def first_argmax(vals):
        mx = jnp.max(vals, axis=0, keepdims=True)
        idx = jnp.min(jnp.where(vals == mx, row8, far), axis=0, keepdims=True)
        return mx, idx

    gl = jnp.where(row8 < N_GROUPS, lt[:SUBLANES, :], neg)
    gmax, gidx = first_argmax(gl)
    g_top = 1.0 / jnp.sum(jnp.exp(gl - gmax), axis=0, keepdims=True)

    el = lt[EXPERT_LANE0:EXPERT_LANE0 + EXPERTS_PER_GROUP, :]
    for g in range(1, N_GROUPS):
        r0 = EXPERT_LANE0 + g * EXPERTS_PER_GROUP
        el = jnp.where(gidx == g, lt[r0:r0 + EXPERTS_PER_GROUP, :], el)
    m1, i1 = first_argmax(el)
    m2, i2 = first_argmax(jnp.where(row8 == i1, neg, el))
    t2 = jnp.exp(m2 - m1)
    gate0 = g_top / (1.0 + t2)
    gate1 = g_top * t2 / (1.0 + t2)
    e0 = gidx * EXPERTS_PER_GROUP + i1
    e1 = gidx * EXPERTS_PER_GROUP + i2

    row_e = lax.broadcasted_iota(jnp.int32, (N_EXPERTS, n), 0)
    oh0 = row_e == e0
    oh1 = row_e == e1
    cum = _dot(jnp.concatenate([oh0.astype(BF16), oh1.astype(BF16)], axis=0), tri)
    oh0 = oh0.astype(F32)
    oh1 = oh1.astype(F32)
    tot0 = jnp.sum(oh0, axis=1, keepdims=True)
    tot1 = jnp.sum(oh1, axis=1, keepdims=True)
    base_n = jnp.concatenate([base] * (n // LANES), axis=1)
    rank0 = jnp.sum(oh0 * (cum[:N_EXPERTS] + base_n), axis=0, keepdims=True)
    rank1 = jnp.sum(oh1 * (cum[N_EXPERTS:] + (base_n + tot0)), axis=0, keepdims=True)

    meta_t = jnp.zeros((SUBLANES, n), F32)
    for r, val in ((META_E0, e0.astype(F32)), (META_E1, e1.astype(F32)), (META_G0, gate0),
                   (META_G1, gate1), (META_R0, rank0), (META_R1, rank1)):
        meta_t = jnp.where(row8 == r, val, meta_t)
    return meta_t, base + (tot0 + tot1)


def _out_router_kernel(oa_ref, ob_ref, x_ref, ga_ref, gb_ref, wo_ref, gf_ref, wr_ref, tri_ref,
                       x1_ref, h_ref, meta_ref, meta_t_ref, cnt_ref, carry_ref):
    @pl.when(pl.program_id(0) == 0)
    def _():
        carry_ref[...] = jnp.zeros_like(carry_ref)

    tm, d = x_ref.shape
    tok = d // (2 * LANES)
    logits = []
    for r0 in range(0, tm, ROUTE_BLOCK):
        rows = slice(r0, r0 + ROUTE_BLOCK)
        na = _rms_rows(oa_ref[rows, :].astype(F32)) * ga_ref[...]
        nb = _rms_rows(ob_ref[rows, :].astype(F32)) * gb_ref[...]
        mixed = jnp.concatenate([na, nb], axis=1).astype(BF16)
        x1 = x_ref[rows, :] + _dot(mixed, wo_ref[...])
        x1_ref[rows, :] = x1
        h = _rms_rows(x1) * gf_ref[...]
        _store_token_tiles(h_ref.at[r0 * tok:(r0 + ROUTE_BLOCK) * tok], _pack_bf16_pairs(h))

        h_hi, h_lo = _split2(h)
        parts = _dot(h_hi, wr_ref[...]) + _dot(h_lo, wr_ref[...])
        logits.append(parts[:, :LANES] + parts[:, LANES:])

    lt = jnp.concatenate(logits, axis=0).T
    meta_t, base = _route(lt, tri_ref[...], carry_ref[...])
    carry_ref[...] = base
    cnt_ref[...] = base
    meta_t_ref[...] = meta_t
    meta_ref[...] = jnp.concatenate([meta_t, jnp.zeros((LANES - SUBLANES, tm), F32)], axis=0).T


def _out_router(oa, ob, x2d, ga, gb, wo_bf16, gf, wr, tri):
    t, d = x2d.shape
    tm = ROW_TILE
    tok = d // (2 * LANES)
    row = lambda i: (i, 0)
    fixed = lambda i: (0, 0)
    return pl.pallas_call(
        _out_router_kernel,
        grid=(t // tm,),
        in_specs=[
            pl.BlockSpec((tm, MIX), row),
            pl.BlockSpec((tm, MIX), row),
            pl.BlockSpec((tm, d), row),
            pl.BlockSpec((1, MIX), fixed),
            pl.BlockSpec((1, MIX), fixed),
            pl.BlockSpec((2 * MIX, d), fixed),
            pl.BlockSpec((1, d), fixed),
            pl.BlockSpec((d, 2 * LANES), fixed),
            pl.BlockSpec((tm, tm), fixed),
        ],
        out_specs=[
            pl.BlockSpec((tm, d), row),
            pl.BlockSpec((tm * tok, LANES), row),
            pl.BlockSpec((tm, LANES), row),
            pl.BlockSpec((None, SUBLANES, tm), lambda i: (i, 0, 0)),
            pl.BlockSpec((N_EXPERTS, LANES), fixed),
        ],
        out_shape=[
            jax.ShapeDtypeStruct((t, d), F32),
            jax.ShapeDtypeStruct((t * tok, LANES), jnp.uint32),
            jax.ShapeDtypeStruct((t, LANES), F32),
            jax.ShapeDtypeStruct((t // tm, SUBLANES, tm), F32),
            jax.ShapeDtypeStruct((N_EXPERTS, LANES), F32),
        ],
        scratch_shapes=[pltpu.VMEM((N_EXPERTS, LANES), F32)],
        compiler_params=pltpu.CompilerParams(
            dimension_semantics=("arbitrary",), vmem_limit_bytes=VMEM_LIMIT),
        name="out_router",
    )(oa, ob, x2d, ga, gb, wo_bf16, gf, wr, tri)


def _token_copy(src, i, dst, j, tok, sem):
    return pltpu.make_async_copy(src.at[pl.ds(pl.multiple_of(i * tok, tok), tok)],
                                 dst.at[pl.ds(pl.multiple_of(j * tok, tok), tok)], sem)


def _dispatch_kernel(slot_ref, h_ref, xs_in_ref, xs_ref, sem, *, tok):
    del xs_in_ref
    n_sub = slot_ref.shape[0]
    tm = slot_ref.shape[2] // TOP_K

    for r in range(n_sub):
        def start(blk, c, src=h_ref.at[pl.ds(r * tm * tok, tm * tok)], slots=slot_ref.at[r]):
            for u in range(DMA_UNROLL):
                i = blk * DMA_UNROLL + u
                for k in range(TOP_K):
                    _token_copy(src, i, xs_ref, slots[0, k * tm + i], tok, sem).start(priority=k)
            return c

        lax.fori_loop(0, tm // DMA_UNROLL, start, 0)
    for k in range(TOP_K):
        pltpu.make_async_copy(h_ref, xs_ref.at[pl.ds(0, n_sub * tm * tok)], sem).wait()


def _dispatch(slots, h_tok, xs_buf, tok):
    tm = ROW_TILE
    n_sub = DISPATCH_TILES
    n_tiles = h_tok.shape[0] // (n_sub * tm * tok)
    return pl.pallas_call(
        functools.partial(_dispatch_kernel, tok=tok),
        grid=(n_tiles,),
        in_specs=[
            pl.BlockSpec((n_sub, 1, TOP_K * tm), lambda i: (i, 0, 0), memory_space=pltpu.SMEM),
            pl.BlockSpec((n_sub * tm * tok, LANES), lambda i: (i, 0)),
            pl.BlockSpec(memory_space=pl.ANY),
        ],
        out_specs=pl.BlockSpec(memory_space=pl.ANY),
        out_shape=jax.ShapeDtypeStruct(xs_buf.shape, xs_buf.dtype),
        scratch_shapes=[pltpu.SemaphoreType.DMA(())],
        input_output_aliases={2: 0},
        compiler_params=pltpu.CompilerParams(
            dimension_semantics=("arbitrary",), vmem_limit_bytes=VMEM_LIMIT),
        name="dispatch",
    )(slots, h_tok, xs_buf)


def _expert_kernel(te_ref, nv_ref, xs_ref, wgu_ref, wdn_ref, ys_ref, wgu_bf, wdn_bf):
    i = pl.program_id(0)
    d = wgu_ref.shape[0]
    tok = d // (2 * LANES)
    tg = xs_ref.shape[0] // tok

    @pl.when((i == 0) | (te_ref[i] != te_ref[jnp.maximum(i - 1, 0)]))
    def _():
        wgu_bf[...] = wgu_ref[...].astype(BF16)
        wdn_bf[...] = wdn_ref[...].astype(BF16)

    @pl.when(i < nv_ref[0])
    def _():
        xs = _unpack_bf16_pairs(_load_token_tiles(xs_ref, tg, tok))
        gu = _dot(xs.astype(BF16), wgu_bf[...])
        g = gu[:, :D_EXPERT]
        a = (g * (1.0 / (1.0 + jnp.exp(-g)))) * gu[:, D_EXPERT:]
        _store_token_tiles(ys_ref, _pack_bf16_pairs(_dot(a.astype(BF16), wdn_bf[...])))

    @pl.when(i >= nv_ref[0])
    def _():
        ys_ref[...] = jnp.zeros_like(ys_ref)


def _experts(tile_expert, n_valid, xs, wgu, wdn, layer):
    d = wgu.shape[2]
    tok = d // (2 * LANES)
    tg = GROUP_TILE
    grid_spec = pltpu.PrefetchScalarGridSpec(
        num_scalar_prefetch=2,
        grid=(xs.shape[0] // (tg * tok),),
        in_specs=[
            pl.BlockSpec((tg * tok, LANES), lambda i, te, nv: (i, 0)),
            pl.BlockSpec((None, None, d, 2 * D_EXPERT), lambda i, te, nv: (layer, te[i], 0, 0)),
            pl.BlockSpec((None, None, D_EXPERT, d), lambda i, te, nv: (layer, te[i], 0, 0)),
        ],
        out_specs=pl.BlockSpec((tg * tok, LANES), lambda i, te, nv: (i, 0)),
        scratch_shapes=[pltpu.VMEM((d, 2 * D_EXPERT), BF16), pltpu.VMEM((D_EXPERT, d), BF16)],
    )
    return pl.pallas_call(
        _expert_kernel,
        grid_spec=grid_spec,
        out_shape=jax.ShapeDtypeStruct(xs.shape, xs.dtype),
        compiler_params=pltpu.CompilerParams(
            dimension_semantics=("arbitrary",), vmem_limit_bytes=VMEM_LIMIT),
        name="experts",
    )(tile_expert, n_valid, xs, wgu, wdn)


def _combine_kernel(slot_ref, slot_next_ref, x_ref, meta_ref, ys_ref, out_ref, buf, sems):
    step = pl.program_id(0)
    tm, d = x_ref.shape
    tok = d // (2 * LANES)

    def gather(slots, half):
        def start(blk, c):
            for u in range(DMA_UNROLL):
                i = blk * DMA_UNROLL + u
                for k in range(TOP_K):
                    _token_copy(ys_ref, slots[0, k * tm + i], buf.at[half, k], i, tok,
                                sems.at[half]).start(priority=k)
            return c

        lax.fori_loop(0, tm // DMA_UNROLL, start, 0)

    cur = step % 2

    @pl.when(step == 0)
    def _():
        gather(slot_ref, 0)

    @pl.when(step + 1 < pl.num_programs(0))
    def _():
        gather(slot_next_ref, 1 - cur)

    for k in range(TOP_K):
        pltpu.make_async_copy(ys_ref.at[pl.ds(0, tm * tok)], buf.at[cur, k], sems.at[cur]).wait()
    meta = meta_ref[...]
    lane = lax.broadcasted_iota(jnp.int32, (1, LANES), 1)
    g0 = _lane_col(meta, lane, META_G0, 0.0, jnp.sum)
    g1 = _lane_col(meta, lane, META_G1, 0.0, jnp.sum)
    y = (g0 * _unpack_bf16_pairs(_load_token_tiles(buf.at[cur, 0], tm, tok))
         + g1 * _unpack_bf16_pairs(_load_token_tiles(buf.at[cur, 1], tm, tok)))
    out_ref[...] = x_ref[...] + y


def _combine(slots, x1, meta, ys):
    t, d = x1.shape
    tm = ROW_TILE
    tok = d // (2 * LANES)
    n_tiles = t // tm
    return pl.pallas_call(
        _combine_kernel,
        grid=(n_tiles,),
        in_specs=[
            pl.BlockSpec((None, 1, TOP_K * tm), lambda i: (i, 0, 0), memory_space=pltpu.SMEM),
            pl.BlockSpec((None, 1, TOP_K * tm), lambda i: (jnp.minimum(i + 1, n_tiles - 1), 0, 0),
                         memory_space=pltpu.SMEM),
            pl.BlockSpec((tm, d), lambda i: (i, 0)),
            pl.BlockSpec((tm, LANES), lambda i: (i, 0)),
            pl.BlockSpec(memory_space=pl.ANY),
        ],
        out_specs=pl.BlockSpec((tm, d), lambda i: (i, 0)),
        out_shape=jax.ShapeDtypeStruct((t, d), F32),
        scratch_shapes=[pltpu.VMEM((2, TOP_K, tm * tok, LANES), ys.dtype),
                        pltpu.SemaphoreType.DMA((2,))],
        compiler_params=pltpu.CompilerParams(
            dimension_semantics=("arbitrary",), vmem_limit_bytes=VMEM_LIMIT),
        name="combine",
    )(slots, slots, x1, meta, ys)


def _rope_expand_matrix():
    src = jnp.arange(LANES)[:, None]
    dst = jnp.arange(LANES)[None, :]
    in_head = dst % HEAD_DIM
    cos_m = (src < ROPE_HALF) & (in_head < ROPE_DIM) & (in_head % ROPE_HALF == src)
    is_sin = (src >= ROPE_HALF) & (src < ROPE_DIM)
    sin_lo = is_sin & (in_head >= ROPE_HALF) & (in_head < ROPE_DIM) & (in_head - ROPE_HALF == src - ROPE_HALF)
    sin_hi = is_sin & (in_head < ROPE_HALF) & (in_head == src - ROPE_HALF)
    e = jnp.concatenate([cos_m.astype(F32), sin_lo.astype(F32), -sin_hi.astype(F32)], axis=1)
    return jnp.concatenate([e, e, e], axis=0).astype(BF16)


def _block_diag_ones():
    i = jnp.arange(2 * LANES)
    return (i[:, None] // HEAD_DIM == i[None, :] // HEAD_DIM).astype(BF16)


def _tile4(g):
    return jnp.tile(g.astype(F32), 2 * LANES // HEAD_DIM)


_B_HEAD_ORDER = tuple(h for p in range(N_HEADS // 2) for h in (p, p + N_HEADS // 2))


def _b_cols():
    order = jnp.asarray(_B_HEAD_ORDER)
    return (order[:, None] * HEAD_DIM + jnp.arange(HEAD_DIM)[None, :]).reshape(-1)


def kernel(x, positions, attn_norm, w_in, q_norm_a, k_norm_a, q_norm_b, k_norm_b, sinks_b,
           out_norm_a, out_norm_b, w_out, ffn_norm, w_router_group, w_router_expert,
           w_gate_up, w_down):
    b, s, d = x.shape
    t = b * s
    depth = w_in.shape[0]
    tok = d // (2 * LANES)
    assert s % CHUNK == 0 and s % ROW_TILE == 0 and s % SWA_STEP == 0
    assert ROW_TILE % (2 * SUBLANES * max(DILATIONS)) == 0 and ROW_TILE % ROUTE_BLOCK == 0 and ROUTE_BLOCK % ROUTE_ROWS == 0
    assert d == 2 * MIX and d % (2 * LANES) == 0 and t % (DISPATCH_TILES * ROW_TILE) == 0

    inv_freq = ROPE_THETA ** (-jnp.arange(0, ROPE_DIM, 2, dtype=F32) / ROPE_DIM)
    ang = positions.astype(F32)[None, :, :] * inv_freq[:, None, None]
    cos_sin = jnp.moveaxis(jnp.concatenate([jnp.cos(ang), jnp.sin(ang)], axis=0), 0, -1)
    rope_tab = jnp.pad(cos_sin, ((0, 0), (0, 0), (0, LANES - ROPE_DIM))).reshape(t, LANES)
    e3 = _rope_expand_matrix()
    bd = _block_diag_ones()
    row_i = jnp.arange(ROW_TILE)
    tri = (row_i[:, None] < row_i[None, :]).astype(BF16)
    bcols = _b_cols()
    q_scale = HEAD_DIM ** -0.5 * LOG2E
    n_sorted = t * TOP_K + N_EXPERTS * GROUP_TILE
    n_tiles = n_sorted // GROUP_TILE
    xs_buf = jnp.zeros((n_sorted * tok, LANES), jnp.uint32)
    nb = s // BLOCK

    x2d = x.reshape(t, d)
    for l in range(depth):
        qb0 = 3 * MIX
        w_l = w_in[l]
        w_l = jnp.concatenate([w_l[:, :qb0], w_l[:, qb0:qb0 + MIX][:, bcols], w_l[:, qb0 + MIX:]], axis=1)
        head_gains = jnp.stack([_tile4(q_norm_a[l]) * q_scale, _tile4(k_norm_a[l]),
                                _tile4(q_norm_b[l]) * q_scale, _tile4(k_norm_b[l])])
        sink = sinks_b[l][jnp.asarray(_B_HEAD_ORDER)].astype(F32) * LOG2E
        gb_perm = out_norm_b[l][bcols]
        wo_l = jnp.concatenate([w_out[l][:MIX], w_out[l][MIX:][bcols]], axis=0).astype(BF16)
        wr = jnp.concatenate(
            [w_router_group[l], jnp.zeros((d, EXPERT_LANE0 - N_GROUPS), F32), w_router_expert[l],
             jnp.zeros((d, LANES - EXPERT_LANE0 - N_EXPERTS), F32)], axis=1)
        wr_hi = wr.astype(BF16)
        wr_split = jnp.concatenate([wr_hi, (wr - wr_hi.astype(F32)).astype(BF16)], axis=1)

        (qa, ka, va, qa4, ka4, va4, qa16, ka16, va16, qb, kb, vb) = _proj(
            x2d, b, attn_norm[l].reshape(1, d), w_l.astype(BF16), rope_tab, e3, bd, head_gains)
        blocks = lambda a: a.reshape(b, nb, BLOCK, a.shape[-1])
        class_blocks = lambda a: a.reshape(b, a.shape[1], a.shape[2] // BLOCK, BLOCK, MIX)
        oa = _dilated_attention(tuple(map(blocks, (qa, ka, va))),
                                tuple(map(class_blocks, (qa4, ka4, va4))),
                                tuple(map(class_blocks, (qa16, ka16, va16))))
        ob = _swa_attention(blocks(qb), blocks(kb), blocks(vb), sink)

        x1, h_tok, meta, meta_t, counts = _out_router(
            oa.reshape(t, MIX), ob.reshape(t, MIX), x2d, out_norm_a[l].reshape(1, MIX),
            gb_perm.reshape(1, MIX), wo_l, ffn_norm[l].reshape(1, d), wr_split, tri)

        cnt = counts[:, 0].astype(jnp.int32)
        padded = ((cnt + GROUP_TILE - 1) // GROUP_TILE) * GROUP_TILE
        ends = jnp.cumsum(padded)
        offs = ends - padded
        eid = meta_t[:, META_E0:META_E1 + 1, :].astype(jnp.int32)
        rank = meta_t[:, META_R0:META_R1 + 1, :].astype(jnp.int32)
        is_expert = eid[..., None] == jnp.arange(N_EXPERTS, dtype=jnp.int32)
        slots = jnp.sum(jnp.where(is_expert, offs, 0), axis=-1) + rank
        slots = slots.reshape(t // ROW_TILE, 1, TOP_K * ROW_TILE)
        tile_start = jnp.arange(n_tiles, dtype=jnp.int32) * GROUP_TILE
        tile_expert = jnp.minimum(
            jnp.sum((tile_start[:, None] >= ends[None, :]).astype(jnp.int32), axis=1), N_EXPERTS - 1)
        n_valid = (ends[-1:] // GROUP_TILE).astype(jnp.int32)

        xs_buf = _dispatch(slots, h_tok, xs_buf, tok)
        ys = _experts(tile_expert, n_valid, xs_buf, w_gate_up, w_down, l)
        x2d = _combine(slots, x1, meta, ys)
    return x2d.reshape(b, s, d)
```

```python
import functools
import math

import jax
import jax.numpy as jnp
from jax import lax
from jax.experimental import pallas as pl
from jax.experimental.pallas import tpu as pltpu

F32 = jnp.float32
BF16 = jnp.bfloat16

HEAD_DIM = 64
N_HEADS = 8
MIX = N_HEADS * HEAD_DIM
N_KV_B = 2
KV_B = N_KV_B * HEAD_DIM
DILATIONS = (1, 4, 16)
BLOCK = 128
CHUNK = BLOCK * max(DILATIONS)
SWA_BACK = 127
SWA_STEP = 8 * BLOCK
ROPE_DIM = HEAD_DIM // 4
ROPE_HALF = ROPE_DIM // 2
ROPE_THETA = 500000.0
N_GROUPS = 4
EXPERTS_PER_GROUP = 8
N_EXPERTS = N_GROUPS * EXPERTS_PER_GROUP
TOP_K = 2
D_EXPERT = 512
EPS = 1e-6
LOG2E = math.log2(math.e)

LANES = 128
SUBLANES = 8
PAIR = 2 * HEAD_DIM
HALF_W = MIX // 2
EXPERT_LANE0 = 32
ROW_TILE = 512
PROJ_BLOCK = 256
ROUTE_BLOCK = 128
GROUP_TILE = 512
DMA_UNROLL = 8
DISPATCH_TILES = 4
VMEM_LIMIT = 48 * 1024 * 1024

META_E0, META_E1, META_G0, META_G1, META_R0, META_R1 = 0, 1, 2, 3, 4, 5


def _split2(a):
    hi = a.astype(BF16)
    lo = (a - hi.astype(F32)).astype(BF16)
    return hi, lo


def _split3(a):
    hi = a.astype(BF16)
    r = a - hi.astype(F32)
    mid = r.astype(BF16)
    lo = (r - mid.astype(F32)).astype(BF16)
    return hi, mid, lo


def _dot(a, b):
    return jnp.dot(a, b, preferred_element_type=F32)


def _rms_rows(a):
    return a * lax.rsqrt(jnp.mean(a * a, axis=-1, keepdims=True) + EPS)


def _lane_col(tile, lane, idx, fill, reduce):
    return reduce(jnp.where(lane == idx, tile, fill), axis=1, keepdims=True)


def _pack_bf16_pairs(val):
    w = val.shape[1] // 2
    bits = lax.bitcast_convert_type(val, jnp.uint32) + jnp.uint32(0x8000)
    return (bits[:, :w] >> 16) | (bits[:, w:] & jnp.uint32(0xFFFF0000))


def _unpack_bf16_pairs(words):
    lo = lax.bitcast_convert_type(words << 16, F32)
    hi = lax.bitcast_convert_type(words & jnp.uint32(0xFFFF0000), F32)
    return jnp.concatenate([lo, hi], axis=1)


def _store_token_tiles(ref, val):
    rows, width = val.shape
    n = width // LANES
    for c in range(n):
        ref[pl.ds(c, rows, stride=n), :] = val[:, c * LANES:(c + 1) * LANES]


def _load_token_tiles(ref, rows, n):
    return jnp.concatenate([ref[pl.ds(c, rows, stride=n), :] for c in range(n)], axis=1)


def _proj_kernel(x_ref, g_ref, w_ref, rope_ref, e3_ref, bd_ref, hg_ref,
                 qa_ref, ka_ref, va_ref, qa4_ref, ka4_ref, va4_ref, qa16_ref, ka16_ref, va16_ref,
                 qb_ref, kb_ref, vb_ref, scr_q, scr_k, scr_v, scr4):
    tm = x_ref.shape[0]
    lane = lax.broadcasted_iota(jnp.int32, (1, LANES), 1)
    not_rope = ((lane & (HEAD_DIM - 1)) >= ROPE_DIM).astype(F32)
    bd = bd_ref[...]
    n_ct = MIX // LANES
    normed_groups = [(c0, min(2 * LANES, col0 + width - c0), g)
                     for col0, width, g in ((0, MIX, 0), (MIX, MIX, 1), (3 * MIX, MIX, 2), (4 * MIX, KV_B, 3))
                     for c0 in range(col0, col0 + width, 2 * LANES)]

    staged = []
    for r0 in range(0, tm, PROJ_BLOCK):
        rows = slice(r0, r0 + PROJ_BLOCK)
        h = _rms_rows(x_ref[rows, :]) * g_ref[...]
        acc = _dot(h.astype(BF16), w_ref[...])
        hi, mid, lo = _split3(rope_ref[rows, :])
        tab = _dot(jnp.concatenate([hi, mid, lo], axis=1), e3_ref[...])
        sums = {}
        for c0, wd, _ in normed_groups:
            a = acc[:, c0:c0 + wd]
            a2_hi, a2_lo = _split2(a * a)
            sums[c0] = _dot(a2_hi, bd[:wd, :wd]) + _dot(a2_lo, bd[:wd, :wd])
        staged.append((rows, acc, tab, sums))

    for rows, acc, tab, sums in staged:
        c = tab[:, :LANES] + not_rope
        s_lo = tab[:, LANES:2 * LANES]
        s_hi = tab[:, 2 * LANES:]
        tiles = {}
        for c0, wd, g in normed_groups:
            y = (acc[:, c0:c0 + wd] * lax.rsqrt(sums[c0] * (1.0 / HEAD_DIM) + EPS)) * hg_ref[g:g + 1, :wd]
            for j in range(wd // LANES):
                yj = y[:, j * LANES:(j + 1) * LANES]
                tiles[c0 + j * LANES] = (yj * c + pltpu.roll(yj, ROPE_HALF, 1) * s_lo
                                         + pltpu.roll(yj, LANES - ROPE_HALF, 1) * s_hi)
        for c0 in range(2 * MIX, 3 * MIX, LANES):
            tiles[c0] = acc[:, c0:c0 + LANES]
        for col0, nat_ref, scr in ((0, qa_ref, scr_q), (MIX, ka_ref, scr_k), (2 * MIX, va_ref, scr_v),
                                   (3 * MIX, qb_ref, None)):
            for ct in range(n_ct):
                tile = tiles[col0 + ct * LANES]
                nat_ref[rows, ct * LANES:(ct + 1) * LANES] = tile.astype(BF16)
                if scr is not None:
                    scr[ct, rows, :] = tile
        kb_ref[rows, :] = tiles[4 * MIX].astype(BF16)
        vb_ref[rows, :] = acc[:, 4 * MIX + KV_B:].astype(BF16)

    def regroup(scr, ref4, ref16):
        q4 = tm // 4
        for r4 in range(4):
            got = [scr[ct, pl.ds(r4, q4, stride=4), :] for ct in range(n_ct)]
            ref4[r4] = jnp.concatenate(got, axis=1).astype(BF16)
            for ct in range(n_ct):
                scr4[ct, r4 * q4:(r4 + 1) * q4, :] = got[ct]
        for r16 in range(16):
            first = (r16 % 4) * q4 + r16 // 4
            got = [scr4[ct, pl.ds(first, tm // 16, stride=4), :] for ct in range(n_ct)]
            ref16[r16] = jnp.concatenate(got, axis=1).astype(BF16)

    regroup(scr_q, qa4_ref, qa16_ref)
    regroup(scr_k, ka4_ref, ka16_ref)
    regroup(scr_v, va4_ref, va16_ref)


def _proj(x2d, b, gain, w_bf16, rope_tab, e3, bd, head_gains):
    t, d = x2d.shape
    s = t // b
    pw = w_bf16.shape[1]
    tm = ROW_TILE
    nt = s // tm
    row = lambda bi, i: (bi * nt + i, 0)
    fixed = lambda bi, i: (0, 0)
    by_class = lambda bi, i: (bi, 0, i, 0)
    nat = lambda w: (jax.ShapeDtypeStruct((t, w), BF16), pl.BlockSpec((tm, w), row))
    grouped = lambda dil: (jax.ShapeDtypeStruct((b, dil, s // dil, MIX), BF16),
                           pl.BlockSpec((None, dil, tm // dil, MIX), by_class))
    outs = [nat(MIX)] * 3 + [grouped(4)] * 3 + [grouped(16)] * 3 + [nat(MIX), nat(KV_B), nat(KV_B)]
    return pl.pallas_call(
        _proj_kernel,
        grid=(b, nt),
        in_specs=[
            pl.BlockSpec((tm, d), row),
            pl.BlockSpec((1, d), fixed),
            pl.BlockSpec((d, pw), fixed),
            pl.BlockSpec((tm, LANES), row),
            pl.BlockSpec(e3.shape, fixed),
            pl.BlockSpec(bd.shape, fixed),
            pl.BlockSpec(head_gains.shape, fixed),
        ],
        out_specs=[o[1] for o in outs],
        out_shape=[o[0] for o in outs],
        scratch_shapes=[pltpu.VMEM((MIX // LANES, tm, LANES), F32)] * 4,
        compiler_params=pltpu.CompilerParams(
            dimension_semantics=("arbitrary", "arbitrary"), vmem_limit_bytes=VMEM_LIMIT),
        name="proj",
    )(x2d, gain, w_bf16, rope_tab, e3, bd, head_gains)


def _band(min_back, first_key):
    qi = lax.broadcasted_iota(jnp.int32, (BLOCK, 2 * BLOCK), 0)
    kj = lax.broadcasted_iota(jnp.int32, (BLOCK, 2 * BLOCK), 1)
    valid = (kj >= qi + min_back) & (kj <= qi + BLOCK)
    return valid if first_key is None else valid & (kj >= first_key)


def _head_scores(q_pair, kcat, sel, valid):
    qm = jnp.where(sel, q_pair, jnp.zeros_like(q_pair))
    s = lax.dot_general(qm, kcat, (((1,), (1,)), ((), ())), preferred_element_type=F32)
    return jnp.where(valid, s, -jnp.inf)


def _weighted_values(pr, v_pair, sel):
    return _dot(pr.astype(BF16), jnp.where(sel, v_pair, jnp.ones_like(v_pair)))


def _normalise_pair(res_even, res_odd, even, extra=None):
    num = jnp.where(even, res_even, res_odd)
    den = pltpu.roll(jnp.where(even, res_odd, res_even), HEAD_DIM, 1)
    return num / (den if extra is None else den + extra)


def _with_halo(prev, cur):
    return jnp.concatenate([prev, cur], axis=0)


def _swa_kernel(q_ref, kp_ref, k_ref, vp_ref, v_ref, sink_ref, o_ref, m_st, r_st):
    first_key = jnp.where(pl.program_id(1) > 0, 0, BLOCK)
    band = _band(BLOCK - SWA_BACK, None)
    band_first = _band(BLOCK - SWA_BACK, first_key)
    lane = lax.broadcasted_iota(jnp.int32, (1, LANES), 1)
    even = lane < HEAD_DIM
    n_tiles = SWA_STEP // BLOCK
    for j in range(n_tiles):
        rows = slice(j * BLOCK, (j + 1) * BLOCK)
        valid = band_first if j == 0 else band
        kcat = _with_halo(kp_ref[...] if j == 0 else k_ref[j - 1], k_ref[j])
        vcat = _with_halo(vp_ref[...] if j == 0 else v_ref[j - 1], v_ref[j])
        for p in range(N_HEADS // 2):
            q_pair = q_ref[j, :, p * PAIR:(p + 1) * PAIR]
            for half in range(2):
                h = 2 * p + half
                sel = even if half == 0 else jnp.logical_not(even)
                s = _head_scores(q_pair, kcat, sel, valid)
                m = jnp.maximum(jnp.max(s, axis=1, keepdims=True), sink_ref[h])
                m_st[h, rows, :] = jnp.broadcast_to(m, (BLOCK, LANES))
                r_st[h, rows, :] = _weighted_values(jnp.exp2(s - m), vcat, sel)

    def finish(j, carry):
        rows = pl.ds(pl.multiple_of(j * BLOCK, BLOCK), BLOCK)
        for p in range(N_HEADS // 2):
            sink_pair = jnp.where(even, sink_ref[2 * p], sink_ref[2 * p + 1])
            m_pair = jnp.where(even, m_st[2 * p, rows, :], m_st[2 * p + 1, rows, :])
            out = _normalise_pair(r_st[2 * p, rows, :], r_st[2 * p + 1, rows, :], even,
                                  jnp.exp2(sink_pair - m_pair))
            o_ref[j, :, p * PAIR:(p + 1) * PAIR] = out.astype(o_ref.dtype)
        return carry

    lax.fori_loop(0, n_tiles, finish, 0, unroll=2)


def _swa_attention(q, k, v, sink):
    b, nb = q.shape[:2]
    step = SWA_STEP // BLOCK
    cur = lambda w: pl.BlockSpec((None, step, BLOCK, w), lambda bi, c: (bi, c, 0, 0))
    prev = pl.BlockSpec((None, None, BLOCK, KV_B),
                        lambda bi, c: (bi, jnp.maximum(step * c - 1, 0), 0, 0))
    return pl.pallas_call(
        _swa_kernel,
        grid=(b, nb // step),
        in_specs=[cur(MIX), prev, cur(KV_B), prev, cur(KV_B), pl.BlockSpec(memory_space=pltpu.SMEM)],
        out_specs=cur(MIX),
        out_shape=jax.ShapeDtypeStruct(q.shape, BF16),
        scratch_shapes=[pltpu.VMEM((N_HEADS, SWA_STEP, LANES), F32)] * 2,
        compiler_params=pltpu.CompilerParams(
            dimension_semantics=("arbitrary", "arbitrary"), vmem_limit_bytes=VMEM_LIMIT),
        name="swa",
    )(q, k, k, v, v, sink)


def _dilated_kernel(qn, knp, kn, vnp, vn, q4, k4p, k4, v4p, v4, q16, k16p, k16, v16p, v16,
                    o_ref, m_st, r_st):
    n_pairs = HALF_W // PAIR
    first_key = jnp.where(pl.program_id(1) > 0, 0, BLOCK)
    band = _band(0, None)
    band_first = _band(0, first_key)
    lane = lax.broadcasted_iota(jnp.int32, (1, LANES), 1)
    even = lane < HEAD_DIM

    def tile(q, kcat, vcat, valid, row0, stride, init):
        rows = pl.ds(row0, BLOCK) if stride == 1 else pl.ds(row0, BLOCK, stride=stride)
        for p in range(n_pairs):
            cols = slice(p * PAIR, (p + 1) * PAIR)
            for half in range(2):
                h = 2 * p + half
                sel = even if half == 0 else jnp.logical_not(even)
                s = _head_scores(q[:, cols], kcat[:, cols], sel, valid)
                m_tile = jnp.max(s, axis=1, keepdims=True)
                if init:
                    m_new = jnp.broadcast_to(m_tile, (BLOCK, LANES))
                    res = _weighted_values(jnp.exp2(s - m_tile), vcat[:, cols], sel)
                else:
                    m_old = m_st[h, rows, :]
                    m_new = jnp.maximum(m_old, m_tile)
                    pr = jnp.exp2(s - jnp.concatenate([m_new, m_new], axis=1))
                    res = (_weighted_values(pr, vcat[:, cols], sel)
                           + jnp.exp2(m_old - m_new) * r_st[h, rows, :])
                m_st[h, rows, :] = m_new
                r_st[h, rows, :] = res

    def d16(r, carry):
        tile(q16[r], _with_halo(k16p[r], k16[r]), _with_halo(v16p[r], v16[r]), band_first, r, 16, True)
        return carry

    lax.fori_loop(0, 16, d16, 0, unroll=True)

    tile(qn[0], _with_halo(knp[...], kn[0]), _with_halo(vnp[...], vn[0]), band_first, 0, 1, False)

    def d1(j, carry):
        tile(qn[j], _with_halo(kn[j - 1], kn[j]), _with_halo(vn[j - 1], vn[j]),
             band, pl.multiple_of(j * BLOCK, BLOCK), 1, False)
        return carry

    lax.fori_loop(1, CHUNK // BLOCK, d1, 0, unroll=True)

    def d4(r, carry):
        tile(q4[r, 0], _with_halo(k4p[r], k4[r, 0]), _with_halo(v4p[r], v4[r, 0]), band_first, r, 4, False)
        for j in range(1, CHUNK // (4 * BLOCK)):
            tile(q4[r, j], _with_halo(k4[r, j - 1], k4[r, j]), _with_halo(v4[r, j - 1], v4[r, j]),
                 band, r + 4 * j * BLOCK, 4, False)
        return carry

    lax.fori_loop(0, 4, d4, 0, unroll=True)

    def finish(j, carry):
        rows = pl.ds(pl.multiple_of(j * BLOCK, BLOCK), BLOCK)
        for p in range(n_pairs):
            out = _normalise_pair(r_st[2 * p, rows, :], r_st[2 * p + 1, rows, :], even)
            o_ref[j, :, p * PAIR:(p + 1) * PAIR] = out.astype(o_ref.dtype)
        return carry

    lax.fori_loop(0, CHUNK // BLOCK, finish, 0, unroll=4)


def _dilated_attention(nat, by4, by16):
    b, nb = nat[0].shape[:2]
    per_chunk = CHUNK // BLOCK
    n_chunks = nb // per_chunk

    def specs(dil):
        step = per_chunk // dil
        if dil == 1:
            cur = pl.BlockSpec((None, step, BLOCK, HALF_W), lambda bi, c, hf: (bi, c, 0, hf))
            prev = pl.BlockSpec((None, None, BLOCK, HALF_W),
                                lambda bi, c, hf: (bi, jnp.maximum(step * c - 1, 0), 0, hf))
        else:
            blocks = None if step == 1 else step
            cur = pl.BlockSpec((None, dil, blocks, BLOCK, HALF_W), lambda bi, c, hf: (bi, 0, c, 0, hf))
            prev = pl.BlockSpec((None, dil, None, BLOCK, HALF_W),
                                lambda bi, c, hf: (bi, 0, jnp.maximum(step * c - 1, 0), 0, hf))
        return cur, prev

    args, in_specs = [], []
    for (q, k, v), dil in ((nat, 1), (by4, 4), (by16, 16)):
        cur, prev = specs(dil)
        args += [q, k, k, v, v]
        in_specs += [cur, prev, cur, prev, cur]
    return pl.pallas_call(
        _dilated_kernel,
        grid=(b, n_chunks, MIX // HALF_W),
        in_specs=in_specs,
        out_specs=specs(1)[0],
        out_shape=jax.ShapeDtypeStruct(nat[0].shape, BF16),
        scratch_shapes=[pltpu.VMEM((HALF_W // HEAD_DIM, CHUNK, LANES), F32),
                        pltpu.VMEM((HALF_W // HEAD_DIM, CHUNK, LANES), F32)],
        compiler_params=pltpu.CompilerParams(
            dimension_semantics=("arbitrary", "arbitrary", "arbitrary"),
            vmem_limit_bytes=VMEM_LIMIT),
        name="dilated",
    )(*args)


def _route(lt, tri, base):
    n = lt.shape[1]
    row8 = lax.broadcasted_iota(jnp.int32, (SUBLANES, n), 0)
    far = jnp.int32(LANES)
    neg = -jnp.inf

    def first_argmax(vals):
        mx = jnp.max(vals, axis=0, keepdims=True)
        idx = jnp.min(jnp.where(vals == mx, row8, far), axis=0, keepdims=True)
        return mx, idx

    gl = jnp.where(row8 < N_GROUPS, lt[:SUBLANES, :], neg)
    gmax, gidx = first_argmax(gl)
    g_top = 1.0 / jnp.sum(jnp.exp(gl - gmax), axis=0, keepdims=True)

    el = lt[EXPERT_LANE0:EXPERT_LANE0 + EXPERTS_PER_GROUP, :]
    for g in range(1, N_GROUPS):
        r0 = EXPERT_LANE0 + g * EXPERTS_PER_GROUP
        el = jnp.where(gidx == g, lt[r0:r0 + EXPERTS_PER_GROUP, :], el)
    m1, i1 = first_argmax(el)
    m2, i2 = first_argmax(jnp.where(row8 == i1, neg, el))
    t2 = jnp.exp(m2 - m1)
    gate0 = g_top / (1.0 + t2)
    gate1 = g_top * t2 / (1.0 + t2)
    e0 = gidx * EXPERTS_PER_GROUP + i1
    e1 = gidx * EXPERTS_PER_GROUP + i2

    row_e = lax.broadcasted_iota(jnp.int32, (N_EXPERTS, n), 0)
    oh0 = row_e == e0
    oh1 = row_e == e1
    cum = _dot(jnp.concatenate([oh0.astype(BF16), oh1.astype(BF16)], axis=0), tri)
    oh0 = oh0.astype(F32)
    oh1 = oh1.astype(F32)
    tot0 = jnp.sum(oh0, axis=1, keepdims=True)
    tot1 = jnp.sum(oh1, axis=1, keepdims=True)
    base_n = jnp.concatenate([base] * (n // LANES), axis=1)
    rank0 = jnp.sum(oh0 * (cum[:N_EXPERTS] + base_n), axis=0, keepdims=True)
    rank1 = jnp.sum(oh1 * (cum[N_EXPERTS:] + (base_n + tot0)), axis=0, keepdims=True)

    meta_t = jnp.zeros((SUBLANES, n), F32)
    for r, val in ((META_E0, e0.astype(F32)), (META_E1, e1.astype(F32)), (META_G0, gate0),
                   (META_G1, gate1), (META_R0, rank0), (META_R1, rank1)):
        meta_t = jnp.where(row8 == r, val, meta_t)
    return meta_t, base + (tot0 + tot1)


def _out_router_kernel(oa_ref, ob_ref, x_ref, ga_ref, gb_ref, wo_ref, gf_ref, wr_ref, tri_ref,
                       x1_ref, h_ref, meta_ref, meta_t_ref, cnt_ref, carry_ref):
    @pl.when(pl.program_id(0) == 0)
    def _():
        carry_ref[...] = jnp.zeros_like(carry_ref)

    tm, d = x_ref.shape
    tok = d // (2 * LANES)
    blocks = [slice(r0, r0 + ROUTE_BLOCK) for r0 in range(0, tm, ROUTE_BLOCK)]
    attn = []
    for rows in blocks:
        na = _rms_rows(oa_ref[rows, :].astype(F32)) * ga_ref[...]
        nb = _rms_rows(ob_ref[rows, :].astype(F32)) * gb_ref[...]
        attn.append(_dot(jnp.concatenate([na, nb], axis=1).astype(BF16), wo_ref[...]))
    logits = []
    for rows, delta in zip(blocks, attn):
        x1 = x_ref[rows, :] + delta
        x1_ref[rows, :] = x1
        h = _rms_rows(x1) * gf_ref[...]
        _store_token_tiles(h_ref.at[rows.start * tok:rows.stop * tok], _pack_bf16_pairs(h))
        h_hi, h_lo = _split2(h)
        parts = _dot(h_hi, wr_ref[...]) + _dot(h_lo, wr_ref[...])
        logits.append(parts[:, :LANES] + parts[:, LANES:])

    lt = jnp.concatenate(logits, axis=0).T
    meta_t, base = _route(lt, tri_ref[...], carry_ref[...])
    carry_ref[...] = base
    cnt_ref[...] = base
    meta_t_ref[...] = meta_t
    meta_ref[...] = jnp.concatenate([meta_t, jnp.zeros((LANES - SUBLANES, tm), F32)], axis=0).T


def _out_router(oa, ob, x2d, ga, gb, wo_bf16, gf, wr, tri):
    t, d = x2d.shape
    tm = ROW_TILE
    tok = d // (2 * LANES)
    row = lambda i: (i, 0)
    fixed = lambda i: (0, 0)
    return pl.pallas_call(
        _out_router_kernel,
        grid=(t // tm,),
        in_specs=[
            pl.BlockSpec((tm, MIX), row),
            pl.BlockSpec((tm, MIX), row),
            pl.BlockSpec((tm, d), row),
            pl.BlockSpec((1, MIX), fixed),
            pl.BlockSpec((1, MIX), fixed),
            pl.BlockSpec((2 * MIX, d), fixed),
            pl.BlockSpec((1, d), fixed),
            pl.BlockSpec((d, 2 * LANES), fixed),
            pl.BlockSpec((tm, tm), fixed),
        ],
        out_specs=[
            pl.BlockSpec((tm, d), row),
            pl.BlockSpec((tm * tok, LANES), row),
            pl.BlockSpec((tm, LANES), row),
            pl.BlockSpec((None, SUBLANES, tm), lambda i: (i, 0, 0)),
            pl.BlockSpec((N_EXPERTS, LANES), fixed),
        ],
        out_shape=[
            jax.ShapeDtypeStruct((t, d), F32),
            jax.ShapeDtypeStruct((t * tok, LANES), jnp.uint32),
            jax.ShapeDtypeStruct((t, LANES), F32),
            jax.ShapeDtypeStruct((t // tm, SUBLANES, tm), F32),
            jax.ShapeDtypeStruct((N_EXPERTS, LANES), F32),
        ],
        scratch_shapes=[pltpu.VMEM((N_EXPERTS, LANES), F32)],
        compiler_params=pltpu.CompilerParams(
            dimension_semantics=("arbitrary",), vmem_limit_bytes=VMEM_LIMIT),
        name="out_router",
    )(oa, ob, x2d, ga, gb, wo_bf16, gf, wr, tri)


def _token_copy(src, i, dst, j, tok, sem):
    return pltpu.make_async_copy(src.at[pl.ds(pl.multiple_of(i * tok, tok), tok)],
                                 dst.at[pl.ds(pl.multiple_of(j * tok, tok), tok)], sem)


def _dispatch_kernel(slot_ref, h_ref, xs_in_ref, xs_ref, sem, *, tok):
    del xs_in_ref
    n_sub = slot_ref.shape[0]
    tm = slot_ref.shape[2] // TOP_K

    for r in range(n_sub):
        def start(blk, c, src=h_ref.at[pl.ds(r * tm * tok, tm * tok)], slots=slot_ref.at[r]):
            for u in range(DMA_UNROLL):
                i = blk * DMA_UNROLL + u
                for k in range(TOP_K):
                    _token_copy(src, i, xs_ref, slots[0, k * tm + i], tok, sem).start(priority=k)
            return c

        lax.fori_loop(0, tm // DMA_UNROLL, start, 0)
    for k in range(TOP_K):
        pltpu.make_async_copy(h_ref, xs_ref.at[pl.ds(0, n_sub * tm * tok)], sem).wait()


def _dispatch(slots, h_tok, xs_buf, tok):
    tm = ROW_TILE
    n_sub = DISPATCH_TILES
    n_tiles = h_tok.shape[0] // (n_sub * tm * tok)
    return pl.pallas_call(
        functools.partial(_dispatch_kernel, tok=tok),
        grid=(n_tiles,),
        in_specs=[
            pl.BlockSpec((n_sub, 1, TOP_K * tm), lambda i: (i, 0, 0), memory_space=pltpu.SMEM),
            pl.BlockSpec((n_sub * tm * tok, LANES), lambda i: (i, 0)),
            pl.BlockSpec(memory_space=pl.ANY),
        ],
        out_specs=pl.BlockSpec(memory_space=pl.ANY),
        out_shape=jax.ShapeDtypeStruct(xs_buf.shape, xs_buf.dtype),
        scratch_shapes=[pltpu.SemaphoreType.DMA(())],
        input_output_aliases={2: 0},
        compiler_params=pltpu.CompilerParams(
            dimension_semantics=("arbitrary",), vmem_limit_bytes=VMEM_LIMIT),
        name="dispatch",
    )(slots, h_tok, xs_buf)


def _expert_kernel(te_ref, nv_ref, xs_ref, wgu_ref, wdn_ref, ys_ref, wgu_bf, wdn_bf):
    i = pl.program_id(0)
    d = wgu_ref.shape[0]
    tok = d // (2 * LANES)
    tg = xs_ref.shape[0] // tok

    @pl.when((i == 0) | (te_ref[i] != te_ref[jnp.maximum(i - 1, 0)]))
    def _():
        wgu_bf[...] = wgu_ref[...].astype(BF16)
        wdn_bf[...] = wdn_ref[...].astype(BF16)

    @pl.when(i < nv_ref[0])
    def _():
        xs = _unpack_bf16_pairs(_load_token_tiles(xs_ref, tg, tok))
        gu = _dot(xs.astype(BF16), wgu_bf[...])
        g = gu[:, :D_EXPERT]
        a = (g * (1.0 / (1.0 + jnp.exp(-g)))) * gu[:, D_EXPERT:]
        _store_token_tiles(ys_ref, _pack_bf16_pairs(_dot(a.astype(BF16), wdn_bf[...])))

    @pl.when(i >= nv_ref[0])
    def _():
        ys_ref[...] = jnp.zeros_like(ys_ref)


def _experts(tile_expert, n_valid, xs, wgu, wdn, layer):
    d = wgu.shape[2]
    tok = d // (2 * LANES)
    tg = GROUP_TILE
    grid_spec = pltpu.PrefetchScalarGridSpec(
        num_scalar_prefetch=2,
        grid=(xs.shape[0] // (tg * tok),),
        in_specs=[
            pl.BlockSpec((tg * tok, LANES), lambda i, te, nv: (i, 0)),
            pl.BlockSpec((None, None, d, 2 * D_EXPERT), lambda i, te, nv: (layer, te[i], 0, 0)),
            pl.BlockSpec((None, None, D_EXPERT, d), lambda i, te, nv: (layer, te[i], 0, 0)),
        ],
        out_specs=pl.BlockSpec((tg * tok, LANES), lambda i, te, nv: (i, 0)),
        scratch_shapes=[pltpu.VMEM((d, 2 * D_EXPERT), BF16), pltpu.VMEM((D_EXPERT, d), BF16)],
    )
    return pl.pallas_call(
        _expert_kernel,
        grid_spec=grid_spec,
        out_shape=jax.ShapeDtypeStruct(xs.shape, xs.dtype),
        compiler_params=pltpu.CompilerParams(
            dimension_semantics=("arbitrary",), vmem_limit_bytes=VMEM_LIMIT),
        name="experts",
    )(tile_expert, n_valid, xs, wgu, wdn)


def _combine_kernel(slot_ref, slot_next_ref, x_ref, meta_ref, ys_ref, out_ref, buf, sems):
    step = pl.program_id(0)
    tm, d = x_ref.shape
    tok = d // (2 * LANES)

    def gather(slots, half):
        def start(blk, c):
            for u in range(DMA_UNROLL):
                i = blk * DMA_UNROLL + u
                for k in range(TOP_K):
                    _token_copy(ys_ref, slots[0, k * tm + i], buf.at[half, k], i, tok,
                                sems.at[half]).start(priority=k)
            return c

        lax.fori_loop(0, tm // DMA_UNROLL, start, 0)

    cur = step % 2

    @pl.when(step == 0)
    def _():
        gather(slot_ref, 0)

    @pl.when(step + 1 < pl.num_programs(0))
    def _():
        gather(slot_next_ref, 1 - cur)

    for k in range(TOP_K):
        pltpu.make_async_copy(ys_ref.at[pl.ds(0, tm * tok)], buf.at[cur, k], sems.at[cur]).wait()
    meta = meta_ref[...]
    lane = lax.broadcasted_iota(jnp.int32, (1, LANES), 1)
    g0 = _lane_col(meta, lane, META_G0, 0.0, jnp.sum)
    g1 = _lane_col(meta, lane, META_G1, 0.0, jnp.sum)
    y = (g0 * _unpack_bf16_pairs(_load_token_tiles(buf.at[cur, 0], tm, tok))
         + g1 * _unpack_bf16_pairs(_load_token_tiles(buf.at[cur, 1], tm, tok)))
    out_ref[...] = x_ref[...] + y


def _combine(slots, x1, meta, ys):
    t, d = x1.shape
    tm = ROW_TILE
    tok = d // (2 * LANES)
    n_tiles = t // tm
    return pl.pallas_call(
        _combine_kernel,
        grid=(n_tiles,),
        in_specs=[
            pl.BlockSpec((None, 1, TOP_K * tm), lambda i: (i, 0, 0), memory_space=pltpu.SMEM),
            pl.BlockSpec((None, 1, TOP_K * tm), lambda i: (jnp.minimum(i + 1, n_tiles - 1), 0, 0),
                         memory_space=pltpu.SMEM),
            pl.BlockSpec((tm, d), lambda i: (i, 0)),
            pl.BlockSpec((tm, LANES), lambda i: (i, 0)),
            pl.BlockSpec(memory_space=pl.ANY),
        ],
        out_specs=pl.BlockSpec((tm, d), lambda i: (i, 0)),
        out_shape=jax.ShapeDtypeStruct((t, d), F32),
        scratch_shapes=[pltpu.VMEM((2, TOP_K, tm * tok, LANES), ys.dtype),
                        pltpu.SemaphoreType.DMA((2,))],
        compiler_params=pltpu.CompilerParams(
            dimension_semantics=("arbitrary",), vmem_limit_bytes=VMEM_LIMIT),
        name="combine",
    )(slots, slots, x1, meta, ys)


def _rope_expand_matrix():
    src = jnp.arange(LANES)[:, None]
    dst = jnp.arange(LANES)[None, :]
    in_head = dst % HEAD_DIM
    cos_m = (src < ROPE_HALF) & (in_head < ROPE_DIM) & (in_head % ROPE_HALF == src)
    is_sin = (src >= ROPE_HALF) & (src < ROPE_DIM)
    sin_lo = is_sin & (in_head >= ROPE_HALF) & (in_head < ROPE_DIM) & (in_head - ROPE_HALF == src - ROPE_HALF)
    sin_hi = is_sin & (in_head < ROPE_HALF) & (in_head == src - ROPE_HALF)
    e = jnp.concatenate([cos_m.astype(F32), sin_lo.astype(F32), -sin_hi.astype(F32)], axis=1)
    return jnp.concatenate([e, e, e], axis=0).astype(BF16)


def _block_diag_ones():
    i = jnp.arange(2 * LANES)
    return (i[:, None] // HEAD_DIM == i[None, :] // HEAD_DIM).astype(BF16)


def _tile4(g):
    return jnp.tile(g.astype(F32), 2 * LANES // HEAD_DIM)


_B_HEAD_ORDER = tuple(h for p in range(N_HEADS // 2) for h in (p, p + N_HEADS // 2))


def _b_cols():
    order = jnp.asarray(_B_HEAD_ORDER)
    return (order[:, None] * HEAD_DIM + jnp.arange(HEAD_DIM)[None, :]).reshape(-1)


def kernel(x, positions, attn_norm, w_in, q_norm_a, k_norm_a, q_norm_b, k_norm_b, sinks_b,
           out_norm_a, out_norm_b, w_out, ffn_norm, w_router_group, w_router_expert,
           w_gate_up, w_down):
    b, s, d = x.shape
    t = b * s
    depth = w_in.shape[0]
    tok = d // (2 * LANES)
    assert s % CHUNK == 0 and s % ROW_TILE == 0 and s % SWA_STEP == 0
    assert ROW_TILE % (2 * SUBLANES * max(DILATIONS)) == 0 and ROW_TILE % ROUTE_BLOCK == 0 and ROW_TILE % PROJ_BLOCK == 0
    assert d == 2 * MIX and d % (2 * LANES) == 0 and t % (DISPATCH_TILES * ROW_TILE) == 0

    inv_freq = ROPE_THETA ** (-jnp.arange(0, ROPE_DIM, 2, dtype=F32) / ROPE_DIM)
    ang = positions.astype(F32)[None, :, :] * inv_freq[:, None, None]
    cos_sin = jnp.moveaxis(jnp.concatenate([jnp.cos(ang), jnp.sin(ang)], axis=0), 0, -1)
    rope_tab = jnp.pad(cos_sin, ((0, 0), (0, 0), (0, LANES - ROPE_DIM))).reshape(t, LANES)
    e3 = _rope_expand_matrix()
    bd = _block_diag_ones()
    row_i = jnp.arange(ROW_TILE)
    tri = (row_i[:, None] < row_i[None, :]).astype(BF16)
    bcols = _b_cols()
    q_scale = HEAD_DIM ** -0.5 * LOG2E
    n_sorted = t * TOP_K + N_EXPERTS * GROUP_TILE
    n_tiles = n_sorted // GROUP_TILE
    xs_buf = jnp.zeros((n_sorted * tok, LANES), jnp.uint32)
    nb = s // BLOCK

    x2d = x.reshape(t, d)
    for l in range(depth):
        qb0 = 3 * MIX
        w_l = w_in[l]
        w_l = jnp.concatenate([w_l[:, :qb0], w_l[:, qb0:qb0 + MIX][:, bcols], w_l[:, qb0 + MIX:]], axis=1)
        head_gains = jnp.stack([_tile4(q_norm_a[l]) * q_scale, _tile4(k_norm_a[l]),
                                _tile4(q_norm_b[l]) * q_scale, _tile4(k_norm_b[l])])
        sink = sinks_b[l][jnp.asarray(_B_HEAD_ORDER)].astype(F32) * LOG2E
        gb_perm = out_norm_b[l][bcols]
        wo_l = jnp.concatenate([w_out[l][:MIX], w_out[l][MIX:][bcols]], axis=0).astype(BF16)
        wr = jnp.concatenate(
            [w_router_group[l], jnp.zeros((d, EXPERT_LANE0 - N_GROUPS), F32), w_router_expert[l],
             jnp.zeros((d, LANES - EXPERT_LANE0 - N_EXPERTS), F32)], axis=1)
        wr_hi = wr.astype(BF16)
        wr_split = jnp.concatenate([wr_hi, (wr - wr_hi.astype(F32)).astype(BF16)], axis=1)

        (qa, ka, va, qa4, ka4, va4, qa16, ka16, va16, qb, kb, vb) = _proj(
            x2d, b, attn_norm[l].reshape(1, d), w_l.astype(BF16), rope_tab, e3, bd, head_gains)
        blocks = lambda a: a.reshape(b, nb, BLOCK, a.shape[-1])
        class_blocks = lambda a: a.reshape(b, a.shape[1], a.shape[2] // BLOCK, BLOCK, MIX)
        oa = _dilated_attention(tuple(map(blocks, (qa, ka, va))),
                                tuple(map(class_blocks, (qa4, ka4, va4))),
                                tuple(map(class_blocks, (qa16, ka16, va16))))
        ob = _swa_attention(blocks(qb), blocks(kb), blocks(vb), sink)

        x1, h_tok, meta, meta_t, counts = _out_router(
            oa.reshape(t, MIX), ob.reshape(t, MIX), x2d, out_norm_a[l].reshape(1, MIX),
            gb_perm.reshape(1, MIX), wo_l, ffn_norm[l].reshape(1, d), wr_split, tri)

        cnt = counts[:, 0].astype(jnp.int32)
        padded = ((cnt + GROUP_TILE - 1) // GROUP_TILE) * GROUP_TILE
        ends = jnp.cumsum(padded)
        offs = ends - padded
        eid = meta_t[:, META_E0:META_E1 + 1, :].astype(jnp.int32)
        rank = meta_t[:, META_R0:META_R1 + 1, :].astype(jnp.int32)
        is_expert = eid[..., None] == jnp.arange(N_EXPERTS, dtype=jnp.int32)
        slots = jnp.sum(jnp.where(is_expert, offs, 0), axis=-1) + rank
        slots = slots.reshape(t // ROW_TILE, 1, TOP_K * ROW_TILE)
        tile_start = jnp.arange(n_tiles, dtype=jnp.int32) * GROUP_TILE
        tile_expert = jnp.minimum(
            jnp.sum((tile_start[:, None] >= ends[None, :]).astype(jnp.int32), axis=1), N_EXPERTS - 1)
        n_valid = (ends[-1:] // GROUP_TILE).astype(jnp.int32)

        xs_buf = _dispatch(slots, h_tok, xs_buf, tok)
        ys = _experts(tile_expert, n_valid, xs_buf, w_gate_up, w_down, l)
        x2d = _combine(slots, x1, meta, ys)
    return x2d.reshape(b, s, d)
```

```python
import functools
import math

import jax
import jax.numpy as jnp
from jax import lax
from jax.experimental import pallas as pl
from jax.experimental.pallas import tpu as pltpu

F32 = jnp.float32
BF16 = jnp.bfloat16

HEAD_DIM = 64
N_HEADS = 8
MIX = N_HEADS * HEAD_DIM
N_KV_B = 2
KV_B = N_KV_B * HEAD_DIM
DILATIONS = (1, 4, 16)
BLOCK = 128
CHUNK = BLOCK * max(DILATIONS)
SWA_BACK = 127
SWA_STEP = 8 * BLOCK
ROPE_DIM = HEAD_DIM // 4
ROPE_HALF = ROPE_DIM // 2
ROPE_THETA = 500000.0
N_GROUPS = 4
EXPERTS_PER_GROUP = 8
N_EXPERTS = N_GROUPS * EXPERTS_PER_GROUP
TOP_K = 2
D_EXPERT = 512
EPS = 1e-6
LOG2E = math.log2(math.e)

LANES = 128
SUBLANES = 8
PAIR = 2 * HEAD_DIM
HALF_W = MIX // 2
EXPERT_LANE0 = 32
ROW_TILE = 512
PROJ_BLOCK = 256
ROUTE_BLOCK = 128
GROUP_TILE = 512
DMA_UNROLL = 8
DISPATCH_TILES = 8
VMEM_LIMIT = 48 * 1024 * 1024

META_E0, META_E1, META_G0, META_G1, META_R0, META_R1 = 0, 1, 2, 3, 4, 5


def _split2(a):
    hi = a.astype(BF16)
    lo = (a - hi.astype(F32)).astype(BF16)
    return hi, lo


def _split3(a):
    hi = a.astype(BF16)
    r = a - hi.astype(F32)
    mid = r.astype(BF16)
    lo = (r - mid.astype(F32)).astype(BF16)
    return hi, mid, lo


def _dot(a, b):
    return jnp.dot(a, b, preferred_element_type=F32)


def _rms_rows(a):
    return a * lax.rsqrt(jnp.mean(a * a, axis=-1, keepdims=True) + EPS)


def _lane_col(tile, lane, idx, fill, reduce):
    return reduce(jnp.where(lane == idx, tile, fill), axis=1, keepdims=True)


BF16_BITS = 16
HIGH_HALF = (1 << 32) - (1 << BF16_BITS)
ROUND_HALF = 1 << (BF16_BITS - 1)


def _pack_bf16_pairs(val):
    w = val.shape[1] // 2
    bits = lax.bitcast_convert_type(val, jnp.uint32) + jnp.uint32(ROUND_HALF)
    return (bits[:, :w] >> BF16_BITS) | (bits[:, w:] & jnp.uint32(HIGH_HALF))


def _unpack_bf16_pairs(words):
    lo = lax.bitcast_convert_type(words << BF16_BITS, F32)
    hi = lax.bitcast_convert_type(words & jnp.uint32(HIGH_HALF), F32)
    return jnp.concatenate([lo, hi], axis=1)


def _store_token_tiles(ref, val):
    rows, width = val.shape
    n = width // LANES
    for c in range(n):
        ref[pl.ds(c, rows, stride=n), :] = val[:, c * LANES:(c + 1) * LANES]


def _load_token_tiles(ref, rows, n):
    return jnp.concatenate([ref[pl.ds(c, rows, stride=n), :] for c in range(n)], axis=1)


def _proj_kernel(x_ref, g_ref, w_ref, rope_ref, e3_ref, bd_ref, hg_ref,
                 qa_ref, ka_ref, va_ref, qa4_ref, ka4_ref, va4_ref, qa16_ref, ka16_ref, va16_ref,
                 qb_ref, kb_ref, vb_ref, scr_q, scr_k, scr_v, scr4):
    tm = x_ref.shape[0]
    lane = lax.broadcasted_iota(jnp.int32, (1, LANES), 1)
    not_rope = ((lane & (HEAD_DIM - 1)) >= ROPE_DIM).astype(F32)
    bd = bd_ref[...]
    n_ct = MIX // LANES
    normed_groups = [(c0, min(2 * LANES, col0 + width - c0), g)
                     for col0, width, g in ((0, MIX, 0), (MIX, MIX, 1), (3 * MIX, MIX, 2), (4 * MIX, KV_B, 3))
                     for c0 in range(col0, col0 + width, 2 * LANES)]

    staged = []
    for r0 in range(0, tm, PROJ_BLOCK):
        rows = slice(r0, r0 + PROJ_BLOCK)
        h = _rms_rows(x_ref[rows, :]) * g_ref[...]
        acc = _dot(h.astype(BF16), w_ref[...])
        hi, mid, lo = _split3(rope_ref[rows, :])
        tab = _dot(jnp.concatenate([hi, mid, lo], axis=1), e3_ref[...])
        sums = {}
        for c0, wd, _ in normed_groups:
            a = acc[:, c0:c0 + wd]
            a2_hi, a2_lo = _split2(a * a)
            sums[c0] = _dot(a2_hi, bd[:wd, :wd]) + _dot(a2_lo, bd[:wd, :wd])
        staged.append((rows, acc, tab, sums))

    for rows, acc, tab, sums in staged:
        c = tab[:, :LANES] + not_rope
        s_lo = tab[:, LANES:2 * LANES]
        s_hi = tab[:, 2 * LANES:]
        tiles = {}
        for c0, wd, g in normed_groups:
            y = (acc[:, c0:c0 + wd] * lax.rsqrt(sums[c0] * (1.0 / HEAD_DIM) + EPS)) * hg_ref[g:g + 1, :wd]
            for j in range(wd // LANES):
                yj = y[:, j * LANES:(j + 1) * LANES]
                tiles[c0 + j * LANES] = (yj * c + pltpu.roll(yj, ROPE_HALF, 1) * s_lo
                                         + pltpu.roll(yj, LANES - ROPE_HALF, 1) * s_hi)
        for c0 in range(2 * MIX, 3 * MIX, LANES):
            tiles[c0] = acc[:, c0:c0 + LANES]
        for col0, nat_ref, scr in ((0, qa_ref, scr_q), (MIX, ka_ref, scr_k), (2 * MIX, va_ref, scr_v),
                                   (3 * MIX, qb_ref, None)):
            for ct in range(n_ct):
                tile = tiles[col0 + ct * LANES]
                nat_ref[rows, ct * LANES:(ct + 1) * LANES] = tile.astype(BF16)
                if scr is not None:
                    scr[ct, rows, :] = tile
        kb_ref[rows, :] = tiles[4 * MIX].astype(BF16)
        vb_ref[rows, :] = acc[:, 4 * MIX + KV_B:].astype(BF16)

    def regroup(scr, ref4, ref16):
        q4 = tm // 4
        for r4 in range(4):
            got = [scr[ct, pl.ds(r4, q4, stride=4), :] for ct in range(n_ct)]
            ref4[r4] = jnp.concatenate(got, axis=1).astype(BF16)
            for ct in range(n_ct):
                scr4[ct, r4 * q4:(r4 + 1) * q4, :] = got[ct]
        for r16 in range(16):
            first = (r16 % 4) * q4 + r16 // 4
            got = [scr4[ct, pl.ds(first, tm // 16, stride=4), :] for ct in range(n_ct)]
            ref16[r16] = jnp.concatenate(got, axis=1).astype(BF16)

    regroup(scr_q, qa4_ref, qa16_ref)
    regroup(scr_k, ka4_ref, ka16_ref)
    regroup(scr_v, va4_ref, va16_ref)


def _proj(x2d, b, gain, w_bf16, rope_tab, e3, bd, head_gains):
    t, d = x2d.shape
    s = t // b
    pw = w_bf16.shape[1]
    tm = ROW_TILE
    nt = s // tm
    row = lambda bi, i: (bi * nt + i, 0)
    fixed = lambda bi, i: (0, 0)
    by_class = lambda bi, i: (bi, 0, i, 0)
    nat = lambda w: (jax.ShapeDtypeStruct((t, w), BF16), pl.BlockSpec((tm, w), row))
    grouped = lambda dil: (jax.ShapeDtypeStruct((b, dil, s // dil, MIX), BF16),
                           pl.BlockSpec((None, dil, tm // dil, MIX), by_class))
    outs = [nat(MIX)] * 3 + [grouped(4)] * 3 + [grouped(16)] * 3 + [nat(MIX), nat(KV_B), nat(KV_B)]
    return pl.pallas_call(
        _proj_kernel,
        grid=(b, nt),
        in_specs=[
            pl.BlockSpec((tm, d), row),
            pl.BlockSpec((1, d), fixed),
            pl.BlockSpec((d, pw), fixed),
            pl.BlockSpec((tm, LANES), row),
            pl.BlockSpec(e3.shape, fixed),
            pl.BlockSpec(bd.shape, fixed),
            pl.BlockSpec(head_gains.shape, fixed),
        ],
        out_specs=[o[1] for o in outs],
        out_shape=[o[0] for o in outs],
        scratch_shapes=[pltpu.VMEM((MIX // LANES, tm, LANES), F32)] * 4,
        compiler_params=pltpu.CompilerParams(
            dimension_semantics=("arbitrary", "arbitrary"), vmem_limit_bytes=VMEM_LIMIT),
        name="proj",
    )(x2d, gain, w_bf16, rope_tab, e3, bd, head_gains)


def _band(min_back, first_key):
    qi = lax.broadcasted_iota(jnp.int32, (BLOCK, 2 * BLOCK), 0)
    kj = lax.broadcasted_iota(jnp.int32, (BLOCK, 2 * BLOCK), 1)
    valid = (kj >= qi + min_back) & (kj <= qi + BLOCK)
    return valid if first_key is None else valid & (kj >= first_key)


def _head_scores(q_pair, kcat, sel, valid):
    qm = jnp.where(sel, q_pair, jnp.zeros_like(q_pair))
    s = lax.dot_general(qm, kcat, (((1,), (1,)), ((), ())), preferred_element_type=F32)
    return jnp.where(valid, s, -jnp.inf)


def _weighted_values(pr, v_pair, sel):
    return _dot(pr.astype(BF16), jnp.where(sel, v_pair, jnp.ones_like(v_pair)))


def _normalise_pair(res_even, res_odd, even, extra=None):
    num = jnp.where(even, res_even, res_odd)
    den = pltpu.roll(jnp.where(even, res_odd, res_even), HEAD_DIM, 1)
    return num / (den if extra is None else den + extra)


def _with_halo(prev, cur):
    return jnp.concatenate([prev, cur], axis=0)


def _swa_kernel(q_ref, kp_ref, k_ref, vp_ref, v_ref, sink_ref, o_ref, m_st, r_st):
    first_key = jnp.where(pl.program_id(1) > 0, 0, BLOCK)
    band = _band(BLOCK - SWA_BACK, None)
    band_first = _band(BLOCK - SWA_BACK, first_key)
    lane = lax.broadcasted_iota(jnp.int32, (1, LANES), 1)
    even = lane < HEAD_DIM
    n_tiles = SWA_STEP // BLOCK
    for j in range(n_tiles):
        rows = slice(j * BLOCK, (j + 1) * BLOCK)
        valid = band_first if j == 0 else band
        kcat = _with_halo(kp_ref[...] if j == 0 else k_ref[j - 1], k_ref[j])
        vcat = _with_halo(vp_ref[...] if j == 0 else v_ref[j - 1], v_ref[j])
        for p in range(N_HEADS // 2):
            q_pair = q_ref[j, :, p * PAIR:(p + 1) * PAIR]
            for half in range(2):
                h = 2 * p + half
                sel = even if half == 0 else jnp.logical_not(even)
                s = _head_scores(q_pair, kcat, sel, valid)
                m = jnp.maximum(jnp.max(s, axis=1, keepdims=True), sink_ref[h])
                m_st[h, rows, :] = jnp.broadcast_to(m, (BLOCK, LANES))
                r_st[h, rows, :] = _weighted_values(jnp.exp2(s - m), vcat, sel)

    def finish(j, carry):
        rows = pl.ds(pl.multiple_of(j * BLOCK, BLOCK), BLOCK)
        for p in range(N_HEADS // 2):
            sink_pair = jnp.where(even, sink_ref[2 * p], sink_ref[2 * p + 1])
            m_pair = jnp.where(even, m_st[2 * p, rows, :], m_st[2 * p + 1, rows, :])
            out = _normalise_pair(r_st[2 * p, rows, :], r_st[2 * p + 1, rows, :], even,
                                  jnp.exp2(sink_pair - m_pair))
            o_ref[j, :, p * PAIR:(p + 1) * PAIR] = out.astype(o_ref.dtype)
        return carry

    lax.fori_loop(0, n_tiles, finish, 0, unroll=2)


def _swa_attention(q, k, v, sink):
    b, nb = q.shape[:2]
    step = SWA_STEP // BLOCK
    cur = lambda w: pl.BlockSpec((None, step, BLOCK, w), lambda bi, c: (bi, c, 0, 0))
    prev = pl.BlockSpec((None, None, BLOCK, KV_B),
                        lambda bi, c: (bi, jnp.maximum(step * c - 1, 0), 0, 0))
    return pl.pallas_call(
        _swa_kernel,
        grid=(b, nb // step),
        in_specs=[cur(MIX), prev, cur(KV_B), prev, cur(KV_B), pl.BlockSpec(memory_space=pltpu.SMEM)],
        out_specs=cur(MIX),
        out_shape=jax.ShapeDtypeStruct(q.shape, BF16),
        scratch_shapes=[pltpu.VMEM((N_HEADS, SWA_STEP, LANES), F32)] * 2,
        compiler_params=pltpu.CompilerParams(
            dimension_semantics=("arbitrary", "arbitrary"), vmem_limit_bytes=VMEM_LIMIT),
        name="swa",
    )(q, k, k, v, v, sink)


def _dilated_kernel(qn, knp, kn, vnp, vn, q4, k4p, k4, v4p, v4, q16, k16p, k16, v16p, v16,
                    o_ref, m_st, r_st):
    n_pairs = HALF_W // PAIR
    first_key = jnp.where(pl.program_id(1) > 0, 0, BLOCK)
    band = _band(0, None)
    band_first = _band(0, first_key)
    lane = lax.broadcasted_iota(jnp.int32, (1, LANES), 1)
    even = lane < HEAD_DIM

    def tile(q, kcat, vcat, valid, row0, stride, init):
        rows = pl.ds(row0, BLOCK) if stride == 1 else pl.ds(row0, BLOCK, stride=stride)
        for p in range(n_pairs):
            cols = slice(p * PAIR, (p + 1) * PAIR)
            for half in range(2):
                h = 2 * p + half
                sel = even if half == 0 else jnp.logical_not(even)
                s = _head_scores(q[:, cols], kcat[:, cols], sel, valid)
                m_tile = jnp.max(s, axis=1, keepdims=True)
                if init:
                    m_new = jnp.broadcast_to(m_tile, (BLOCK, LANES))
                    res = _weighted_values(jnp.exp2(s - m_tile), vcat[:, cols], sel)
                else:
                    m_old = m_st[h, rows, :]
                    m_new = jnp.maximum(m_old, m_tile)
                    pr = jnp.exp2(s - jnp.concatenate([m_new, m_new], axis=1))
                    res = (_weighted_values(pr, vcat[:, cols], sel)
                           + jnp.exp2(m_old - m_new) * r_st[h, rows, :])
                m_st[h, rows, :] = m_new
                r_st[h, rows, :] = res

    def d16(r, carry):
        tile(q16[r], _with_halo(k16p[r], k16[r]), _with_halo(v16p[r], v16[r]), band_first, r, 16, True)
        return carry

    lax.fori_loop(0, 16, d16, 0, unroll=True)

    tile(qn[0], _with_halo(knp[...], kn[0]), _with_halo(vnp[...], vn[0]), band_first, 0, 1, False)

    def d1(j, carry):
        tile(qn[j], _with_halo(kn[j - 1], kn[j]), _with_halo(vn[j - 1], vn[j]),
             band, pl.multiple_of(j * BLOCK, BLOCK), 1, False)
        return carry

    lax.fori_loop(1, CHUNK // BLOCK, d1, 0, unroll=True)

    def d4(r, carry):
        tile(q4[r, 0], _with_halo(k4p[r], k4[r, 0]), _with_halo(v4p[r], v4[r, 0]), band_first, r, 4, False)
        for j in range(1, CHUNK // (4 * BLOCK)):
            tile(q4[r, j], _with_halo(k4[r, j - 1], k4[r, j]), _with_halo(v4[r, j - 1], v4[r, j]),
                 band, r + 4 * j * BLOCK, 4, False)
        return carry

    lax.fori_loop(0, 4, d4, 0, unroll=True)

    def finish(j, carry):
        rows = pl.ds(pl.multiple_of(j * BLOCK, BLOCK), BLOCK)
        for p in range(n_pairs):
            out = _normalise_pair(r_st[2 * p, rows, :], r_st[2 * p + 1, rows, :], even)
            o_ref[j, :, p * PAIR:(p + 1) * PAIR] = out.astype(o_ref.dtype)
        return carry

    lax.fori_loop(0, CHUNK // BLOCK, finish, 0, unroll=True)


def _dilated_attention(nat, by4, by16):
    b, nb = nat[0].shape[:2]
    per_chunk = CHUNK // BLOCK
    n_chunks = nb // per_chunk

    def specs(dil):
        step = per_chunk // dil
        if dil == 1:
            cur = pl.BlockSpec((None, step, BLOCK, HALF_W), lambda bi, c, hf: (bi, c, 0, hf))
            prev = pl.BlockSpec((None, None, BLOCK, HALF_W),
                                lambda bi, c, hf: (bi, jnp.maximum(step * c - 1, 0), 0, hf))
        else:
            blocks = None if step == 1 else step
            cur = pl.BlockSpec((None, dil, blocks, BLOCK, HALF_W), lambda bi, c, hf: (bi, 0, c, 0, hf))
            prev = pl.BlockSpec((None, dil, None, BLOCK, HALF_W),
                                lambda bi, c, hf: (bi, 0, jnp.maximum(step * c - 1, 0), 0, hf))
        return cur, prev

    args, in_specs = [], []
    for (q, k, v), dil in ((nat, 1), (by4, 4), (by16, 16)):
        cur, prev = specs(dil)
        args += [q, k, k, v, v]
        in_specs += [cur, prev, cur, prev, cur]
    return pl.pallas_call(
        _dilated_kernel,
        grid=(b, n_chunks, MIX // HALF_W),
        in_specs=in_specs,
        out_specs=specs(1)[0],
        out_shape=jax.ShapeDtypeStruct(nat[0].shape, BF16),
        scratch_shapes=[pltpu.VMEM((HALF_W // HEAD_DIM, CHUNK, LANES), F32),
                        pltpu.VMEM((HALF_W // HEAD_DIM, CHUNK, LANES), F32)],
        compiler_params=pltpu.CompilerParams(
            dimension_semantics=("arbitrary", "arbitrary", "arbitrary"),
            vmem_limit_bytes=VMEM_LIMIT),
        name="dilated",
    )(*args)


def _route(lt, tri, base):
    n = lt.shape[1]
    row8 = lax.broadcasted_iota(jnp.int32, (SUBLANES, n), 0)
    far = jnp.int32(LANES)
    neg = -jnp.inf

    def first_argmax(vals):
        mx = jnp.max(vals, axis=0, keepdims=True)
        idx = jnp.min(jnp.where(vals == mx, row8, far), axis=0, keepdims=True)
        return mx, idx

    gl = jnp.where(row8 < N_GROUPS, lt[:SUBLANES, :], neg)
    gmax, gidx = first_argmax(gl)
    g_top = 1.0 / jnp.sum(jnp.exp(gl - gmax), axis=0, keepdims=True)

    el = lt[EXPERT_LANE0:EXPERT_LANE0 + EXPERTS_PER_GROUP, :]
    for g in range(1, N_GROUPS):
        r0 = EXPERT_LANE0 + g * EXPERTS_PER_GROUP
        el = jnp.where(gidx == g, lt[r0:r0 + EXPERTS_PER_GROUP, :], el)
    m1, i1 = first_argmax(el)
    m2, i2 = first_argmax(jnp.where(row8 == i1, neg, el))
    t2 = jnp.exp(m2 - m1)
    gate0 = g_top / (1.0 + t2)
    gate1 = g_top * t2 / (1.0 + t2)
    e0 = gidx * EXPERTS_PER_GROUP + i1
    e1 = gidx * EXPERTS_PER_GROUP + i2

    row_e = lax.broadcasted_iota(jnp.int32, (N_EXPERTS, n), 0)
    oh0 = row_e == e0
    oh1 = row_e == e1
    cum = _dot(jnp.concatenate([oh0.astype(BF16), oh1.astype(BF16)], axis=0), tri)
    oh0 = oh0.astype(F32)
    oh1 = oh1.astype(F32)
    tot0 = jnp.sum(oh0, axis=1, keepdims=True)
    tot1 = jnp.sum(oh1, axis=1, keepdims=True)
    base_n = jnp.concatenate([base] * (n // LANES), axis=1)
    rank0 = jnp.sum(oh0 * (cum[:N_EXPERTS] + base_n), axis=0, keepdims=True)
    rank1 = jnp.sum(oh1 * (cum[N_EXPERTS:] + (base_n + tot0)), axis=0, keepdims=True)

    meta_t = jnp.zeros((SUBLANES, n), F32)
    for r, val in ((META_E0, e0.astype(F32)), (META_E1, e1.astype(F32)), (META_G0, gate0),
                   (META_G1, gate1), (META_R0, rank0), (META_R1, rank1)):
        meta_t = jnp.where(row8 == r, val, meta_t)
    return meta_t, base + (tot0 + tot1)


def _out_router_kernel(oa_ref, ob_ref, x_ref, ga_ref, gb_ref, wo_ref, gf_ref, wr_ref, tri_ref,
                       x1_ref, h_ref, meta_ref, meta_t_ref, cnt_ref, carry_ref):
    @pl.when(pl.program_id(0) == 0)
    def _():
        carry_ref[...] = jnp.zeros_like(carry_ref)

    tm, d = x_ref.shape
    tok = d // (2 * LANES)
    blocks = [slice(r0, r0 + ROUTE_BLOCK) for r0 in range(0, tm, ROUTE_BLOCK)]
    attn = []
    for rows in blocks:
        na = _rms_rows(oa_ref[rows, :].astype(F32)) * ga_ref[...]
        nb = _rms_rows(ob_ref[rows, :].astype(F32)) * gb_ref[...]
        attn.append(_dot(jnp.concatenate([na, nb], axis=1).astype(BF16), wo_ref[...]))
    logits = []
    for rows, delta in zip(blocks, attn):
        x1 = x_ref[rows, :] + delta
        x1_ref[rows, :] = x1
        h = _rms_rows(x1) * gf_ref[...]
        _store_token_tiles(h_ref.at[rows.start * tok:rows.stop * tok], _pack_bf16_pairs(h))
        h_hi, h_lo = _split2(h)
        parts = _dot(h_hi, wr_ref[...]) + _dot(h_lo, wr_ref[...])
        logits.append(parts[:, :LANES] + parts[:, LANES:])

    lt = jnp.concatenate(logits, axis=0).T
    meta_t, base = _route(lt, tri_ref[...], carry_ref[...])
    carry_ref[...] = base
    cnt_ref[...] = base
    meta_t_ref[...] = meta_t
    meta_ref[...] = jnp.concatenate([meta_t, jnp.zeros((LANES - SUBLANES, tm), F32)], axis=0).T


def _out_router(oa, ob, x2d, ga, gb, wo_bf16, gf, wr, tri):
    t, d = x2d.shape
    tm = ROW_TILE
    tok = d // (2 * LANES)
    row = lambda i: (i, 0)
    fixed = lambda i: (0, 0)
    return pl.pallas_call(
        _out_router_kernel,
        grid=(t // tm,),
        in_specs=[
            pl.BlockSpec((tm, MIX), row),
            pl.BlockSpec((tm, MIX), row),
            pl.BlockSpec((tm, d), row),
            pl.BlockSpec((1, MIX), fixed),
            pl.BlockSpec((1, MIX), fixed),
            pl.BlockSpec((2 * MIX, d), fixed),
            pl.BlockSpec((1, d), fixed),
            pl.BlockSpec((d, 2 * LANES), fixed),
            pl.BlockSpec((tm, tm), fixed),
        ],
        out_specs=[
            pl.BlockSpec((tm, d), row),
            pl.BlockSpec((tm * tok, LANES), row),
            pl.BlockSpec((tm, LANES), row),
            pl.BlockSpec((None, SUBLANES, tm), lambda i: (i, 0, 0)),
            pl.BlockSpec((N_EXPERTS, LANES), fixed),
        ],
        out_shape=[
            jax.ShapeDtypeStruct((t, d), F32),
            jax.ShapeDtypeStruct((t * tok, LANES), jnp.uint32),
            jax.ShapeDtypeStruct((t, LANES), F32),
            jax.ShapeDtypeStruct((t // tm, SUBLANES, tm), F32),
            jax.ShapeDtypeStruct((N_EXPERTS, LANES), F32),
        ],
        scratch_shapes=[pltpu.VMEM((N_EXPERTS, LANES), F32)],
        compiler_params=pltpu.CompilerParams(
            dimension_semantics=("arbitrary",), vmem_limit_bytes=VMEM_LIMIT),
        name="out_router",
    )(oa, ob, x2d, ga, gb, wo_bf16, gf, wr, tri)


def _token_copy(src, i, dst, j, tok, sem):
    return pltpu.make_async_copy(src.at[pl.ds(pl.multiple_of(i * tok, tok), tok)],
                                 dst.at[pl.ds(pl.multiple_of(j * tok, tok), tok)], sem)


def _dispatch_kernel(slot_ref, h_ref, xs_in_ref, xs_ref, sem, *, tok):
    del xs_in_ref
    n_sub = slot_ref.shape[0]
    tm = slot_ref.shape[2] // TOP_K

    for r in range(n_sub):
        def start(blk, c, src=h_ref.at[pl.ds(r * tm * tok, tm * tok)], slots=slot_ref.at[r]):
            for u in range(DMA_UNROLL):
                i = blk * DMA_UNROLL + u
                for k in range(TOP_K):
                    _token_copy(src, i, xs_ref, slots[0, k * tm + i], tok, sem).start(priority=k)
            return c

        lax.fori_loop(0, tm // DMA_UNROLL, start, 0)
    for k in range(TOP_K):
        pltpu.make_async_copy(h_ref, xs_ref.at[pl.ds(0, n_sub * tm * tok)], sem).wait()


def _dispatch(slots, h_tok, xs_buf, tok):
    tm = ROW_TILE
    n_sub = DISPATCH_TILES
    n_tiles = h_tok.shape[0] // (n_sub * tm * tok)
    return pl.pallas_call(
        functools.partial(_dispatch_kernel, tok=tok),
        grid=(n_tiles,),
        in_specs=[
            pl.BlockSpec((n_sub, 1, TOP_K * tm), lambda i: (i, 0, 0), memory_space=pltpu.SMEM),
            pl.BlockSpec((n_sub * tm * tok, LANES), lambda i: (i, 0)),
            pl.BlockSpec(memory_space=pl.ANY),
        ],
        out_specs=pl.BlockSpec(memory_space=pl.ANY),
        out_shape=jax.ShapeDtypeStruct(xs_buf.shape, xs_buf.dtype),
        scratch_shapes=[pltpu.SemaphoreType.DMA(())],
        input_output_aliases={2: 0},
        compiler_params=pltpu.CompilerParams(
            dimension_semantics=("arbitrary",), vmem_limit_bytes=VMEM_LIMIT),
        name="dispatch",
    )(slots, h_tok, xs_buf)


def _expert_kernel(te_ref, nv_ref, xs_ref, wgu_ref, wdn_ref, ys_ref, wgu_bf, wdn_bf):
    i = pl.program_id(0)
    d = wgu_ref.shape[0]
    tok = d // (2 * LANES)
    tg = xs_ref.shape[0] // tok

    @pl.when((i == 0) | (te_ref[i] != te_ref[jnp.maximum(i - 1, 0)]))
    def _():
        wgu_bf[...] = wgu_ref[...].astype(BF16)
        wdn_bf[...] = wdn_ref[...].astype(BF16)

    @pl.when(i < nv_ref[0])
    def _():
        xs = _unpack_bf16_pairs(_load_token_tiles(xs_ref, tg, tok))
        gu = _dot(xs.astype(BF16), wgu_bf[...])
        g = gu[:, :D_EXPERT]
        a = (g * (1.0 / (1.0 + jnp.exp(-g)))) * gu[:, D_EXPERT:]
        _store_token_tiles(ys_ref, _pack_bf16_pairs(_dot(a.astype(BF16), wdn_bf[...])))

    @pl.when(i >= nv_ref[0])
    def _():
        ys_ref[...] = jnp.zeros_like(ys_ref)


def _experts(tile_expert, n_valid, xs, wgu, wdn, layer):
    d = wgu.shape[2]
    tok = d // (2 * LANES)
    tg = GROUP_TILE
    grid_spec = pltpu.PrefetchScalarGridSpec(
        num_scalar_prefetch=2,
        grid=(xs.shape[0] // (tg * tok),),
        in_specs=[
            pl.BlockSpec((tg * tok, LANES), lambda i, te, nv: (i, 0)),
            pl.BlockSpec((None, None, d, 2 * D_EXPERT), lambda i, te, nv: (layer, te[i], 0, 0)),
            pl.BlockSpec((None, None, D_EXPERT, d), lambda i, te, nv: (layer, te[i], 0, 0)),
        ],
        out_specs=pl.BlockSpec((tg * tok, LANES), lambda i, te, nv: (i, 0)),
        scratch_shapes=[pltpu.VMEM((d, 2 * D_EXPERT), BF16), pltpu.VMEM((D_EXPERT, d), BF16)],
    )
    return pl.pallas_call(
        _expert_kernel,
        grid_spec=grid_spec,
        out_shape=jax.ShapeDtypeStruct(xs.shape, xs.dtype),
        compiler_params=pltpu.CompilerParams(
            dimension_semantics=("arbitrary",), vmem_limit_bytes=VMEM_LIMIT),
        name="experts",
    )(tile_expert, n_valid, xs, wgu, wdn)


def _combine_kernel(slot_ref, slot_next_ref, x_ref, meta_ref, ys_ref, out_ref, buf, sems):
    step = pl.program_id(0)
    tm, d = x_ref.shape
    tok = d // (2 * LANES)

    def gather(slots, half):
        def start(blk, c):
            for u in range(DMA_UNROLL):
                i = blk * DMA_UNROLL + u
                for k in range(TOP_K):
                    _token_copy(ys_ref, slots[0, k * tm + i], buf.at[half, k], i, tok,
                                sems.at[half]).start(priority=k)
            return c

        lax.fori_loop(0, tm // DMA_UNROLL, start, 0)

    cur = step % 2

    @pl.when(step == 0)
    def _():
        gather(slot_ref, 0)

    @pl.when(step + 1 < pl.num_programs(0))
    def _():
        gather(slot_next_ref, 1 - cur)

    for k in range(TOP_K):
        pltpu.make_async_copy(ys_ref.at[pl.ds(0, tm * tok)], buf.at[cur, k], sems.at[cur]).wait()
    meta = meta_ref[...]
    lane = lax.broadcasted_iota(jnp.int32, (1, LANES), 1)
    g0 = _lane_col(meta, lane, META_G0, 0.0, jnp.sum)
    g1 = _lane_col(meta, lane, META_G1, 0.0, jnp.sum)
    y = (g0 * _unpack_bf16_pairs(_load_token_tiles(buf.at[cur, 0], tm, tok))
         + g1 * _unpack_bf16_pairs(_load_token_tiles(buf.at[cur, 1], tm, tok)))
    out_ref[...] = x_ref[...] + y


def _combine(slots, x1, meta, ys):
    t, d = x1.shape
    tm = ROW_TILE
    tok = d // (2 * LANES)
    n_tiles = t // tm
    return pl.pallas_call(
        _combine_kernel,
        grid=(n_tiles,),
        in_specs=[
            pl.BlockSpec((None, 1, TOP_K * tm), lambda i: (i, 0, 0), memory_space=pltpu.SMEM),
            pl.BlockSpec((None, 1, TOP_K * tm), lambda i: (jnp.minimum(i + 1, n_tiles - 1), 0, 0),
                         memory_space=pltpu.SMEM),
            pl.BlockSpec((tm, d), lambda i: (i, 0)),
            pl.BlockSpec((tm, LANES), lambda i: (i, 0)),
            pl.BlockSpec(memory_space=pl.ANY),
        ],
        out_specs=pl.BlockSpec((tm, d), lambda i: (i, 0)),
        out_shape=jax.ShapeDtypeStruct((t, d), F32),
        scratch_shapes=[pltpu.VMEM((2, TOP_K, tm * tok, LANES), ys.dtype),
                        pltpu.SemaphoreType.DMA((2,))],
        compiler_params=pltpu.CompilerParams(
            dimension_semantics=("arbitrary",), vmem_limit_bytes=VMEM_LIMIT),
        name="combine",
    )(slots, slots, x1, meta, ys)


def _rope_expand_matrix():
    src = jnp.arange(LANES)[:, None]
    dst = jnp.arange(LANES)[None, :]
    in_head = dst % HEAD_DIM
    cos_m = (src < ROPE_HALF) & (in_head < ROPE_DIM) & (in_head % ROPE_HALF == src)
    is_sin = (src >= ROPE_HALF) & (src < ROPE_DIM)
    sin_lo = is_sin & (in_head >= ROPE_HALF) & (in_head < ROPE_DIM) & (in_head - ROPE_HALF == src - ROPE_HALF)
    sin_hi = is_sin & (in_head < ROPE_HALF) & (in_head == src - ROPE_HALF)
    e = jnp.concatenate([cos_m.astype(F32), sin_lo.astype(F32), -sin_hi.astype(F32)], axis=1)
    return jnp.concatenate([e, e, e], axis=0).astype(BF16)


def _block_diag_ones():
    i = jnp.arange(2 * LANES)
    return (i[:, None] // HEAD_DIM == i[None, :] // HEAD_DIM).astype(BF16)


def _tile4(g):
    return jnp.tile(g.astype(F32), 2 * LANES // HEAD_DIM)


_B_HEAD_ORDER = tuple(h for p in range(N_HEADS // 2) for h in (p, p + N_HEADS // 2))


def _b_cols():
    order = jnp.asarray(_B_HEAD_ORDER)
    return (order[:, None] * HEAD_DIM + jnp.arange(HEAD_DIM)[None, :]).reshape(-1)


def kernel(x, positions, attn_norm, w_in, q_norm_a, k_norm_a, q_norm_b, k_norm_b, sinks_b,
           out_norm_a, out_norm_b, w_out, ffn_norm, w_router_group, w_router_expert,
           w_gate_up, w_down):
    b, s, d = x.shape
    t = b * s
    depth = w_in.shape[0]
    tok = d // (2 * LANES)
    assert s % CHUNK == 0 and s % ROW_TILE == 0 and s % SWA_STEP == 0
    assert ROW_TILE % (2 * SUBLANES * max(DILATIONS)) == 0 and ROW_TILE % ROUTE_BLOCK == 0 and ROW_TILE % PROJ_BLOCK == 0
    assert d == 2 * MIX and d % (2 * LANES) == 0 and t % (DISPATCH_TILES * ROW_TILE) == 0

    inv_freq = ROPE_THETA ** (-jnp.arange(0, ROPE_DIM, 2, dtype=F32) / ROPE_DIM)
    ang = positions.astype(F32)[None, :, :] * inv_freq[:, None, None]
    cos_sin = jnp.moveaxis(jnp.concatenate([jnp.cos(ang), jnp.sin(ang)], axis=0), 0, -1)
    rope_tab = jnp.pad(cos_sin, ((0, 0), (0, 0), (0, LANES - ROPE_DIM))).reshape(t, LANES)
    e3 = _rope_expand_matrix()
    bd = _block_diag_ones()
    row_i = jnp.arange(ROW_TILE)
    tri = (row_i[:, None] < row_i[None, :]).astype(BF16)
    bcols = _b_cols()
    q_scale = HEAD_DIM ** -0.5 * LOG2E
    n_sorted = t * TOP_K + N_EXPERTS * GROUP_TILE
    n_tiles = n_sorted // GROUP_TILE
    xs_buf = jnp.zeros((n_sorted * tok, LANES), jnp.uint32)
    nb = s // BLOCK

    x2d = x.reshape(t, d)
    for l in range(depth):
        qb0 = 3 * MIX
        w_l = w_in[l]
        w_l = jnp.concatenate([w_l[:, :qb0], w_l[:, qb0:qb0 + MIX][:, bcols], w_l[:, qb0 + MIX:]], axis=1)
        head_gains = jnp.stack([_tile4(q_norm_a[l]) * q_scale, _tile4(k_norm_a[l]),
                                _tile4(q_norm_b[l]) * q_scale, _tile4(k_norm_b[l])])
        sink = sinks_b[l][jnp.asarray(_B_HEAD_ORDER)].astype(F32) * LOG2E
        gb_perm = out_norm_b[l][bcols]
        wo_l = jnp.concatenate([w_out[l][:MIX], w_out[l][MIX:][bcols]], axis=0).astype(BF16)
        wr = jnp.concatenate(
            [w_router_group[l], jnp.zeros((d, EXPERT_LANE0 - N_GROUPS), F32), w_router_expert[l],
             jnp.zeros((d, LANES - EXPERT_LANE0 - N_EXPERTS), F32)], axis=1)
        wr_hi = wr.astype(BF16)
        wr_split = jnp.concatenate([wr_hi, (wr - wr_hi.astype(F32)).astype(BF16)], axis=1)

        (qa, ka, va, qa4, ka4, va4, qa16, ka16, va16, qb, kb, vb) = _proj(
            x2d, b, attn_norm[l].reshape(1, d), w_l.astype(BF16), rope_tab, e3, bd, head_gains)
        blocks = lambda a: a.reshape(b, nb, BLOCK, a.shape[-1])
        class_blocks = lambda a: a.reshape(b, a.shape[1], a.shape[2] // BLOCK, BLOCK, MIX)
        oa = _dilated_attention(tuple(map(blocks, (qa, ka, va))),
                                tuple(map(class_blocks, (qa4, ka4, va4))),
                                tuple(map(class_blocks, (qa16, ka16, va16))))
        ob = _swa_attention(blocks(qb), blocks(kb), blocks(vb), sink)

        x1, h_tok, meta, meta_t, counts = _out_router(
            oa.reshape(t, MIX), ob.reshape(t, MIX), x2d, out_norm_a[l].reshape(1, MIX),
            gb_perm.reshape(1, MIX), wo_l, ffn_norm[l].reshape(1, d), wr_split, tri)

        cnt = counts[:, 0].astype(jnp.int32)
        padded = ((cnt + GROUP_TILE - 1) // GROUP_TILE) * GROUP_TILE
        ends = jnp.cumsum(padded)
        offs = ends - padded
        eid = meta_t[:, META_E0:META_E1 + 1, :].astype(jnp.int32)
        rank = meta_t[:, META_R0:META_R1 + 1, :].astype(jnp.int32)
        is_expert = eid[..., None] == jnp.arange(N_EXPERTS, dtype=jnp.int32)
        slots = jnp.sum(jnp.where(is_expert, offs, 0), axis=-1) + rank
        slots = slots.reshape(t // ROW_TILE, 1, TOP_K * ROW_TILE)
        tile_start = jnp.arange(n_tiles, dtype=jnp.int32) * GROUP_TILE
        tile_expert = jnp.minimum(
            jnp.sum((tile_start[:, None] >= ends[None, :]).astype(jnp.int32), axis=1), N_EXPERTS - 1)
        n_valid = (ends[-1:] // GROUP_TILE).astype(jnp.int32)

        xs_buf = _dispatch(slots, h_tok, xs_buf, tok)
        ys = _experts(tile_expert, n_valid, xs_buf, w_gate_up, w_down, l)
        x2d = _combine(slots, x1, meta, ys)
    return x2d.reshape(b, s, d)
```

```python
import functools
import math

import jax
import jax.numpy as jnp
from jax import lax
from jax.experimental import pallas as pl
from jax.experimental.pallas import tpu as pltpu

F32 = jnp.float32
BF16 = jnp.bfloat16

HEAD_DIM = 64
N_HEADS = 8
MIX = N_HEADS * HEAD_DIM
N_KV_B = 2
KV_B = N_KV_B * HEAD_DIM
DILATIONS = (1, 4, 16)
BLOCK = 128
CHUNK = BLOCK * max(DILATIONS)
SWA_BACK = 127
SWA_STEP = 8 * BLOCK
ROPE_DIM = HEAD_DIM // 4
ROPE_HALF = ROPE_DIM // 2
ROPE_THETA = 500000.0
N_GROUPS = 4
EXPERTS_PER_GROUP = 8
N_EXPERTS = N_GROUPS * EXPERTS_PER_GROUP
TOP_K = 2
D_EXPERT = 512
EPS = 1e-6
LOG2E = math.log2(math.e)

LANES = 128
SUBLANES = 8
PAIR = 2 * HEAD_DIM
HALF_W = MIX // 2
EXPERT_LANE0 = 32
ROW_TILE = 512
PROJ_BLOCK = 256
ROUTE_BLOCK = 128
GROUP_TILE = 512
DMA_UNROLL = 8
DISPATCH_TILES = 8
VMEM_LIMIT = 48 * 1024 * 1024

META_E0, META_E1, META_G0, META_G1, META_R0, META_R1 = 0, 1, 2, 3, 4, 5


def _split2(a):
    hi = a.astype(BF16)
    lo = (a - hi.astype(F32)).astype(BF16)
    return hi, lo


def _split3(a):
    hi = a.astype(BF16)
    r = a - hi.astype(F32)
    mid = r.astype(BF16)
    lo = (r - mid.astype(F32)).astype(BF16)
    return hi, mid, lo


def _dot(a, b):
    return jnp.dot(a, b, preferred_element_type=F32)


def _rms_rows(a):
    return a * lax.rsqrt(jnp.mean(a * a, axis=-1, keepdims=True) + EPS)


def _lane_col(tile, lane, idx, fill, reduce):
    return reduce(jnp.where(lane == idx, tile, fill), axis=1, keepdims=True)


BF16_BITS = 16
HIGH_HALF = (1 << 32) - (1 << BF16_BITS)
ROUND_HALF = 1 << (BF16_BITS - 1)


def _pack_bf16_pairs(val):
    w = val.shape[1] // 2
    bits = lax.bitcast_convert_type(val, jnp.uint32) + jnp.uint32(ROUND_HALF)
    return (bits[:, :w] >> BF16_BITS) | (bits[:, w:] & jnp.uint32(HIGH_HALF))


def _unpack_bf16_pairs(words):
    lo = lax.bitcast_convert_type(words << BF16_BITS, F32)
    hi = lax.bitcast_convert_type(words & jnp.uint32(HIGH_HALF), F32)
    return jnp.concatenate([lo, hi], axis=1)


def _store_token_tiles(ref, val):
    rows, width = val.shape
    n = width // LANES
    for c in range(n):
        ref[pl.ds(c, rows, stride=n), :] = val[:, c * LANES:(c + 1) * LANES]


def _load_token_tiles(ref, rows, n):
    return jnp.concatenate([ref[pl.ds(c, rows, stride=n), :] for c in range(n)], axis=1)


def _proj_kernel(x_ref, g_ref, w_ref, rope_ref, e3_ref, bd_ref, hg_ref,
                 qa_ref, ka_ref, va_ref, qa4_ref, ka4_ref, va4_ref, qa16_ref, ka16_ref, va16_ref,
                 qb_ref, kb_ref, vb_ref, scr_q, scr_k, scr_v, scr4):
    tm = x_ref.shape[0]
    lane = lax.broadcasted_iota(jnp.int32, (1, LANES), 1)
    not_rope = ((lane & (HEAD_DIM - 1)) >= ROPE_DIM).astype(F32)
    bd = bd_ref[...]
    n_ct = MIX // LANES
    normed_groups = [(c0, min(2 * LANES, col0 + width - c0), g)
                     for col0, width, g in ((0, MIX, 0), (MIX, MIX, 1), (3 * MIX, MIX, 2), (4 * MIX, KV_B, 3))
                     for c0 in range(col0, col0 + width, 2 * LANES)]

    staged = []
    for r0 in range(0, tm, PROJ_BLOCK):
        rows = slice(r0, r0 + PROJ_BLOCK)
        h = _rms_rows(x_ref[rows, :]) * g_ref[...]
        acc = _dot(h.astype(BF16), w_ref[...])
        hi, mid, lo = _split3(rope_ref[rows, :])
        tab = _dot(jnp.concatenate([hi, mid, lo], axis=1), e3_ref[...])
        sums = {}
        for c0, wd, _ in normed_groups:
            a = acc[:, c0:c0 + wd]
            a2_hi, a2_lo = _split2(a * a)
            sums[c0] = _dot(a2_hi, bd[:wd, :wd]) + _dot(a2_lo, bd[:wd, :wd])
        staged.append((rows, acc, tab, sums))

    for rows, acc, tab, sums in staged:
        c = tab[:, :LANES] + not_rope
        s_lo = tab[:, LANES:2 * LANES]
        s_hi = tab[:, 2 * LANES:]
        tiles = {}
        for c0, wd, g in normed_groups:
            y = (acc[:, c0:c0 + wd] * lax.rsqrt(sums[c0] * (1.0 / HEAD_DIM) + EPS)) * hg_ref[g:g + 1, :wd]
            for j in range(wd // LANES):
                yj = y[:, j * LANES:(j + 1) * LANES]
                tiles[c0 + j * LANES] = (yj * c + pltpu.roll(yj, ROPE_HALF, 1) * s_lo
                                         + pltpu.roll(yj, LANES - ROPE_HALF, 1) * s_hi)
        for c0 in range(2 * MIX, 3 * MIX, LANES):
            tiles[c0] = acc[:, c0:c0 + LANES]
        for col0, nat_ref, scr in ((0, qa_ref, scr_q), (MIX, ka_ref, scr_k), (2 * MIX, va_ref, scr_v),
                                   (3 * MIX, qb_ref, None)):
            for ct in range(n_ct):
                tile = tiles[col0 + ct * LANES]
                nat_ref[rows, ct * LANES:(ct + 1) * LANES] = tile.astype(BF16)
                if scr is not None:
                    scr[ct, rows, :] = tile
        kb_ref[rows, :] = tiles[4 * MIX].astype(BF16)
        vb_ref[rows, :] = acc[:, 4 * MIX + KV_B:].astype(BF16)

    def regroup(scr, ref4, ref16):
        q4 = tm // 4
        for r4 in range(4):
            got = [scr[ct, pl.ds(r4, q4, stride=4), :] for ct in range(n_ct)]
            ref4[r4] = jnp.concatenate(got, axis=1).astype(BF16)
            for ct in range(n_ct):
                scr4[ct, r4 * q4:(r4 + 1) * q4, :] = got[ct]
        for r16 in range(16):
            first = (r16 % 4) * q4 + r16 // 4
            got = [scr4[ct, pl.ds(first, tm // 16, stride=4), :] for ct in range(n_ct)]
            ref16[r16] = jnp.concatenate(got, axis=1).astype(BF16)

    regroup(scr_q, qa4_ref, qa16_ref)
    regroup(scr_k, ka4_ref, ka16_ref)
    regroup(scr_v, va4_ref, va16_ref)


def _proj(x2d, b, gain, w_bf16, rope_tab, e3, bd, head_gains):
    t, d = x2d.shape
    s = t // b
    pw = w_bf16.shape[1]
    tm = ROW_TILE
    nt = s // tm
    row = lambda bi, i: (bi * nt + i, 0)
    fixed = lambda bi, i: (0, 0)
    by_class = lambda bi, i: (bi, 0, i, 0)
    nat = lambda w: (jax.ShapeDtypeStruct((t, w), BF16), pl.BlockSpec((tm, w), row))
    grouped = lambda dil: (jax.ShapeDtypeStruct((b, dil, s // dil, MIX), BF16),
                           pl.BlockSpec((None, dil, tm // dil, MIX), by_class))
    outs = [nat(MIX)] * 3 + [grouped(4)] * 3 + [grouped(16)] * 3 + [nat(MIX), nat(KV_B), nat(KV_B)]
    return pl.pallas_call(
        _proj_kernel,
        grid=(b, nt),
        in_specs=[
            pl.BlockSpec((tm, d), row),
            pl.BlockSpec((1, d), fixed),
            pl.BlockSpec((d, pw), fixed),
            pl.BlockSpec((tm, LANES), row),
            pl.BlockSpec(e3.shape, fixed),
            pl.BlockSpec(bd.shape, fixed),
            pl.BlockSpec(head_gains.shape, fixed),
        ],
        out_specs=[o[1] for o in outs],
        out_shape=[o[0] for o in outs],
        scratch_shapes=[pltpu.VMEM((MIX // LANES, tm, LANES), F32)] * 4,
        compiler_params=pltpu.CompilerParams(
            dimension_semantics=("arbitrary", "arbitrary"), vmem_limit_bytes=VMEM_LIMIT),
        name="proj",
    )(x2d, gain, w_bf16, rope_tab, e3, bd, head_gains)


def _band(min_back, first_key):
    qi = lax.broadcasted_iota(jnp.int32, (BLOCK, 2 * BLOCK), 0)
    kj = lax.broadcasted_iota(jnp.int32, (BLOCK, 2 * BLOCK), 1)
    valid = (kj >= qi + min_back) & (kj <= qi + BLOCK)
    return valid if first_key is None else valid & (kj >= first_key)


def _head_scores(q_pair, kcat, sel, valid):
    qm = jnp.where(sel, q_pair, jnp.zeros_like(q_pair))
    s = lax.dot_general(qm, kcat, (((1,), (1,)), ((), ())), preferred_element_type=F32)
    return jnp.where(valid, s, -jnp.inf)


def _weighted_values(pr, v_pair, sel):
    return _dot(pr.astype(BF16), jnp.where(sel, v_pair, jnp.ones_like(v_pair)))


def _normalise_pair(res_even, res_odd, even, extra=None):
    num = jnp.where(even, res_even, res_odd)
    den = pltpu.roll(jnp.where(even, res_odd, res_even), HEAD_DIM, 1)
    return num / (den if extra is None else den + extra)


def _with_halo(prev, cur):
    return jnp.concatenate([prev, cur], axis=0)


def _swa_kernel(q_ref, kp_ref, k_ref, vp_ref, v_ref, sink_ref, o_ref, m_st, r_st):
    first_key = jnp.where(pl.program_id(1) > 0, 0, BLOCK)
    band = _band(BLOCK - SWA_BACK, None)
    band_first = _band(BLOCK - SWA_BACK, first_key)
    lane = lax.broadcasted_iota(jnp.int32, (1, LANES), 1)
    even = lane < HEAD_DIM
    n_tiles = SWA_STEP // BLOCK
    for j in range(n_tiles):
        rows = slice(j * BLOCK, (j + 1) * BLOCK)
        valid = band_first if j == 0 else band
        kcat = _with_halo(kp_ref[...] if j == 0 else k_ref[j - 1], k_ref[j])
        vcat = _with_halo(vp_ref[...] if j == 0 else v_ref[j - 1], v_ref[j])
        for p in range(N_HEADS // 2):
            q_pair = q_ref[j, :, p * PAIR:(p + 1) * PAIR]
            for half in range(2):
                h = 2 * p + half
                sel = even if half == 0 else jnp.logical_not(even)
                s = _head_scores(q_pair, kcat, sel, valid)
                m = jnp.maximum(jnp.max(s, axis=1, keepdims=True), sink_ref[h])
                m_st[h, rows, :] = jnp.broadcast_to(m, (BLOCK, LANES))
                r_st[h, rows, :] = _weighted_values(jnp.exp2(s - m), vcat, sel)

    def finish(j, carry):
        rows = pl.ds(pl.multiple_of(j * BLOCK, BLOCK), BLOCK)
        for p in range(N_HEADS // 2):
            sink_pair = jnp.where(even, sink_ref[2 * p], sink_ref[2 * p + 1])
            m_pair = jnp.where(even, m_st[2 * p, rows, :], m_st[2 * p + 1, rows, :])
            out = _normalise_pair(r_st[2 * p, rows, :], r_st[2 * p + 1, rows, :], even,
                                  jnp.exp2(sink_pair - m_pair))
            o_ref[j, :, p * PAIR:(p + 1) * PAIR] = out.astype(o_ref.dtype)
        return carry

    lax.fori_loop(0, n_tiles, finish, 0, unroll=2)


def _swa_attention(q, k, v, sink):
    b, nb = q.shape[:2]
    step = SWA_STEP // BLOCK
    cur = lambda w: pl.BlockSpec((None, step, BLOCK, w), lambda bi, c: (bi, c, 0, 0))
    prev = pl.BlockSpec((None, None, BLOCK, KV_B),
                        lambda bi, c: (bi, jnp.maximum(step * c - 1, 0), 0, 0))
    return pl.pallas_call(
        _swa_kernel,
        grid=(b, nb // step),
        in_specs=[cur(MIX), prev, cur(KV_B), prev, cur(KV_B), pl.BlockSpec(memory_space=pltpu.SMEM)],
        out_specs=cur(MIX),
        out_shape=jax.ShapeDtypeStruct(q.shape, BF16),
        scratch_shapes=[pltpu.VMEM((N_HEADS, SWA_STEP, LANES), F32)] * 2,
        compiler_params=pltpu.CompilerParams(
            dimension_semantics=("arbitrary", "arbitrary"), vmem_limit_bytes=VMEM_LIMIT),
        name="swa",
    )(q, k, k, v, v, sink)


def _dilated_kernel(qn, knp, kn, vnp, vn, q4, k4p, k4, v4p, v4, q16, k16p, k16, v16p, v16,
                    o_ref, m_st, r_st):
    n_pairs = HALF_W // PAIR
    first_key = jnp.where(pl.program_id(1) > 0, 0, BLOCK)
    band = _band(0, None)
    band_first = _band(0, first_key)
    lane = lax.broadcasted_iota(jnp.int32, (1, LANES), 1)
    even = lane < HEAD_DIM

    def scores(q, kcat, valid):
        out = []
        for p in range(n_pairs):
            cols = slice(p * PAIR, (p + 1) * PAIR)
            for half in range(2):
                sel = even if half == 0 else jnp.logical_not(even)
                out.append(_head_scores(q[:, cols], kcat[:, cols], sel, valid))
        return out

    def merge(s_heads, vcat, row0, stride, init):
        rows = pl.ds(row0, BLOCK) if stride == 1 else pl.ds(row0, BLOCK, stride=stride)
        for h, s in enumerate(s_heads):
            cols = slice((h // 2) * PAIR, (h // 2 + 1) * PAIR)
            sel = even if h % 2 == 0 else jnp.logical_not(even)
            m_tile = jnp.max(s, axis=1, keepdims=True)
            if init:
                m_new = jnp.broadcast_to(m_tile, (BLOCK, LANES))
                res = _weighted_values(jnp.exp2(s - m_tile), vcat[:, cols], sel)
            else:
                m_old = m_st[h, rows, :]
                m_new = jnp.maximum(m_old, m_tile)
                pr = jnp.exp2(s - jnp.concatenate([m_new, m_new], axis=1))
                res = (_weighted_values(pr, vcat[:, cols], sel)
                       + jnp.exp2(m_old - m_new) * r_st[h, rows, :])
            m_st[h, rows, :] = m_new
            r_st[h, rows, :] = res

    tiles = [(q16[r], (k16p[r], k16[r]), (v16p[r], v16[r]), band_first, r, 16, True) for r in range(16)]
    tiles.append((qn[0], (knp[...], kn[0]), (vnp[...], vn[0]), band_first, 0, 1, False))
    tiles += [(qn[j], (kn[j - 1], kn[j]), (vn[j - 1], vn[j]), band, j * BLOCK, 1, False)
              for j in range(1, CHUNK // BLOCK)]
    for r in range(4):
        tiles.append((q4[r, 0], (k4p[r], k4[r, 0]), (v4p[r], v4[r, 0]), band_first, r, 4, False))
        tiles += [(q4[r, j], (k4[r, j - 1], k4[r, j]), (v4[r, j - 1], v4[r, j]), band,
                   r + 4 * j * BLOCK, 4, False) for j in range(1, CHUNK // (4 * BLOCK))]

    pending = None
    for q, ks, vs, valid, row0, stride, init in tiles:
        s_heads = scores(q, _with_halo(*ks), valid)
        if pending is not None:
            merge(*pending)
        pending = (s_heads, _with_halo(*vs), row0, stride, init)
    merge(*pending)

    def finish(j, carry):
        rows = pl.ds(pl.multiple_of(j * BLOCK, BLOCK), BLOCK)
        for p in range(n_pairs):
            out = _normalise_pair(r_st[2 * p, rows, :], r_st[2 * p + 1, rows, :], even)
            o_ref[j, :, p * PAIR:(p + 1) * PAIR] = out.astype(o_ref.dtype)
        return carry

    lax.fori_loop(0, CHUNK // BLOCK, finish, 0, unroll=4)


def _dilated_attention(nat, by4, by16):
    b, nb = nat[0].shape[:2]
    per_chunk = CHUNK // BLOCK
    n_chunks = nb // per_chunk

    def specs(dil):
        step = per_chunk // dil
        if dil == 1:
            cur = pl.BlockSpec((None, step, BLOCK, HALF_W), lambda bi, c, hf: (bi, c, 0, hf))
            prev = pl.BlockSpec((None, None, BLOCK, HALF_W),
                                lambda bi, c, hf: (bi, jnp.maximum(step * c - 1, 0), 0, hf))
        else:
            blocks = None if step == 1 else step
            cur = pl.BlockSpec((None, dil, blocks, BLOCK, HALF_W), lambda bi, c, hf: (bi, 0, c, 0, hf))
            prev = pl.BlockSpec((None, dil, None, BLOCK, HALF_W),
                                lambda bi, c, hf: (bi, 0, jnp.maximum(step * c - 1, 0), 0, hf))
        return cur, prev

    args, in_specs = [], []
    for (q, k, v), dil in ((nat, 1), (by4, 4), (by16, 16)):
        cur, prev = specs(dil)
        args += [q, k, k, v, v]
        in_specs += [cur, prev, cur, prev, cur]
    return pl.pallas_call(
        _dilated_kernel,
        grid=(b, n_chunks, MIX // HALF_W),
        in_specs=in_specs,
        out_specs=specs(1)[0],
        out_shape=jax.ShapeDtypeStruct(nat[0].shape, BF16),
        scratch_shapes=[pltpu.VMEM((HALF_W // HEAD_DIM, CHUNK, LANES), F32),
                        pltpu.VMEM((HALF_W // HEAD_DIM, CHUNK, LANES), F32)],
        compiler_params=pltpu.CompilerParams(
            dimension_semantics=("arbitrary", "arbitrary", "arbitrary"),
            vmem_limit_bytes=VMEM_LIMIT),
        name="dilated",
    )(*args)


def _route(lt, tri, base):
    n = lt.shape[1]
    row8 = lax.broadcasted_iota(jnp.int32, (SUBLANES, n), 0)
    far = jnp.int32(LANES)
    neg = -jnp.inf

    def first_argmax(vals):
        mx = jnp.max(vals, axis=0, keepdims=True)
        idx = jnp.min(jnp.where(vals == mx, row8, far), axis=0, keepdims=True)
        return mx, idx

    gl = jnp.where(row8 < N_GROUPS, lt[:SUBLANES, :], neg)
    gmax, gidx = first_argmax(gl)
    g_top = 1.0 / jnp.sum(jnp.exp(gl - gmax), axis=0, keepdims=True)

    el = lt[EXPERT_LANE0:EXPERT_LANE0 + EXPERTS_PER_GROUP, :]
    for g in range(1, N_GROUPS):
        r0 = EXPERT_LANE0 + g * EXPERTS_PER_GROUP
        el = jnp.where(gidx == g, lt[r0:r0 + EXPERTS_PER_GROUP, :], el)
    m1, i1 = first_argmax(el)
    m2, i2 = first_argmax(jnp.where(row8 == i1, neg, el))
    t2 = jnp.exp(m2 - m1)
    gate0 = g_top / (1.0 + t2)
    gate1 = g_top * t2 / (1.0 + t2)
    e0 = gidx * EXPERTS_PER_GROUP + i1
    e1 = gidx * EXPERTS_PER_GROUP + i2

    row_e = lax.broadcasted_iota(jnp.int32, (N_EXPERTS, n), 0)
    oh0 = row_e == e0
    oh1 = row_e == e1
    cum = _dot(jnp.concatenate([oh0.astype(BF16), oh1.astype(BF16)], axis=0), tri)
    oh0 = oh0.astype(F32)
    oh1 = oh1.astype(F32)
    tot0 = jnp.sum(oh0, axis=1, keepdims=True)
    tot1 = jnp.sum(oh1, axis=1, keepdims=True)
    base_n = jnp.concatenate([base] * (n // LANES), axis=1)
    rank0 = jnp.sum(oh0 * (cum[:N_EXPERTS] + base_n), axis=0, keepdims=True)
    rank1 = jnp.sum(oh1 * (cum[N_EXPERTS:] + (base_n + tot0)), axis=0, keepdims=True)

    meta_t = jnp.zeros((SUBLANES, n), F32)
    for r, val in ((META_E0, e0.astype(F32)), (META_E1, e1.astype(F32)), (META_G0, gate0),
                   (META_G1, gate1), (META_R0, rank0), (META_R1, rank1)):
        meta_t = jnp.where(row8 == r, val, meta_t)
    return meta_t, base + (tot0 + tot1)


def _out_router_kernel(oa_ref, ob_ref, x_ref, ga_ref, gb_ref, wo_ref, gf_ref, wr_ref, tri_ref,
                       x1_ref, h_ref, meta_ref, meta_t_ref, cnt_ref, carry_ref):
    @pl.when(pl.program_id(0) == 0)
    def _():
        carry_ref[...] = jnp.zeros_like(carry_ref)

    tm, d = x_ref.shape
    tok = d // (2 * LANES)
    blocks = [slice(r0, r0 + ROUTE_BLOCK) for r0 in range(0, tm, ROUTE_BLOCK)]
    attn = []
    for rows in blocks:
        na = _rms_rows(oa_ref[rows, :].astype(F32)) * ga_ref[...]
        nb = _rms_rows(ob_ref[rows, :].astype(F32)) * gb_ref[...]
        attn.append(_dot(jnp.concatenate([na, nb], axis=1).astype(BF16), wo_ref[...]))
    logits = []
    for rows, delta in zip(blocks, attn):
        x1 = x_ref[rows, :] + delta
        x1_ref[rows, :] = x1
        h = _rms_rows(x1) * gf_ref[...]
        _store_token_tiles(h_ref.at[rows.start * tok:rows.stop * tok], _pack_bf16_pairs(h))
        h_hi, h_lo = _split2(h)
        parts = _dot(h_hi, wr_ref[...]) + _dot(h_lo, wr_ref[...])
        logits.append(parts[:, :LANES] + parts[:, LANES:])

    lt = jnp.concatenate(logits, axis=0).T
    meta_t, base = _route(lt, tri_ref[...], carry_ref[...])
    carry_ref[...] = base
    cnt_ref[...] = base
    meta_t_ref[...] = meta_t
    meta_ref[...] = jnp.concatenate([meta_t, jnp.zeros((LANES - SUBLANES, tm), F32)], axis=0).T


def _out_router(oa, ob, x2d, ga, gb, wo_bf16, gf, wr, tri):
    t, d = x2d.shape
    tm = ROW_TILE
    tok = d // (2 * LANES)
    row = lambda i: (i, 0)
    fixed = lambda i: (0, 0)
    return pl.pallas_call(
        _out_router_kernel,
        grid=(t // tm,),
        in_specs=[
            pl.BlockSpec((tm, MIX), row),
            pl.BlockSpec((tm, MIX), row),
            pl.BlockSpec((tm, d), row),
            pl.BlockSpec((1, MIX), fixed),
            pl.BlockSpec((1, MIX), fixed),
            pl.BlockSpec((2 * MIX, d), fixed),
            pl.BlockSpec((1, d), fixed),
            pl.BlockSpec((d, 2 * LANES), fixed),
            pl.BlockSpec((tm, tm), fixed),
        ],
        out_specs=[
            pl.BlockSpec((tm, d), row),
            pl.BlockSpec((tm * tok, LANES), row),
            pl.BlockSpec((tm, LANES), row),
            pl.BlockSpec((None, SUBLANES, tm), lambda i: (i, 0, 0)),
            pl.BlockSpec((N_EXPERTS, LANES), fixed),
        ],
        out_shape=[
            jax.ShapeDtypeStruct((t, d), F32),
            jax.ShapeDtypeStruct((t * tok, LANES), jnp.uint32),
            jax.ShapeDtypeStruct((t, LANES), F32),
            jax.ShapeDtypeStruct((t // tm, SUBLANES, tm), F32),
            jax.ShapeDtypeStruct((N_EXPERTS, LANES), F32),
        ],
        scratch_shapes=[pltpu.VMEM((N_EXPERTS, LANES), F32)],
        compiler_params=pltpu.CompilerParams(
            dimension_semantics=("arbitrary",), vmem_limit_bytes=VMEM_LIMIT),
        name="out_router",
    )(oa, ob, x2d, ga, gb, wo_bf16, gf, wr, tri)


def _token_copy(src, i, dst, j, tok, sem):
    return pltpu.make_async_copy(src.at[pl.ds(pl.multiple_of(i * tok, tok), tok)],
                                 dst.at[pl.ds(pl.multiple_of(j * tok, tok), tok)], sem)


def _dispatch_kernel(slot_ref, h_ref, xs_in_ref, xs_ref, sem, *, tok):
    del xs_in_ref
    n_sub = slot_ref.shape[0]
    tm = slot_ref.shape[2] // TOP_K

    for r in range(n_sub):
        def start(blk, c, src=h_ref.at[pl.ds(r * tm * tok, tm * tok)], slots=slot_ref.at[r]):
            for u in range(DMA_UNROLL):
                i = blk * DMA_UNROLL + u
                for k in range(TOP_K):
                    _token_copy(src, i, xs_ref, slots[0, k * tm + i], tok, sem).start(priority=k)
            return c

        lax.fori_loop(0, tm // DMA_UNROLL, start, 0)
    for k in range(TOP_K):
        pltpu.make_async_copy(h_ref, xs_ref.at[pl.ds(0, n_sub * tm * tok)], sem).wait()


def _dispatch(slots, h_tok, xs_buf, tok):
    tm = ROW_TILE
    n_sub = DISPATCH_TILES
    n_tiles = h_tok.shape[0] // (n_sub * tm * tok)
    return pl.pallas_call(
        functools.partial(_dispatch_kernel, tok=tok),
        grid=(n_tiles,),
        in_specs=[
            pl.BlockSpec((n_sub, 1, TOP_K * tm), lambda i: (i, 0, 0), memory_space=pltpu.SMEM),
            pl.BlockSpec((n_sub * tm * tok, LANES), lambda i: (i, 0)),
            pl.BlockSpec(memory_space=pl.ANY),
        ],
        out_specs=pl.BlockSpec(memory_space=pl.ANY),
        out_shape=jax.ShapeDtypeStruct(xs_buf.shape, xs_buf.dtype),
        scratch_shapes=[pltpu.SemaphoreType.DMA(())],
        input_output_aliases={2: 0},
        compiler_params=pltpu.CompilerParams(
            dimension_semantics=("arbitrary",), vmem_limit_bytes=VMEM_LIMIT),
        name="dispatch",
    )(slots, h_tok, xs_buf)


def _expert_kernel(te_ref, nv_ref, xs_ref, wgu_ref, wdn_ref, ys_ref, wgu_bf, wdn_bf):
    i = pl.program_id(0)
    d = wgu_ref.shape[0]
    tok = d // (2 * LANES)
    tg = xs_ref.shape[0] // tok

    @pl.when((i == 0) | (te_ref[i] != te_ref[jnp.maximum(i - 1, 0)]))
    def _():
        wgu_bf[...] = wgu_ref[...].astype(BF16)
        wdn_bf[...] = wdn_ref[...].astype(BF16)

    @pl.when(i < nv_ref[0])
    def _():
        xs = _unpack_bf16_pairs(_load_token_tiles(xs_ref, tg, tok))
        gu = _dot(xs.astype(BF16), wgu_bf[...])
        g = gu[:, :D_EXPERT]
        a = (g * (1.0 / (1.0 + jnp.exp(-g)))) * gu[:, D_EXPERT:]
        _store_token_tiles(ys_ref, _pack_bf16_pairs(_dot(a.astype(BF16), wdn_bf[...])))

    @pl.when(i >= nv_ref[0])
    def _():
        ys_ref[...] = jnp.zeros_like(ys_ref)


def _experts(tile_expert, n_valid, xs, wgu, wdn, layer):
    d = wgu.shape[2]
    tok = d // (2 * LANES)
    tg = GROUP_TILE
    grid_spec = pltpu.PrefetchScalarGridSpec(
        num_scalar_prefetch=2,
        grid=(xs.shape[0] // (tg * tok),),
        in_specs=[
            pl.BlockSpec((tg * tok, LANES), lambda i, te, nv: (i, 0)),
            pl.BlockSpec((None, None, d, 2 * D_EXPERT), lambda i, te, nv: (layer, te[i], 0, 0)),
            pl.BlockSpec((None, None, D_EXPERT, d), lambda i, te, nv: (layer, te[i], 0, 0)),
        ],
        out_specs=pl.BlockSpec((tg * tok, LANES), lambda i, te, nv: (i, 0)),
        scratch_shapes=[pltpu.VMEM((d, 2 * D_EXPERT), BF16), pltpu.VMEM((D_EXPERT, d), BF16)],
    )
    return pl.pallas_call(
        _expert_kernel,
        grid_spec=grid_spec,
        out_shape=jax.ShapeDtypeStruct(xs.shape, xs.dtype),
        compiler_params=pltpu.CompilerParams(
            dimension_semantics=("arbitrary",), vmem_limit_bytes=VMEM_LIMIT),
        name="experts",
    )(tile_expert, n_valid, xs, wgu, wdn)


def _combine_kernel(slot_ref, slot_next_ref, x_ref, meta_ref, ys_ref, out_ref, buf, sems):
    step = pl.program_id(0)
    tm, d = x_ref.shape
    tok = d // (2 * LANES)

    def gather(slots, half):
        def start(blk, c):
            for u in range(DMA_UNROLL):
                i = blk * DMA_UNROLL + u
                for k in range(TOP_K):
                    _token_copy(ys_ref, slots[0, k * tm + i], buf.at[half, k], i, tok,
                                sems.at[half]).start(priority=k)
            return c

        lax.fori_loop(0, tm // DMA_UNROLL, start, 0)

    cur = step % 2

    @pl.when(step == 0)
    def _():
        gather(slot_ref, 0)

    @pl.when(step + 1 < pl.num_programs(0))
    def _():
        gather(slot_next_ref, 1 - cur)

    for k in range(TOP_K):
        pltpu.make_async_copy(ys_ref.at[pl.ds(0, tm * tok)], buf.at[cur, k], sems.at[cur]).wait()
    meta = meta_ref[...]
    lane = lax.broadcasted_iota(jnp.int32, (1, LANES), 1)
    g0 = _lane_col(meta, lane, META_G0, 0.0, jnp.sum)
    g1 = _lane_col(meta, lane, META_G1, 0.0, jnp.sum)
    y = (g0 * _unpack_bf16_pairs(_load_token_tiles(buf.at[cur, 0], tm, tok))
         + g1 * _unpack_bf16_pairs(_load_token_tiles(buf.at[cur, 1], tm, tok)))
    out_ref[...] = x_ref[...] + y


def _combine(slots, x1, meta, ys):
    t, d = x1.shape
    tm = ROW_TILE
    tok = d // (2 * LANES)
    n_tiles = t // tm
    return pl.pallas_call(
        _combine_kernel,
        grid=(n_tiles,),
        in_specs=[
            pl.BlockSpec((None, 1, TOP_K * tm), lambda i: (i, 0, 0), memory_space=pltpu.SMEM),
            pl.BlockSpec((None, 1, TOP_K * tm), lambda i: (jnp.minimum(i + 1, n_tiles - 1), 0, 0),
                         memory_space=pltpu.SMEM),
            pl.BlockSpec((tm, d), lambda i: (i, 0)),
            pl.BlockSpec((tm, LANES), lambda i: (i, 0)),
            pl.BlockSpec(memory_space=pl.ANY),
        ],
        out_specs=pl.BlockSpec((tm, d), lambda i: (i, 0)),
        out_shape=jax.ShapeDtypeStruct((t, d), F32),
        scratch_shapes=[pltpu.VMEM((2, TOP_K, tm * tok, LANES), ys.dtype),
                        pltpu.SemaphoreType.DMA((2,))],
        compiler_params=pltpu.CompilerParams(
            dimension_semantics=("arbitrary",), vmem_limit_bytes=VMEM_LIMIT),
        name="combine",
    )(slots, slots, x1, meta, ys)


def _rope_expand_matrix():
    src = jnp.arange(LANES)[:, None]
    dst = jnp.arange(LANES)[None, :]
    in_head = dst % HEAD_DIM
    cos_m = (src < ROPE_HALF) & (in_head < ROPE_DIM) & (in_head % ROPE_HALF == src)
    is_sin = (src >= ROPE_HALF) & (src < ROPE_DIM)
    sin_lo = is_sin & (in_head >= ROPE_HALF) & (in_head < ROPE_DIM) & (in_head - ROPE_HALF == src - ROPE_HALF)
    sin_hi = is_sin & (in_head < ROPE_HALF) & (in_head == src - ROPE_HALF)
    e = jnp.concatenate([cos_m.astype(F32), sin_lo.astype(F32), -sin_hi.astype(F32)], axis=1)
    return jnp.concatenate([e, e, e], axis=0).astype(BF16)


def _block_diag_ones():
    i = jnp.arange(2 * LANES)
    return (i[:, None] // HEAD_DIM == i[None, :] // HEAD_DIM).astype(BF16)


def _tile4(g):
    return jnp.tile(g.astype(F32), 2 * LANES // HEAD_DIM)


_B_HEAD_ORDER = tuple(h for p in range(N_HEADS // 2) for h in (p, p + N_HEADS // 2))


def _b_cols():
    order = jnp.asarray(_B_HEAD_ORDER)
    return (order[:, None] * HEAD_DIM + jnp.arange(HEAD_DIM)[None, :]).reshape(-1)


def kernel(x, positions, attn_norm, w_in, q_norm_a, k_norm_a, q_norm_b, k_norm_b, sinks_b,
           out_norm_a, out_norm_b, w_out, ffn_norm, w_router_group, w_router_expert,
           w_gate_up, w_down):
    b, s, d = x.shape
    t = b * s
    depth = w_in.shape[0]
    tok = d // (2 * LANES)
    assert s % CHUNK == 0 and s % ROW_TILE == 0 and s % SWA_STEP == 0
    assert ROW_TILE % (2 * SUBLANES * max(DILATIONS)) == 0 and ROW_TILE % ROUTE_BLOCK == 0 and ROW_TILE % PROJ_BLOCK == 0
    assert d == 2 * MIX and d % (2 * LANES) == 0 and t % (DISPATCH_TILES * ROW_TILE) == 0

    inv_freq = ROPE_THETA ** (-jnp.arange(0, ROPE_DIM, 2, dtype=F32) / ROPE_DIM)
    ang = positions.astype(F32)[None, :, :] * inv_freq[:, None, None]
    cos_sin = jnp.moveaxis(jnp.concatenate([jnp.cos(ang), jnp.sin(ang)], axis=0), 0, -1)
    rope_tab = jnp.pad(cos_sin, ((0, 0), (0, 0), (0, LANES - ROPE_DIM))).reshape(t, LANES)
    e3 = _rope_expand_matrix()
    bd = _block_diag_ones()
    row_i = jnp.arange(ROW_TILE)
    tri = (row_i[:, None] < row_i[None, :]).astype(BF16)
    bcols = _b_cols()
    q_scale = HEAD_DIM ** -0.5 * LOG2E
    n_sorted = t * TOP_K + N_EXPERTS * GROUP_TILE
    n_tiles = n_sorted // GROUP_TILE
    xs_buf = jnp.zeros((n_sorted * tok, LANES), jnp.uint32)
    nb = s // BLOCK

    x2d = x.reshape(t, d)
    for l in range(depth):
        qb0 = 3 * MIX
        w_l = w_in[l]
        w_l = jnp.concatenate([w_l[:, :qb0], w_l[:, qb0:qb0 + MIX][:, bcols], w_l[:, qb0 + MIX:]], axis=1)
        head_gains = jnp.stack([_tile4(q_norm_a[l]) * q_scale, _tile4(k_norm_a[l]),
                                _tile4(q_norm_b[l]) * q_scale, _tile4(k_norm_b[l])])
        sink = sinks_b[l][jnp.asarray(_B_HEAD_ORDER)].astype(F32) * LOG2E
        gb_perm = out_norm_b[l][bcols]
        wo_l = jnp.concatenate([w_out[l][:MIX], w_out[l][MIX:][bcols]], axis=0).astype(BF16)
        wr = jnp.concatenate(
            [w_router_group[l], jnp.zeros((d, EXPERT_LANE0 - N_GROUPS), F32), w_router_expert[l],
             jnp.zeros((d, LANES - EXPERT_LANE0 - N_EXPERTS), F32)], axis=1)
        wr_hi = wr.astype(BF16)
        wr_split = jnp.concatenate([wr_hi, (wr - wr_hi.astype(F32)).astype(BF16)], axis=1)

        (qa, ka, va, qa4, ka4, va4, qa16, ka16, va16, qb, kb, vb) = _proj(
            x2d, b, attn_norm[l].reshape(1, d), w_l.astype(BF16), rope_tab, e3, bd, head_gains)
        blocks = lambda a: a.reshape(b, nb, BLOCK, a.shape[-1])
        class_blocks = lambda a: a.reshape(b, a.shape[1], a.shape[2] // BLOCK, BLOCK, MIX)
        oa = _dilated_attention(tuple(map(blocks, (qa, ka, va))),
                                tuple(map(class_blocks, (qa4, ka4, va4))),
                                tuple(map(class_blocks, (qa16, ka16, va16))))
        ob = _swa_attention(blocks(qb), blocks(kb), blocks(vb), sink)

        x1, h_tok, meta, meta_t, counts = _out_router(
            oa.reshape(t, MIX), ob.reshape(t, MIX), x2d, out_norm_a[l].reshape(1, MIX),
            gb_perm.reshape(1, MIX), wo_l, ffn_norm[l].reshape(1, d), wr_split, tri)

        cnt = counts[:, 0].astype(jnp.int32)
        padded = ((cnt + GROUP_TILE - 1) // GROUP_TILE) * GROUP_TILE
        ends = jnp.cumsum(padded)
        offs = ends - padded
        eid = meta_t[:, META_E0:META_E1 + 1, :].astype(jnp.int32)
        rank = meta_t[:, META_R0:META_R1 + 1, :].astype(jnp.int32)
        is_expert = eid[..., None] == jnp.arange(N_EXPERTS, dtype=jnp.int32)
        slots = jnp.sum(jnp.where(is_expert, offs, 0), axis=-1) + rank
        slots = slots.reshape(t // ROW_TILE, 1, TOP_K * ROW_TILE)
        tile_start = jnp.arange(n_tiles, dtype=jnp.int32) * GROUP_TILE
        tile_expert = jnp.minimum(
            jnp.sum((tile_start[:, None] >= ends[None, :]).astype(jnp.int32), axis=1), N_EXPERTS - 1)
        n_valid = (ends[-1:] // GROUP_TILE).astype(jnp.int32)

        xs_buf = _dispatch(slots, h_tok, xs_buf, tok)
        ys = _experts(tile_expert, n_valid, xs_buf, w_gate_up, w_down, l)
        x2d = _combine(slots, x1, meta, ys)
    return x2d.reshape(b, s, d)
```
